```python
import math
import jax, jax.numpy as jnp
from jax import lax
import numpy as np

D_MODEL = 2048
BATCH = 1
SEQ = 8192
DEPTH = 4

CHUNK = 64
HEAD_DIM = 128
A_Q_HEADS = 8
A_KV_HEADS = 2
A_WINDOW = 128
A_PREV_CHUNKS = A_WINDOW // CHUNK
B_HEADS = 4
B_QK_DIM = HEAD_DIM // 2
B_V_DIM = HEAD_DIM
B_QBLOCK = 128
C_HEADS = 4
C_PREV_CHUNKS = 8
REL_CLIP = 256
FFN_HIDDEN = int(math.ceil(8 * D_MODEL / 3 / 256)) * 256

ROPE_THETA = 10000.0
LN_EPS = 1e-5
DEEPNORM_ALPHA = (2 * DEPTH) ** 0.25
DEEPNORM_BETA = (8 * DEPTH) ** -0.25
NEG_INF = -1e30

A_Q_W = A_Q_HEADS * HEAD_DIM
A_KV_W = A_KV_HEADS * HEAD_DIM
B_QK_W = B_HEADS * 2 * B_QK_DIM
B_V_W = B_HEADS * B_V_DIM
C_W = C_HEADS * HEAD_DIM
IN_SPLITS = [A_Q_W, A_KV_W, A_KV_W, B_QK_W, B_QK_W, B_V_W, C_W, C_W, C_W]
V_SEGMENTS = (2, 5, 8)
IN_WIDTH = sum(IN_SPLITS)
IN_OFFSETS = [int(o) for o in np.cumsum(IN_SPLITS)[:-1]]

kernel_name = "hybrid_chunk_causal_gated_trunk"


def layer_norm(x, g, b):
    xf = x.astype(jnp.float32)
    mu = jnp.mean(xf, axis=-1, keepdims=True)
    var = jnp.mean(jnp.square(xf - mu), axis=-1, keepdims=True)
    y = (xf - mu) * lax.rsqrt(var + LN_EPS) * g.astype(jnp.float32) + b.astype(jnp.float32)
    return y.astype(x.dtype)


def rope_tables(seq, dim):
    pos = jnp.arange(seq, dtype=jnp.float32)
    inv = 1.0 / (ROPE_THETA ** (jnp.arange(0, dim, 2, dtype=jnp.float32) / dim))
    ang = pos[:, None] * inv[None, :]
    ang = jnp.concatenate([ang, ang], axis=-1)
    return jnp.cos(ang), jnp.sin(ang)


def apply_rope(t, cos, sin):
    half = t.shape[-1] // 2
    t1, t2 = t[..., :half], t[..., half:]
    rot = jnp.concatenate([-t2, t1], axis=-1)
    return t * cos.astype(t.dtype) + rot * sin.astype(t.dtype)


def to_heads(t, h):
    b, s, _ = t.shape
    return t.reshape(b, s, h, -1).transpose(0, 2, 1, 3)


def from_heads(t):
    b, h, s, d = t.shape
    return t.transpose(0, 2, 1, 3).reshape(b, s, h * d)


def band_gather(t, n_prev):
    b, h, s, d = t.shape
    nc = s // CHUNK
    tc = t.reshape(b, h, nc, CHUNK, d)
    tp = jnp.pad(tc, ((0, 0), (0, 0), (n_prev, 0), (0, 0), (0, 0)))
    band = jnp.stack([tp[:, :, i:i + nc] for i in range(n_prev + 1)], axis=3)
    return band.reshape(b, h, nc, (n_prev + 1) * CHUNK, d)


def band_valid(nc, n_prev):
    src = jnp.arange(nc)[:, None] + jnp.arange(n_prev + 1)[None, :] - n_prev
    return jnp.repeat(src >= 0, CHUNK, axis=1)


def window_sink_gqa(q, k, v, sinks):
    b, hq, s, d = q.shape
    hkv = k.shape[1]
    g = hq // hkv
    nc = s // CHUNK
    qc = q.reshape(b, hkv, g, nc, CHUNK, d)
    kb = band_gather(k, A_PREV_CHUNKS)
    vb = band_gather(v, A_PREV_CHUNKS)
    sc = jnp.einsum('bkgcqd,bkcjd->bkgcqj', qc, kb).astype(jnp.float32) * (d ** -0.5)
    valid = band_valid(nc, A_PREV_CHUNKS)
    sc = jnp.where(valid[:, None, :], sc, NEG_INF)
    sink = jnp.broadcast_to(sinks.astype(jnp.float32).reshape(1, hkv, g, 1, 1, 1), sc.shape[:-1] + (1,))
    p = jax.nn.softmax(jnp.concatenate([sc, sink], axis=-1), axis=-1)[..., :-1]
    o = jnp.einsum('bkgcqj,bkcjd->bkgcqd', p.astype(v.dtype), vb)
    return o.reshape(b, hq, s, d)


def diff_attention(q, k, v, lam, sub_g, lam_init):
    b, h, _, s, dq = q.shape
    dv = v.shape[-1]
    nb = s // B_QBLOCK
    qb = jnp.moveaxis(q.reshape(b, h, 2, nb, B_QBLOCK, dq), 3, 0)
    kchunk = jnp.arange(s) // CHUNK
    scale = dq ** -0.5

    def block(args):
        qi, bi = args
        qchunk = (bi * B_QBLOCK + jnp.arange(B_QBLOCK)) // CHUNK
        mask = kchunk[None, :] <= qchunk[:, None]
        sc = jnp.einsum('bhmqd,bhmkd->bhmqk', qi, k).astype(jnp.float32) * scale
        p = jax.nn.softmax(jnp.where(mask, sc, NEG_INF), axis=-1)
        a = p[:, :, 0] - lam * p[:, :, 1]
        return jnp.einsum('bhqk,bhkd->bhqd', a.astype(v.dtype), v)

    o = lax.map(block, (qb, jnp.arange(nb)))
    o = jnp.moveaxis(o, 0, 2).reshape(b, h, s, dv)
    of = o.astype(jnp.float32)
    of = of * lax.rsqrt(jnp.mean(jnp.square(of), axis=-1, keepdims=True) + LN_EPS)
    of = of * sub_g.astype(jnp.float32) * (1.0 - lam_init)
    return of.astype(v.dtype)


def chunk_relbias_attention(q, k, v, rel_bias):
    b, h, s, d = q.shape
    nc = s // CHUNK
    j = (C_PREV_CHUNKS + 1) * CHUNK
    qc = q.reshape(b, h, nc, CHUNK, d)
    kb = band_gather(k, C_PREV_CHUNKS)
    vb = band_gather(v, C_PREV_CHUNKS)
    dist = C_PREV_CHUNKS * CHUNK + jnp.arange(CHUNK)[:, None] - jnp.arange(j)[None, :]
    idx = jnp.clip(dist, -REL_CLIP, REL_CLIP) + REL_CLIP
    bias = rel_bias.astype(jnp.float32)[:, idx]
    sc = jnp.einsum('bhcqd,bhcjd->bhcqj', qc, kb).astype(jnp.float32) * (d ** -0.5) + bias[None, :, None]
    valid = band_valid(nc, C_PREV_CHUNKS)
    sc = jnp.where(valid[:, None, :], sc, NEG_INF)
    p = jax.nn.softmax(sc, axis=-1)
    o = jnp.einsum('bhcqj,bhcjd->bhcqd', p.astype(v.dtype), vb)
    return o.reshape(b, h, s, d)


def setup_inputs(seed: int = 0) -> dict:
    key = jax.random.key(seed)
    ks = jax.random.split(key, 24)
    f32 = jnp.float32
    col_scale = jnp.concatenate([
        jnp.full((w,), DEEPNORM_BETA if i in V_SEGMENTS else 1.0, f32) for i, w in enumerate(IN_SPLITS)])
    nrm = lambda k, shape: jax.random.normal(k, shape, f32)
    return {
        "x": nrm(ks[0], (BATCH, SEQ, D_MODEL)),
        "w_in": nrm(ks[1], (DEPTH, D_MODEL, IN_WIDTH)) * (D_MODEL ** -0.5) * col_scale,
        "sinks": nrm(ks[2], (DEPTH, A_Q_HEADS)) * 0.5,
        "lambda_q1": nrm(ks[3], (DEPTH, B_QK_DIM)) * 0.1,
        "lambda_k1": nrm(ks[4], (DEPTH, B_QK_DIM)) * 0.1,
        "lambda_q2": nrm(ks[5], (DEPTH, B_QK_DIM)) * 0.1,
        "lambda_k2": nrm(ks[6], (DEPTH, B_QK_DIM)) * 0.1,
        "diff_norm_g": 1.0 + 0.02 * nrm(ks[7], (DEPTH, B_V_DIM)),
        "rel_bias": nrm(ks[8], (DEPTH, C_HEADS, 2 * REL_CLIP + 1)) * 0.1,
        "w_br_a": nrm(ks[9], (DEPTH, A_Q_W, D_MODEL)) * (A_Q_W ** -0.5),
        "w_br_b": nrm(ks[10], (DEPTH, B_V_W, D_MODEL)) * (B_V_W ** -0.5),
        "w_br_c": nrm(ks[11], (DEPTH, C_W, D_MODEL)) * (C_W ** -0.5),
        "w_gate": nrm(ks[12], (DEPTH, D_MODEL, 3 * D_MODEL)) * (D_MODEL ** -0.5),
        "b_gate": nrm(ks[13], (DEPTH, 3 * D_MODEL)) * 0.02,
        "w_out": nrm(ks[14], (DEPTH, D_MODEL, D_MODEL)) * (D_MODEL ** -0.5) * DEEPNORM_BETA,
        "ln1_g": 1.0 + 0.02 * nrm(ks[15], (DEPTH, D_MODEL)),
        "ln1_b": 0.02 * nrm(ks[16], (DEPTH, D_MODEL)),
        "w_ffn_in": nrm(ks[17], (DEPTH, D_MODEL, 2 * FFN_HIDDEN)) * (D_MODEL ** -0.5),
        "w_ffn_out": nrm(ks[18], (DEPTH, FFN_HIDDEN, D_MODEL)) * (FFN_HIDDEN ** -0.5) * DEEPNORM_BETA,
        "ln2_g": 1.0 + 0.02 * nrm(ks[19], (DEPTH, D_MODEL)),
        "ln2_b": 0.02 * nrm(ks[20], (DEPTH, D_MODEL)),
    }


def reference(x, w_in, sinks, lambda_q1, lambda_k1, lambda_q2, lambda_k2, diff_norm_g, rel_bias,
              w_br_a, w_br_b, w_br_c, w_gate, b_gate, w_out, ln1_g, ln1_b,
              w_ffn_in, w_ffn_out, ln2_g, ln2_b):
    b, s, _ = x.shape
    cos_a, sin_a = rope_tables(s, HEAD_DIM)
    cos_b, sin_b = rope_tables(s, B_QK_DIM)
    for l in range(DEPTH):
        h = x @ w_in[l]
        aq, ak, av, bq, bk, bv, cq, ck, cv = jnp.split(h, IN_OFFSETS, axis=-1)
        aq = apply_rope(to_heads(aq, A_Q_HEADS), cos_a, sin_a)
        ak = apply_rope(to_heads(ak, A_KV_HEADS), cos_a, sin_a)
        ya = from_heads(window_sink_gqa(aq, ak, to_heads(av, A_KV_HEADS), sinks[l]))
        bq = apply_rope(bq.reshape(b, s, B_HEADS, 2, B_QK_DIM).transpose(0, 2, 3, 1, 4), cos_b, sin_b)
        bk = apply_rope(bk.reshape(b, s, B_HEADS, 2, B_QK_DIM).transpose(0, 2, 3, 1, 4), cos_b, sin_b)
        lam_init = 0.8 - 0.6 * math.exp(-0.3 * l)
        lam = (jnp.exp(jnp.sum(lambda_q1[l].astype(jnp.float32) * lambda_k1[l].astype(jnp.float32)))
               - jnp.exp(jnp.sum(lambda_q2[l].astype(jnp.float32) * lambda_k2[l].astype(jnp.float32)))
               + lam_init)
        yb = from_heads(diff_attention(bq, bk, to_heads(bv, B_HEADS), lam, diff_norm_g[l], lam_init))
        yc = from_heads(chunk_relbias_attention(to_heads(cq, C_HEADS), to_heads(ck, C_HEADS),
                                                to_heads(cv, C_HEADS), rel_bias[l]))
        ga, gb, gc = jnp.split(jax.nn.sigmoid(x @ w_gate[l] + b_gate[l]), 3, axis=-1)
        mix = ga * (ya @ w_br_a[l]) + gb * (yb @ w_br_b[l]) + gc * (yc @ w_br_c[l])
        x = layer_norm(DEEPNORM_ALPHA * x + mix @ w_out[l], ln1_g[l], ln1_b[l])
        f_gate, f_up = jnp.split(x @ w_ffn_in[l], 2, axis=-1)
        x = layer_norm(DEEPNORM_ALPHA * x + (jax.nn.silu(f_gate) * f_up) @ w_ffn_out[l], ln2_g[l], ln2_b[l])
    return x
```

```python
import functools
import math

import jax
import jax.numpy as jnp
import numpy as np
from jax import lax
from jax.experimental import pallas as pl
from jax.experimental.pallas import tpu as pltpu

D_MODEL = 2048
SEQ = 8192
DEPTH = 4
CHUNK = 64
HEAD_DIM = 128
A_Q_HEADS = 8
A_KV_HEADS = 2
A_GROUP = A_Q_HEADS // A_KV_HEADS
B_HEADS = 4
B_QK_DIM = 64
C_HEADS = 4
C_PREV_CHUNKS = 8
REL_CLIP = 256
FFN_HIDDEN = 5632
IN_WIDTH = 4608
ROPE_THETA = 10000.0
LN_EPS = 1e-5
DEEPNORM_ALPHA = (2 * DEPTH) ** 0.25
NEG_INF = -1e30

BF16 = jnp.bfloat16
F32 = jnp.float32

VMEM_LIMIT_BYTES = 56 * 1024 * 1024

AQ_OFF, AK_OFF, AV_OFF = 0, 1024, 1280
BQ_OFF, BK_OFF, BV_OFF = 1536, 2048, 2560
CQ_OFF, CK_OFF, CV_OFF = 3072, 3584, 4096

PROJ_TM, PROJ_TN = 1024, 512
MIX_TM, MIX_TN = 1024, 256
OUT_TM = 512
FFN_TM, FFN_TN = 1024, 512
FFN_OUT_TM = 256
A_TQ = 128
B_TQ = 256
C_TQ = 128
C_KBLOCKS = (C_PREV_CHUNKS * CHUNK) // C_TQ + 1


def _params(*sem):
    return pltpu.CompilerParams(dimension_semantics=sem, vmem_limit_bytes=VMEM_LIMIT_BYTES)


def _rope_a(t, cos, sin_signed):
    return t * cos + pltpu.roll(t, HEAD_DIM // 2, 1) * sin_signed


def _rope_b(t, cos2, sin_lo, sin_hi):
    return t * cos2 + pltpu.roll(t, 96, 1) * sin_lo + pltpu.roll(t, 32, 1) * sin_hi


def _in_proj_kernel(x_ref, w_ref, cosa_ref, sina_ref, cosb_ref, sinb_lo_ref, sinb_hi_ref,
                    o_ref, wb_ref):
    n = pl.program_id(0)
    m = pl.program_id(1)

    @pl.when(m == 0)
    def _():
        wb_ref[...] = w_ref[...].astype(BF16)

    acc = jnp.dot(x_ref[...], wb_ref[...], preferred_element_type=F32)
    heads = PROJ_TN // HEAD_DIM
    a_scale = HEAD_DIM ** -0.5
    b_scale = B_QK_DIM ** -0.5

    def rope_a_cols(lo, hi, scale):
        cos, sin = cosa_ref[...], sina_ref[...]
        for j in range(lo, hi):
            sl = slice(j * HEAD_DIM, (j + 1) * HEAD_DIM)
            r = _rope_a(acc[:, sl], cos, sin)
            if scale != 1.0:
                r = r * scale
            o_ref[:, sl] = r.astype(o_ref.dtype)

    def rope_b_cols(scale):
        cos, lo_, hi_ = cosb_ref[...], sinb_lo_ref[...], sinb_hi_ref[...]
        for j in range(heads):
            sl = slice(j * HEAD_DIM, (j + 1) * HEAD_DIM)
            r = _rope_b(acc[:, sl], cos, lo_, hi_)
            if scale != 1.0:
                r = r * scale
            o_ref[:, sl] = r.astype(o_ref.dtype)

    @pl.when(n < 2)
    def _():
        rope_a_cols(0, heads, a_scale)

    @pl.when(n == 2)
    def _():
        rope_a_cols(0, 2, 1.0)
        o_ref[:, 2 * HEAD_DIM:] = acc[:, 2 * HEAD_DIM:].astype(o_ref.dtype)

    @pl.when(n == 3)
    def _():
        rope_b_cols(b_scale)

    @pl.when(n == 4)
    def _():
        rope_b_cols(1.0)

    @pl.when(n == 6)
    def _():
        o_ref[...] = (acc * a_scale).astype(o_ref.dtype)

    @pl.when((n == 5) | (n == 7) | (n == 8))
    def _():
        o_ref[...] = acc.astype(o_ref.dtype)


def _in_proj(xb, w_in, tabs):
    tm, tn = PROJ_TM, PROJ_TN
    rope_spec = pl.BlockSpec((tm, HEAD_DIM), lambda n, m: (m, 0))
    return pl.pallas_call(
        _in_proj_kernel,
        grid=(IN_WIDTH // tn, SEQ // tm),
        in_specs=[
            pl.BlockSpec((tm, D_MODEL), lambda n, m: (m, 0)),
            pl.BlockSpec((D_MODEL, tn), lambda n, m: (0, n)),
            rope_spec, rope_spec, rope_spec, rope_spec, rope_spec,
        ],
        out_specs=pl.BlockSpec((tm, tn), lambda n, m: (m, n)),
        out_shape=jax.ShapeDtypeStruct((SEQ, IN_WIDTH), BF16),
        scratch_shapes=[pltpu.VMEM((D_MODEL, tn), BF16)],
        compiler_params=_params("arbitrary", "arbitrary"),
        name="in_proj",
    )(xb, w_in, *tabs)


def _attn_a_kernel(sink_ref, q_ref, kp_ref, kc_ref, vp_ref, vc_ref, mask_ref, o_ref):
    i = pl.program_id(0)
    has_prev = i > 0
    mask = mask_ref[...]
    for g in range(A_KV_HEADS):
        ksl = slice(g * HEAD_DIM, (g + 1) * HEAD_DIM)
        q = jnp.concatenate(
            [q_ref[:, (A_GROUP * g + j) * HEAD_DIM:(A_GROUP * g + j + 1) * HEAD_DIM]
             for j in range(A_GROUP)], axis=0)
        dn = (((1,), (1,)), ((), ()))
        s_prev = lax.dot_general(q, kp_ref[:, ksl], dn, preferred_element_type=F32)
        s_cur = lax.dot_general(q, kc_ref[:, ksl], dn, preferred_element_type=F32)
        s_prev = jnp.where(has_prev, s_prev + mask[:, :A_TQ], NEG_INF)
        s_cur = s_cur + mask[:, A_TQ:]
        sink = jnp.concatenate(
            [jnp.full((A_TQ, 1), sink_ref[A_GROUP * g + j], F32) for j in range(A_GROUP)], axis=0)
        mx = jnp.maximum(jnp.maximum(s_prev.max(-1, keepdims=True), s_cur.max(-1, keepdims=True)), sink)
        p_prev = jnp.exp(s_prev - mx)
        p_cur = jnp.exp(s_cur - mx)
        denom = p_prev.sum(-1, keepdims=True) + p_cur.sum(-1, keepdims=True) + jnp.exp(sink - mx)
        o = (jnp.dot(p_prev.astype(BF16), vp_ref[:, ksl], preferred_element_type=F32)
             + jnp.dot(p_cur.astype(BF16), vc_ref[:, ksl], preferred_element_type=F32))
        o = o / denom
        for j in range(A_GROUP):
            hsl = slice((A_GROUP * g + j) * HEAD_DIM, (A_GROUP * g + j + 1) * HEAD_DIM)
            o_ref[:, hsl] = o[j * A_TQ:(j + 1) * A_TQ].astype(o_ref.dtype)


def _attn_a(h, sinks_l, mask_a):
    tq = A_TQ
    kblk, vblk = AK_OFF // 256, AV_OFF // 256
    prev = lambda i: jnp.maximum(i - 1, 0)
    return pl.pallas_call(
        _attn_a_kernel,
        grid=(SEQ // tq,),
        in_specs=[
            pl.BlockSpec(memory_space=pltpu.SMEM),
            pl.BlockSpec((tq, 1024), lambda i: (i, 0)),
            pl.BlockSpec((tq, 256), lambda i: (prev(i), kblk)),
            pl.BlockSpec((tq, 256), lambda i: (i, kblk)),
            pl.BlockSpec((tq, 256), lambda i: (prev(i), vblk)),
            pl.BlockSpec((tq, 256), lambda i: (i, vblk)),
            pl.BlockSpec((A_GROUP * tq, 2 * tq), lambda i: (0, 0)),
        ],
        out_specs=pl.BlockSpec((tq, 1024), lambda i: (i, 0)),
        out_shape=jax.ShapeDtypeStruct((SEQ, 1024), BF16),
        compiler_params=_params("arbitrary"),
        name="attn_a",
    )(sinks_l, h, h, h, h, h, mask_a)


def _attn_b_kernel(lam_ref, g_ref, q_ref, k_ref, v_ref, o_ref, *, lam_init):
    i = pl.program_id(1)
    tq = B_TQ
    q = q_ref[...]
    lane = lax.broadcasted_iota(jnp.int32, q.shape, 1)
    zero = jnp.zeros_like(q)
    qs = jnp.concatenate([jnp.where(lane < B_QK_DIM, q, zero), jnp.where(lane >= B_QK_DIM, q, zero)], axis=0)
    dn = (((1,), (1,)), ((), ()))

    def step(j, carry, masked):
        m, l, acc = carry
        start = pl.multiple_of(j * tq, tq)
        kj = k_ref[pl.ds(start, tq), :]
        vj = v_ref[pl.ds(start, tq), :]
        s = lax.dot_general(qs, kj, dn, preferred_element_type=F32)
        if masked:
            r = lax.broadcasted_iota(jnp.int32, s.shape, 0) % tq
            c = lax.broadcasted_iota(jnp.int32, s.shape, 1)
            s = jnp.where((c // CHUNK) <= (r // CHUNK), s, NEG_INF)
        m_new = jnp.maximum(m, s.max(-1, keepdims=True))
        a = jnp.exp(m - m_new)
        p = jnp.exp(s - m_new)
        l = a * l + p.sum(-1, keepdims=True)
        acc = a * acc + jnp.dot(p.astype(BF16), vj, preferred_element_type=F32)
        return m_new, l, acc

    init = (jnp.full((2 * tq, 1), NEG_INF, F32), jnp.zeros((2 * tq, 1), F32),
            jnp.zeros((2 * tq, HEAD_DIM), F32))
    carry = lax.fori_loop(0, i, lambda j, c: step(j, c, False), init)
    m, l, acc = step(i, carry, True)

    lam = (jnp.exp(jnp.sum(lam_ref[0:1, :] * lam_ref[1:2, :], axis=-1, keepdims=True))
           - jnp.exp(jnp.sum(lam_ref[2:3, :] * lam_ref[3:4, :], axis=-1, keepdims=True))
           + lam_init)
    o_all = acc / l
    o = o_all[:tq] - lam * o_all[tq:]
    o = o * lax.rsqrt(jnp.mean(jnp.square(o), axis=-1, keepdims=True) + LN_EPS)
    o = o * g_ref[...] * (1.0 - lam_init)
    o_ref[...] = o.astype(o_ref.dtype)


def _attn_b(h, lam_vecs, sub_g, lam_init):
    tq = B_TQ
    qb, kb, vb = BQ_OFF // HEAD_DIM, BK_OFF // HEAD_DIM, BV_OFF // HEAD_DIM
    return pl.pallas_call(
        functools.partial(_attn_b_kernel, lam_init=lam_init),
        grid=(B_HEADS, SEQ // tq),
        in_specs=[
            pl.BlockSpec((4, B_QK_DIM), lambda hd, i: (0, 0)),
            pl.BlockSpec((1, HEAD_DIM), lambda hd, i: (0, 0)),
            pl.BlockSpec((tq, HEAD_DIM), lambda hd, i: (i, qb + hd)),
            pl.BlockSpec((SEQ, HEAD_DIM), lambda hd, i: (0, kb + hd)),
            pl.BlockSpec((SEQ, HEAD_DIM), lambda hd, i: (0, vb + hd)),
        ],
        out_specs=pl.BlockSpec((tq, HEAD_DIM), lambda hd, i: (i, hd)),
        out_shape=jax.ShapeDtypeStruct((SEQ, B_HEADS * HEAD_DIM), BF16),
        compiler_params=_params("arbitrary", "arbitrary"),
        name="attn_b",
    )(lam_vecs, sub_g, h, h, h)


def _attn_c_kernel(q_ref, *refs):
    k_refs = refs[:C_KBLOCKS]
    v_refs = refs[C_KBLOCKS:2 * C_KBLOCKS]
    bias_ref = refs[2 * C_KBLOCKS]
    o_ref = refs[2 * C_KBLOCKS + 1]
    i = pl.program_id(0)
    dn = (((1,), (1,)), ((), ()))
    for hd in range(C_HEADS):
        hsl = slice(hd * HEAD_DIM, (hd + 1) * HEAD_DIM)
        q = q_ref[:, hsl]
        s = []
        for b in range(C_KBLOCKS):
            sb = lax.dot_general(q, k_refs[b][:, hsl], dn, preferred_element_type=F32)
            sb = sb + bias_ref[hd, :, b * C_TQ:(b + 1) * C_TQ]
            s.append(jnp.where(i - (C_KBLOCKS - 1) + b >= 0, sb, NEG_INF))
        mx = functools.reduce(jnp.maximum, [sb.max(-1, keepdims=True) for sb in s])
        p = [jnp.exp(sb - mx) for sb in s]
        denom = functools.reduce(lambda a, b_: a + b_, [pb.sum(-1, keepdims=True) for pb in p])
        o = functools.reduce(
            lambda a, b_: a + b_,
            [jnp.dot(p[b].astype(BF16), v_refs[b][:, hsl], preferred_element_type=F32)
             for b in range(C_KBLOCKS)])
        o_ref[:, hsl] = (o / denom).astype(o_ref.dtype)


def _attn_c(h, bias_c):
    tq = C_TQ
    qb, kb, vb = CQ_OFF // 512, CK_OFF // 512, CV_OFF // 512

    def kv_spec(b, col):
        return pl.BlockSpec((tq, 512), lambda i: (jnp.maximum(i - (C_KBLOCKS - 1) + b, 0), col))

    return pl.pallas_call(
        _attn_c_kernel,
        grid=(SEQ // tq,),
        in_specs=([pl.BlockSpec((tq, 512), lambda i: (i, qb))]
                  + [kv_spec(b, kb) for b in range(C_KBLOCKS)]
                  + [kv_spec(b, vb) for b in range(C_KBLOCKS)]
                  + [pl.BlockSpec((C_HEADS, tq, C_KBLOCKS * tq), lambda i: (0, 0, 0))]),
        out_specs=pl.BlockSpec((tq, 512), lambda i: (i, 0)),
        out_shape=jax.ShapeDtypeStruct((SEQ, C_HEADS * HEAD_DIM), BF16),
        compiler_params=_params("arbitrary"),
        name="attn_c",
    )(h, *([h] * (2 * C_KBLOCKS)), bias_c)


def _mix_kernel(x_ref, ya_ref, yb_ref, yc_ref, wga_ref, wgb_ref, wgc_ref, ba_ref, bb_ref, bc_ref,
                wa_ref, wb_ref, wc_ref, o_ref, wg_s, wa_s, wb_s, wc_s):
    m = pl.program_id(1)

    @pl.when(m == 0)
    def _():
        wg_s[0] = wga_ref[...].astype(BF16)
        wg_s[1] = wgb_ref[...].astype(BF16)
        wg_s[2] = wgc_ref[...].astype(BF16)
        wa_s[...] = wa_ref[...].astype(BF16)
        wb_s[...] = wb_ref[...].astype(BF16)
        wc_s[...] = wc_ref[...].astype(BF16)

    x = x_ref[...]

    def branch(idx, b_ref, y_ref, w_s):
        gate = jax.nn.sigmoid(jnp.dot(x, wg_s[idx], preferred_element_type=F32) + b_ref[...])
        return gate * jnp.dot(y_ref[...], w_s[...], preferred_element_type=F32)

    mix = branch(0, ba_ref, ya_ref, wa_s) + branch(1, bb_ref, yb_ref, wb_s) + branch(2, bc_ref, yc_ref, wc_s)
    o_ref[...] = mix.astype(o_ref.dtype)


def _mix(xb, ya, yb, yc, w_gate, b_gate, w_br_a, w_br_b, w_br_c):
    tm, tn = MIX_TM, MIX_TN
    nb = D_MODEL // tn
    row = lambda width: pl.BlockSpec((tm, width), lambda n, m: (m, 0))
    gate_w = lambda k: pl.BlockSpec((D_MODEL, tn), lambda n, m: (0, k * nb + n))
    gate_b = lambda k: pl.BlockSpec((1, tn), lambda n, m: (0, k * nb + n))
    br_w = lambda width: pl.BlockSpec((width, tn), lambda n, m: (0, n))
    return pl.pallas_call(
        _mix_kernel,
        grid=(nb, SEQ // tm),
        in_specs=[row(D_MODEL), row(1024), row(512), row(512),
                  gate_w(0), gate_w(1), gate_w(2), gate_b(0), gate_b(1), gate_b(2),
                  br_w(1024), br_w(512), br_w(512)],
        out_specs=pl.BlockSpec((tm, tn), lambda n, m: (m, n)),
        out_shape=jax.ShapeDtypeStruct((SEQ, D_MODEL), BF16),
        scratch_shapes=[pltpu.VMEM((3, D_MODEL, tn), BF16), pltpu.VMEM((1024, tn), BF16),
                        pltpu.VMEM((512, tn), BF16), pltpu.VMEM((512, tn), BF16)],
        compiler_params=_params("arbitrary", "arbitrary"),
        name="mix",
    )(xb, ya, yb, yc, w_gate, w_gate, w_gate, b_gate, b_gate, b_gate, w_br_a, w_br_b, w_br_c)


def _matmul_ln_kernel(y_ref, w_ref, x_ref, g_ref, b_ref, of_ref, ob_ref):
    z = DEEPNORM_ALPHA * x_ref[...] + jnp.dot(y_ref[...], w_ref[...], preferred_element_type=F32)
    mu = jnp.mean(z, axis=-1, keepdims=True)
    zc = z - mu
    var = jnp.mean(jnp.square(zc), axis=-1, keepdims=True)
    out = zc * lax.rsqrt(var + LN_EPS) * g_ref[...] + b_ref[...]
    of_ref[...] = out
    ob_ref[...] = out.astype(ob_ref.dtype)


def _matmul_ln(y, w_bf16, x, g, b, tm, name):
    k = y.shape[1]
    vec = pl.BlockSpec((1, D_MODEL), lambda m: (0, 0))
    return pl.pallas_call(
        _matmul_ln_kernel,
        grid=(SEQ // tm,),
        in_specs=[
            pl.BlockSpec((tm, k), lambda m: (m, 0)),
            pl.BlockSpec((k, D_MODEL), lambda m: (0, 0), pipeline_mode=pl.Buffered(1)),
            pl.BlockSpec((tm, D_MODEL), lambda m: (m, 0)),
            vec, vec,
        ],
        out_specs=[pl.BlockSpec((tm, D_MODEL), lambda m: (m, 0)),
                   pl.BlockSpec((tm, D_MODEL), lambda m: (m, 0))],
        out_shape=[jax.ShapeDtypeStruct((SEQ, D_MODEL), F32),
                   jax.ShapeDtypeStruct((SEQ, D_MODEL), BF16)],
        compiler_params=_params("arbitrary"),
        name=name,
    )(y, w_bf16, x, g, b)


def _ffn_in_kernel(x_ref, wg_ref, wu_ref, o_ref, wg_s, wu_s):
    m = pl.program_id(1)

    @pl.when(m == 0)
    def _():
        wg_s[...] = wg_ref[...].astype(BF16)
        wu_s[...] = wu_ref[...].astype(BF16)

    x = x_ref[...]
    gate = jnp.dot(x, wg_s[...], preferred_element_type=F32)
    up = jnp.dot(x, wu_s[...], preferred_element_type=F32)
    o_ref[...] = (jax.nn.silu(gate) * up).astype(o_ref.dtype)


def _ffn_in(xb, w_ffn_in):
    tm, tn = FFN_TM, FFN_TN
    nb = FFN_HIDDEN // tn
    return pl.pallas_call(
        _ffn_in_kernel,
        grid=(nb, SEQ // tm),
        in_specs=[
            pl.BlockSpec((tm, D_MODEL), lambda n, m: (m, 0)),
            pl.BlockSpec((D_MODEL, tn), lambda n, m: (0, n)),
            pl.BlockSpec((D_MODEL, tn), lambda n, m: (0, nb + n)),
        ],
        out_specs=pl.BlockSpec((tm, tn), lambda n, m: (m, n)),
        out_shape=jax.ShapeDtypeStruct((SEQ, FFN_HIDDEN), BF16),
        scratch_shapes=[pltpu.VMEM((D_MODEL, tn), BF16), pltpu.VMEM((D_MODEL, tn), BF16)],
        compiler_params=_params("arbitrary", "arbitrary"),
        name="ffn_in",
    )(xb, w_ffn_in, w_ffn_in)


def _rope_tables():
    pos = jnp.arange(SEQ, dtype=F32)

    def cs(dim):
        inv = 1.0 / (ROPE_THETA ** (jnp.arange(0, dim, 2, dtype=F32) / dim))
        ang = pos[:, None] * inv[None, :]
        ang = jnp.concatenate([ang, ang], axis=-1)
        return jnp.cos(ang), jnp.sin(ang)

    cos_a, sin_a = cs(HEAD_DIM)
    half = HEAD_DIM // 2
    sin_a = jnp.concatenate([-sin_a[:, :half], sin_a[:, half:]], axis=-1)
    cos_b, sin_b = cs(B_QK_DIM)
    cos_b2 = jnp.concatenate([cos_b, cos_b], axis=-1)
    sin_b2 = jnp.concatenate([sin_b, sin_b], axis=-1)
    first_half = (np.arange(HEAD_DIM) % B_QK_DIM) < (B_QK_DIM // 2)
    sin_lo = jnp.where(first_half[None, :], -sin_b2, 0.0)
    sin_hi = jnp.where(first_half[None, :], 0.0, sin_b2)
    return cos_a, sin_a, cos_b2, sin_lo, sin_hi


def _mask_a():
    qc = (np.arange(A_TQ) // CHUNK)[:, None]
    kc = (np.arange(2 * A_TQ) // CHUNK)[None, :]
    ok = (kc >= qc) & (kc <= qc + 2)
    m = np.where(ok, 0.0, NEG_INF).astype(np.float32)
    return jnp.asarray(np.tile(m, (A_GROUP, 1)))


def _bias_c(rel_bias_l):
    q = np.arange(C_TQ)[:, None]
    k = np.arange(C_KBLOCKS * C_TQ)[None, :]
    dist = (C_KBLOCKS - 1) * C_TQ + q - k
    idx = np.clip(dist, -REL_CLIP, REL_CLIP) + REL_CLIP
    qc, kc = q // CHUNK, k // CHUNK
    ok = (kc >= qc) & (kc <= qc + C_PREV_CHUNKS)
    bias = rel_bias_l.astype(F32)[:, idx]
    return jnp.where(ok[None], bias, NEG_INF)


def kernel(x, w_in, sinks, lambda_q1, lambda_k1, lambda_q2, lambda_k2, diff_norm_g, rel_bias,
           w_br_a, w_br_b, w_br_c, w_gate, b_gate, w_out, ln1_g, ln1_b,
           w_ffn_in, w_ffn_out, ln2_g, ln2_b):
    assert x.shape == (1, SEQ, D_MODEL)
    tabs = _rope_tables()
    mask_a = _mask_a()
    xf = x.reshape(SEQ, D_MODEL)
    xb = xf.astype(BF16)
    for l in range(DEPTH):
        lam_init = 0.8 - 0.6 * math.exp(-0.3 * l)
        h = _in_proj(xb, w_in[l], tabs)
        ya = _attn_a(h, sinks[l], mask_a)
        lam_vecs = jnp.stack([lambda_q1[l], lambda_k1[l], lambda_q2[l], lambda_k2[l]]).astype(F32)
        yb = _attn_b(h, lam_vecs, diff_norm_g[l].reshape(1, HEAD_DIM), lam_init)
        yc = _attn_c(h, _bias_c(rel_bias[l]))
        mix = _mix(xb, ya, yb, yc, w_gate[l], b_gate[l].reshape(1, -1), w_br_a[l], w_br_b[l], w_br_c[l])
        xf, xb = _matmul_ln(mix, w_out[l].astype(BF16), xf, ln1_g[l].reshape(1, -1),
                            ln1_b[l].reshape(1, -1), OUT_TM, "out_ln")
        f = _ffn_in(xb, w_ffn_in[l])
        xf, xb = _matmul_ln(f, w_ffn_out[l].astype(BF16), xf, ln2_g[l].reshape(1, -1),
                            ln2_b[l].reshape(1, -1), FFN_OUT_TM, "ffn_out")
    return xf.reshape(1, SEQ, D_MODEL)
```

```python
import functools
import math

import jax
import jax.numpy as jnp
import numpy as np
from jax import lax
from jax.experimental import pallas as pl
from jax.experimental.pallas import tpu as pltpu

D_MODEL = 2048
SEQ = 8192
DEPTH = 4
CHUNK = 64
HEAD_DIM = 128
A_Q_HEADS = 8
A_KV_HEADS = 2
A_GROUP = A_Q_HEADS // A_KV_HEADS
B_HEADS = 4
B_QK_DIM = 64
C_HEADS = 4
C_PREV_CHUNKS = 8
REL_CLIP = 256
FFN_HIDDEN = 5632
IN_WIDTH = 4608
ROPE_THETA = 10000.0
LN_EPS = 1e-5
DEEPNORM_ALPHA = (2 * DEPTH) ** 0.25
NEG_INF = -1e30

BF16 = jnp.bfloat16
F32 = jnp.float32

VMEM_LIMIT_BYTES = 56 * 1024 * 1024

AQ_OFF, AK_OFF, AV_OFF = 0, 1024, 1280
BQ_OFF, BK_OFF, BV_OFF = 1536, 2048, 2560
CQ_OFF, CK_OFF, CV_OFF = 3072, 3584, 4096

PROJ_TM, PROJ_TN = 1024, 512
MIX_TM, MIX_TN = 1024, 256
OUT_TM = 512
FFN_TM, FFN_TN = 1024, 512
FFN_OUT_TM = 256
A_TQ = 128
B_TQ = 512
B_TK = 256
B_ONES_ROWS = 16
C_TQ = 128
C_KBLOCKS = (C_PREV_CHUNKS * CHUNK) // C_TQ + 1


def _params(*sem):
    return pltpu.CompilerParams(dimension_semantics=sem, vmem_limit_bytes=VMEM_LIMIT_BYTES)


def _rope_a(t, cos, sin_signed):
    return t * cos + pltpu.roll(t, HEAD_DIM // 2, 1) * sin_signed


def _rope_b(t, cos2, sin_lo, sin_hi):
    return t * cos2 + pltpu.roll(t, 96, 1) * sin_lo + pltpu.roll(t, 32, 1) * sin_hi


def _in_proj_kernel(x_ref, w_ref, cosa_ref, sina_ref, cosb_ref, sinb_lo_ref, sinb_hi_ref,
                    o_ref, wb_ref):
    n = pl.program_id(0)
    m = pl.program_id(1)

    @pl.when(m == 0)
    def _():
        wb_ref[...] = w_ref[...].astype(BF16)

    acc = jnp.dot(x_ref[...], wb_ref[...], preferred_element_type=F32)
    heads = PROJ_TN // HEAD_DIM
    a_scale = HEAD_DIM ** -0.5
    b_scale = B_QK_DIM ** -0.5 * math.log2(math.e)

    def rope_a_cols(lo, hi, scale):
        cos, sin = cosa_ref[...], sina_ref[...]
        for j in range(lo, hi):
            sl = slice(j * HEAD_DIM, (j + 1) * HEAD_DIM)
            r = _rope_a(acc[:, sl], cos, sin)
            if scale != 1.0:
                r = r * scale
            o_ref[:, sl] = r.astype(o_ref.dtype)

    def rope_b_cols(scale):
        cos, lo_, hi_ = cosb_ref[...], sinb_lo_ref[...], sinb_hi_ref[...]
        for j in range(heads):
            sl = slice(j * HEAD_DIM, (j + 1) * HEAD_DIM)
            r = _rope_b(acc[:, sl], cos, lo_, hi_)
            if scale != 1.0:
                r = r * scale
            o_ref[:, sl] = r.astype(o_ref.dtype)

    @pl.when(n < 2)
    def _():
        rope_a_cols(0, heads, a_scale)

    @pl.when(n == 2)
    def _():
        rope_a_cols(0, 2, 1.0)
        o_ref[:, 2 * HEAD_DIM:] = acc[:, 2 * HEAD_DIM:].astype(o_ref.dtype)

    @pl.when(n == 3)
    def _():
        rope_b_cols(b_scale)

    @pl.when(n == 4)
    def _():
        rope_b_cols(1.0)

    @pl.when(n == 6)
    def _():
        o_ref[...] = (acc * a_scale).astype(o_ref.dtype)

    @pl.when((n == 5) | (n == 7) | (n == 8))
    def _():
        o_ref[...] = acc.astype(o_ref.dtype)


def _in_proj(xb, w_in, layer, tabs):
    tm, tn = PROJ_TM, PROJ_TN
    rope_spec = pl.BlockSpec((tm, HEAD_DIM), lambda n, m: (m, 0))
    return pl.pallas_call(
        _in_proj_kernel,
        grid=(IN_WIDTH // tn, SEQ // tm),
        in_specs=[
            pl.BlockSpec((tm, D_MODEL), lambda n, m: (m, 0)),
            pl.BlockSpec((None, D_MODEL, tn), lambda n, m: (layer, 0, n)),
            rope_spec, rope_spec, rope_spec, rope_spec, rope_spec,
        ],
        out_specs=pl.BlockSpec((tm, tn), lambda n, m: (m, n)),
        out_shape=jax.ShapeDtypeStruct((SEQ, IN_WIDTH), BF16),
        scratch_shapes=[pltpu.VMEM((D_MODEL, tn), BF16)],
        compiler_params=_params("arbitrary", "arbitrary"),
        name="in_proj",
    )(xb, w_in, *tabs)


def _attn_a_kernel(sink_ref, q_ref, kp_ref, kc_ref, vp_ref, vc_ref, mask_ref, o_ref, *, layer):
    i = pl.program_id(0)
    has_prev = i > 0
    mask = mask_ref[...]
    for g in range(A_KV_HEADS):
        ksl = slice(g * HEAD_DIM, (g + 1) * HEAD_DIM)
        q = jnp.concatenate(
            [q_ref[:, (A_GROUP * g + j) * HEAD_DIM:(A_GROUP * g + j + 1) * HEAD_DIM]
             for j in range(A_GROUP)], axis=0)
        dn = (((1,), (1,)), ((), ()))
        s_prev = lax.dot_general(q, kp_ref[:, ksl], dn, preferred_element_type=F32)
        s_cur = lax.dot_general(q, kc_ref[:, ksl], dn, preferred_element_type=F32)
        s_prev = jnp.where(has_prev, s_prev + mask[:, :A_TQ], NEG_INF)
        s_cur = s_cur + mask[:, A_TQ:]
        sink = jnp.concatenate(
            [jnp.full((A_TQ, 1), sink_ref[layer, A_GROUP * g + j], F32) for j in range(A_GROUP)], axis=0)
        mx = jnp.maximum(jnp.maximum(s_prev.max(-1, keepdims=True), s_cur.max(-1, keepdims=True)), sink)
        p_prev = jnp.exp(s_prev - mx)
        p_cur = jnp.exp(s_cur - mx)
        denom = p_prev.sum(-1, keepdims=True) + p_cur.sum(-1, keepdims=True) + jnp.exp(sink - mx)
        o = (jnp.dot(p_prev.astype(BF16), vp_ref[:, ksl], preferred_element_type=F32)
             + jnp.dot(p_cur.astype(BF16), vc_ref[:, ksl], preferred_element_type=F32))
        o = o / denom
        for j in range(A_GROUP):
            hsl = slice((A_GROUP * g + j) * HEAD_DIM, (A_GROUP * g + j + 1) * HEAD_DIM)
            o_ref[:, hsl] = o[j * A_TQ:(j + 1) * A_TQ].astype(o_ref.dtype)


def _attn_a(h, sinks, layer, mask_a):
    tq = A_TQ
    kblk, vblk = AK_OFF // 256, AV_OFF // 256
    prev = lambda i: jnp.maximum(i - 1, 0)
    return pl.pallas_call(
        functools.partial(_attn_a_kernel, layer=layer),
        grid=(SEQ // tq,),
        in_specs=[
            pl.BlockSpec(memory_space=pltpu.SMEM),
            pl.BlockSpec((tq, 1024), lambda i: (i, 0)),
            pl.BlockSpec((tq, 256), lambda i: (prev(i), kblk)),
            pl.BlockSpec((tq, 256), lambda i: (i, kblk)),
            pl.BlockSpec((tq, 256), lambda i: (prev(i), vblk)),
            pl.BlockSpec((tq, 256), lambda i: (i, vblk)),
            pl.BlockSpec((A_GROUP * tq, 2 * tq), lambda i: (0, 0)),
        ],
        out_specs=pl.BlockSpec((tq, 1024), lambda i: (i, 0)),
        out_shape=jax.ShapeDtypeStruct((SEQ, 1024), BF16),
        compiler_params=_params("arbitrary"),
        name="attn_a",
    )(sinks, h, h, h, h, h, mask_a)


def _attn_b_kernel(lam_ref, g_ref, q_ref, k_ref, v_ref, o_ref, vt_s, qst_s, acc_s, s_s, *, lam_init):
    i = pl.program_id(0)
    tq, tk = B_TQ, B_TK
    hcols = lambda hd: slice(hd * HEAD_DIM, (hd + 1) * HEAD_DIM)

    @pl.when(i == 0)
    def _():
        ones = jnp.ones((B_ONES_ROWS, tk), BF16)
        for hd in range(B_HEADS):
            def transpose_block(c, carry, hd=hd):
                start = pl.multiple_of(c * tk, tk)
                blk = v_ref[pl.ds(start, tk), hcols(hd)].astype(F32)
                vt_s[hd, c, :HEAD_DIM, :] = blk.T.astype(BF16)
                vt_s[hd, c, HEAD_DIM:, :] = ones
                return carry
            lax.fori_loop(0, SEQ // tk, transpose_block, 0)

    feat = lax.broadcasted_iota(jnp.int32, (HEAD_DIM, tq), 0)
    for hd in range(B_HEADS):
        qt = q_ref[:, hcols(hd)].astype(F32).T
        qst_s[hd, :, :tq] = jnp.where(feat < B_QK_DIM, qt, 0.0).astype(BF16)
        qst_s[hd, :, tq:] = jnp.where(feat >= B_QK_DIM, qt, 0.0).astype(BF16)
    acc_s[...] = jnp.zeros_like(acc_s)

    def scores(j, hd):
        kj = k_ref[pl.ds(pl.multiple_of(j * tk, tk), tk), hcols(hd)]
        return jnp.dot(kj, qst_s[hd], preferred_element_type=F32)

    def block(j, ms, diag):
        last = diag == n_diag - 1
        out = []
        for hd in range(B_HEADS):
            s = s_s[hd]
            if diag is not None:
                kc = diag * (tk // CHUNK) + lax.broadcasted_iota(jnp.int32, s.shape, 0) // CHUNK
                qc = (lax.broadcasted_iota(jnp.int32, s.shape, 1) % tq) // CHUNK
                s = jnp.where(kc <= qc, s, NEG_INF)
            m_new = jnp.maximum(ms[hd], s.max(axis=0, keepdims=True))
            a = jnp.exp2(ms[hd] - m_new)
            p = jnp.exp2(s - m_new).astype(BF16)
            acc_s[hd] = a * acc_s[hd] + jnp.dot(vt_s[hd, j], p, preferred_element_type=F32)
            if not last:
                s_s[hd] = scores(j + 1, hd)
            out.append(m_new)
        return tuple(out)

    n_diag = tq // tk
    for hd in range(B_HEADS):
        s_s[hd] = scores(0, hd)
    ms = tuple(jnp.full((1, 2 * tq), NEG_INF, F32) for _ in range(B_HEADS))
    ms = lax.fori_loop(0, i * n_diag, lambda j, c: block(j, c, None), ms)
    for d in range(n_diag):
        ms = block(i * n_diag + d, ms, d)

    lam = (jnp.exp(jnp.sum(lam_ref[0:1, :] * lam_ref[1:2, :], axis=-1, keepdims=True))
           - jnp.exp(jnp.sum(lam_ref[2:3, :] * lam_ref[3:4, :], axis=-1, keepdims=True))
           + lam_init)
    for hd in range(B_HEADS):
        acc = acc_s[hd]
        o_all = acc[:HEAD_DIM] * (1.0 / acc[HEAD_DIM:HEAD_DIM + 1])
        o = o_all[:, :tq] - lam * o_all[:, tq:]
        o = o * lax.rsqrt(jnp.mean(jnp.square(o), axis=0, keepdims=True) + LN_EPS)
        o = o * g_ref[...] * (1.0 - lam_init)
        o_ref[:, hcols(hd)] = o.T.astype(o_ref.dtype)


def _attn_b(h, lam_vecs, sub_g, layer, lam_init):
    tq, tk = B_TQ, B_TK
    width = B_HEADS * HEAD_DIM
    resident = lambda col: pl.BlockSpec((SEQ, width), lambda i: (0, col), pipeline_mode=pl.Buffered(1))
    return pl.pallas_call(
        functools.partial(_attn_b_kernel, lam_init=lam_init),
        grid=(SEQ // tq,),
        in_specs=[
            pl.BlockSpec((None, 4, B_QK_DIM), lambda i: (layer, 0, 0)),
            pl.BlockSpec((None, HEAD_DIM, 1), lambda i: (layer, 0, 0)),
            pl.BlockSpec((tq, width), lambda i: (i, BQ_OFF // width)),
            resident(BK_OFF // width),
            resident(BV_OFF // width),
        ],
        out_specs=pl.BlockSpec((tq, width), lambda i: (i, 0)),
        out_shape=jax.ShapeDtypeStruct((SEQ, width), BF16),
        scratch_shapes=[
            pltpu.VMEM((B_HEADS, SEQ // tk, HEAD_DIM + B_ONES_ROWS, tk), BF16),
            pltpu.VMEM((B_HEADS, HEAD_DIM, 2 * tq), BF16),
            pltpu.VMEM((B_HEADS, HEAD_DIM + B_ONES_ROWS, 2 * tq), F32),
            pltpu.VMEM((B_HEADS, tk, 2 * tq), F32),
        ],
        compiler_params=_params("arbitrary"),
        name="attn_b",
    )(lam_vecs, sub_g, h, h, h)


def _attn_c_kernel(q_ref, *refs):
    k_refs = refs[:C_KBLOCKS]
    v_refs = refs[C_KBLOCKS:2 * C_KBLOCKS]
    bias_ref = refs[2 * C_KBLOCKS]
    o_ref = refs[2 * C_KBLOCKS + 1]
    i = pl.program_id(0)
    dn = (((1,), (1,)), ((), ()))
    for hd in range(C_HEADS):
        hsl = slice(hd * HEAD_DIM, (hd + 1) * HEAD_DIM)
        q = q_ref[:, hsl]
        s = []
        for b in range(C_KBLOCKS):
            sb = lax.dot_general(q, k_refs[b][:, hsl], dn, preferred_element_type=F32)
            sb = sb + bias_ref[hd, :, b * C_TQ:(b + 1) * C_TQ]
            s.append(jnp.where(i - (C_KBLOCKS - 1) + b >= 0, sb, NEG_INF))
        mx = functools.reduce(jnp.maximum, [sb.max(-1, keepdims=True) for sb in s])
        p = [jnp.exp(sb - mx) for sb in s]
        denom = functools.reduce(lambda a, b_: a + b_, [pb.sum(-1, keepdims=True) for pb in p])
        o = functools.reduce(
            lambda a, b_: a + b_,
            [jnp.dot(p[b].astype(BF16), v_refs[b][:, hsl], preferred_element_type=F32)
             for b in range(C_KBLOCKS)])
        o_ref[:, hsl] = (o / denom).astype(o_ref.dtype)


def _attn_c(h, bias_c, layer):
    tq = C_TQ
    qb, kb, vb = CQ_OFF // 512, CK_OFF // 512, CV_OFF // 512

    def kv_spec(b, col):
        return pl.BlockSpec((tq, 512), lambda i: (jnp.maximum(i - (C_KBLOCKS - 1) + b, 0), col))

    return pl.pallas_call(
        _attn_c_kernel,
        grid=(SEQ // tq,),
        in_specs=([pl.BlockSpec((tq, 512), lambda i: (i, qb))]
                  + [kv_spec(b, kb) for b in range(C_KBLOCKS)]
                  + [kv_spec(b, vb) for b in range(C_KBLOCKS)]
                  + [pl.BlockSpec((None, C_HEADS, tq, C_KBLOCKS * tq), lambda i: (layer, 0, 0, 0))]),
        out_specs=pl.BlockSpec((tq, 512), lambda i: (i, 0)),
        out_shape=jax.ShapeDtypeStruct((SEQ, C_HEADS * HEAD_DIM), BF16),
        compiler_params=_params("arbitrary"),
        name="attn_c",
    )(h, *([h] * (2 * C_KBLOCKS)), bias_c)


def _mix_kernel(x_ref, ya_ref, yb_ref, yc_ref, wga_ref, wgb_ref, wgc_ref, ba_ref, bb_ref, bc_ref,
                wa_ref, wb_ref, wc_ref, o_ref, wg_s, wa_s, wb_s, wc_s):
    m = pl.program_id(1)

    @pl.when(m == 0)
    def _():
        wg_s[0] = wga_ref[...].astype(BF16)
        wg_s[1] = wgb_ref[...].astype(BF16)
        wg_s[2] = wgc_ref[...].astype(BF16)
        wa_s[...] = wa_ref[...].astype(BF16)
        wb_s[...] = wb_ref[...].astype(BF16)
        wc_s[...] = wc_ref[...].astype(BF16)

    x = x_ref[...]

    def branch(idx, b_ref, y_ref, w_s):
        gate = jax.nn.sigmoid(jnp.dot(x, wg_s[idx], preferred_element_type=F32) + b_ref[...])
        return gate * jnp.dot(y_ref[...], w_s[...], preferred_element_type=F32)

    mix = branch(0, ba_ref, ya_ref, wa_s) + branch(1, bb_ref, yb_ref, wb_s) + branch(2, bc_ref, yc_ref, wc_s)
    o_ref[...] = mix.astype(o_ref.dtype)


def _mix(xb, ya, yb, yc, w_gate, b_gate, w_br_a, w_br_b, w_br_c, layer):
    tm, tn = MIX_TM, MIX_TN
    nb = D_MODEL // tn
    row = lambda width: pl.BlockSpec((tm, width), lambda n, m: (m, 0))
    gate_w = lambda k: pl.BlockSpec((None, D_MODEL, tn), lambda n, m: (layer, 0, k * nb + n))
    gate_b = lambda k: pl.BlockSpec((None, 1, tn), lambda n, m: (layer, 0, k * nb + n))
    br_w = lambda width: pl.BlockSpec((None, width, tn), lambda n, m: (layer, 0, n))
    return pl.pallas_call(
        _mix_kernel,
        grid=(nb, SEQ // tm),
        in_specs=[row(D_MODEL), row(1024), row(512), row(512),
                  gate_w(0), gate_w(1), gate_w(2), gate_b(0), gate_b(1), gate_b(2),
                  br_w(1024), br_w(512), br_w(512)],
        out_specs=pl.BlockSpec((tm, tn), lambda n, m: (m, n)),
        out_shape=jax.ShapeDtypeStruct((SEQ, D_MODEL), BF16),
        scratch_shapes=[pltpu.VMEM((3, D_MODEL, tn), BF16), pltpu.VMEM((1024, tn), BF16),
                        pltpu.VMEM((512, tn), BF16), pltpu.VMEM((512, tn), BF16)],
        compiler_params=_params("arbitrary", "arbitrary"),
        name="mix",
    )(xb, ya, yb, yc, w_gate, w_gate, w_gate, b_gate, b_gate, b_gate, w_br_a, w_br_b, w_br_c)


def _matmul_ln_kernel(y_ref, w_ref, x_ref, g_ref, b_ref, of_ref, ob_ref):
    z = DEEPNORM_ALPHA * x_ref[...] + jnp.dot(y_ref[...], w_ref[...], preferred_element_type=F32)
    mu = jnp.mean(z, axis=-1, keepdims=True)
    zc = z - mu
    var = jnp.mean(jnp.square(zc), axis=-1, keepdims=True)
    out = zc * lax.rsqrt(var + LN_EPS) * g_ref[...] + b_ref[...]
    of_ref[...] = out
    ob_ref[...] = out.astype(ob_ref.dtype)


def _matmul_ln(y, w_bf16, x, g, b, layer, tm, name):
    k = y.shape[1]
    vec = pl.BlockSpec((None, 1, D_MODEL), lambda m: (layer, 0, 0))
    return pl.pallas_call(
        _matmul_ln_kernel,
        grid=(SEQ // tm,),
        in_specs=[
            pl.BlockSpec((tm, k), lambda m: (m, 0)),
            pl.BlockSpec((None, k, D_MODEL), lambda m: (layer, 0, 0), pipeline_mode=pl.Buffered(1)),
            pl.BlockSpec((tm, D_MODEL), lambda m: (m, 0)),
            vec, vec,
        ],
        out_specs=[pl.BlockSpec((tm, D_MODEL), lambda m: (m, 0)),
                   pl.BlockSpec((tm, D_MODEL), lambda m: (m, 0))],
        out_shape=[jax.ShapeDtypeStruct((SEQ, D_MODEL), F32),
                   jax.ShapeDtypeStruct((SEQ, D_MODEL), BF16)],
        compiler_params=_params("arbitrary"),
        name=name,
    )(y, w_bf16, x, g, b)


def _ffn_in_kernel(x_ref, wg_ref, wu_ref, o_ref, wg_s, wu_s):
    m = pl.program_id(1)

    @pl.when(m == 0)
    def _():
        wg_s[...] = wg_ref[...].astype(BF16)
        wu_s[...] = wu_ref[...].astype(BF16)

    x = x_ref[...]
    gate = jnp.dot(x, wg_s[...], preferred_element_type=F32)
    up = jnp.dot(x, wu_s[...], preferred_element_type=F32)
    o_ref[...] = (jax.nn.silu(gate) * up).astype(o_ref.dtype)


def _ffn_in(xb, w_ffn_in, layer):
    tm, tn = FFN_TM, FFN_TN
    nb = FFN_HIDDEN // tn
    return pl.pallas_call(
        _ffn_in_kernel,
        grid=(nb, SEQ // tm),
        in_specs=[
            pl.BlockSpec((tm, D_MODEL), lambda n, m: (m, 0)),
            pl.BlockSpec((None, D_MODEL, tn), lambda n, m: (layer, 0, n)),
            pl.BlockSpec((None, D_MODEL, tn), lambda n, m: (layer, 0, nb + n)),
        ],
        out_specs=pl.BlockSpec((tm, tn), lambda n, m: (m, n)),
        out_shape=jax.ShapeDtypeStruct((SEQ, FFN_HIDDEN), BF16),
        scratch_shapes=[pltpu.VMEM((D_MODEL, tn), BF16), pltpu.VMEM((D_MODEL, tn), BF16)],
        compiler_params=_params("arbitrary", "arbitrary"),
        name="ffn_in",
    )(xb, w_ffn_in, w_ffn_in)


def _rope_tables():
    pos = jnp.arange(SEQ, dtype=F32)

    def cs(dim):
        inv = 1.0 / (ROPE_THETA ** (jnp.arange(0, dim, 2, dtype=F32) / dim))
        ang = pos[:, None] * inv[None, :]
        ang = jnp.concatenate([ang, ang], axis=-1)
        return jnp.cos(ang), jnp.sin(ang)

    cos_a, sin_a = cs(HEAD_DIM)
    half = HEAD_DIM // 2
    sin_a = jnp.concatenate([-sin_a[:, :half], sin_a[:, half:]], axis=-1)
    cos_b, sin_b = cs(B_QK_DIM)
    cos_b2 = jnp.concatenate([cos_b, cos_b], axis=-1)
    sin_b2 = jnp.concatenate([sin_b, sin_b], axis=-1)
    first_half = (np.arange(HEAD_DIM) % B_QK_DIM) < (B_QK_DIM // 2)
    sin_lo = jnp.where(first_half[None, :], -sin_b2, 0.0)
    sin_hi = jnp.where(first_half[None, :], 0.0, sin_b2)
    return cos_a, sin_a, cos_b2, sin_lo, sin_hi


def _mask_a():
    qc = (np.arange(A_TQ) // CHUNK)[:, None]
    kc = (np.arange(2 * A_TQ) // CHUNK)[None, :]
    ok = (kc >= qc) & (kc <= qc + 2)
    m = np.where(ok, 0.0, NEG_INF).astype(np.float32)
    return jnp.asarray(np.tile(m, (A_GROUP, 1)))


def _bias_c(rel_bias):
    nq, nk = C_TQ, C_KBLOCKS * C_TQ
    nt = nq + nk - 1
    lo = REL_CLIP - (nq - 1)
    rb = rel_bias.astype(F32)
    ramp = rb[..., lo:]
    flat_part = jnp.broadcast_to(rb[..., -1:], rb.shape[:-1] + (nt - ramp.shape[-1],))
    w = jnp.concatenate([ramp, flat_part], axis=-1)
    u = jnp.concatenate([w[..., ::-1], jnp.zeros(w.shape[:-1] + (1,), F32)], axis=-1)
    flat = jnp.tile(u, nq)[..., :nq * nt]
    skew = flat.reshape(flat.shape[:-1] + (nq, nt))
    bias = skew[..., nq - 1:]
    q = np.arange(nq)[:, None]
    k = np.arange(nk)[None, :]
    qc, kc = q // CHUNK, k // CHUNK
    ok = (kc >= qc) & (kc <= qc + C_PREV_CHUNKS)
    return jnp.where(ok, bias, NEG_INF)


def kernel(x, w_in, sinks, lambda_q1, lambda_k1, lambda_q2, lambda_k2, diff_norm_g, rel_bias,
           w_br_a, w_br_b, w_br_c, w_gate, b_gate, w_out, ln1_g, ln1_b,
           w_ffn_in, w_ffn_out, ln2_g, ln2_b):
    assert x.shape == (1, SEQ, D_MODEL)
    tabs = _rope_tables()
    mask_a = _mask_a()
    xf = x.reshape(SEQ, D_MODEL)
    xb = xf.astype(BF16)
    lam_vecs = jnp.stack([lambda_q1, lambda_k1, lambda_q2, lambda_k2], axis=1).astype(F32)
    sub_g = diff_norm_g.astype(F32).reshape(DEPTH, HEAD_DIM, 1)
    bias_c = _bias_c(rel_bias)
    b_gate3 = b_gate.reshape(DEPTH, 1, -1)
    vec3 = lambda v: v.reshape(DEPTH, 1, D_MODEL)
    ln1_g, ln1_b, ln2_g, ln2_b = vec3(ln1_g), vec3(ln1_b), vec3(ln2_g), vec3(ln2_b)
    w_out_b = w_out.astype(BF16)
    w_ffn_out_b = w_ffn_out.astype(BF16)
    for l in range(DEPTH):
        lam_init = 0.8 - 0.6 * math.exp(-0.3 * l)
        h = _in_proj(xb, w_in, l, tabs)
        ya = _attn_a(h, sinks, l, mask_a)
        yb = _attn_b(h, lam_vecs, sub_g, l, lam_init)
        yc = _attn_c(h, bias_c, l)
        mix = _mix(xb, ya, yb, yc, w_gate, b_gate3, w_br_a, w_br_b, w_br_c, l)
        xf, xb = _matmul_ln(mix, w_out_b, xf, ln1_g, ln1_b, l, OUT_TM, "out_ln")
        f = _ffn_in(xb, w_ffn_in, l)
        xf, xb = _matmul_ln(f, w_ffn_out_b, xf, ln2_g, ln2_b, l, FFN_OUT_TM, "ffn_out")
    return xf.reshape(1, SEQ, D_MODEL)
```

```python
import functools
import math

import jax
import jax.numpy as jnp
import numpy as np
from jax import lax
from jax.experimental import pallas as pl
from jax.experimental.pallas import tpu as pltpu

D_MODEL = 2048
SEQ = 8192
DEPTH = 4
CHUNK = 64
HEAD_DIM = 128
A_Q_HEADS = 8
A_KV_HEADS = 2
A_GROUP = A_Q_HEADS // A_KV_HEADS
B_HEADS = 4
B_QK_DIM = 64
C_HEADS = 4
C_PREV_CHUNKS = 8
REL_CLIP = 256
FFN_HIDDEN = 5632
IN_WIDTH = 4608
ROPE_THETA = 10000.0
LN_EPS = 1e-5
DEEPNORM_ALPHA = (2 * DEPTH) ** 0.25
NEG_INF = -1e30

BF16 = jnp.bfloat16
F32 = jnp.float32

VMEM_LIMIT_BYTES = 56 * 1024 * 1024

AQ_OFF, AK_OFF, AV_OFF = 0, 1024, 1280
BQ_OFF, BK_OFF, BV_OFF = 1536, 2048, 2560
CQ_OFF, CK_OFF, CV_OFF = 3072, 3584, 4096

LN_ROW_CHUNK = 128
ROW_CHUNK = 256
PROJ_TM, PROJ_TN = 1024, 512
MIX_TM, MIX_TN = 1024, 256
OUT_TM = 512
FFN_TM, FFN_TN = 1024, 512
FFN_OUT_TM = 256
A_TQ = 256
A_WINDOW = 128
B_TQ = 512
B_TK = 256
B_ONES_ROWS = 16
C_TQ = 256
C_KBLOCKS = (C_PREV_CHUNKS * CHUNK) // C_TQ + 1


def _params(*sem):
    return pltpu.CompilerParams(dimension_semantics=sem, vmem_limit_bytes=VMEM_LIMIT_BYTES)


def _rope_a(t, cos, sin_signed):
    return t * cos + pltpu.roll(t, HEAD_DIM // 2, 1) * sin_signed


def _rope_b(t, cos2, sin_lo, sin_hi):
    return t * cos2 + pltpu.roll(t, 96, 1) * sin_lo + pltpu.roll(t, 32, 1) * sin_hi


def _in_proj_kernel(x_ref, w_ref, cosa_ref, sina_ref, cosb_ref, sinb_lo_ref, sinb_hi_ref,
                    o_ref, wb_ref):
    n = pl.program_id(0)
    m = pl.program_id(1)

    @pl.when(m == 0)
    def _():
        wb_ref[...] = w_ref[...].astype(BF16)

    heads = PROJ_TN // HEAD_DIM
    a_scale = HEAD_DIM ** -0.5
    b_scale = B_QK_DIM ** -0.5 * math.log2(math.e)

    def run(epilogue):
        for r in range(PROJ_TM // ROW_CHUNK):
            rows = slice(r * ROW_CHUNK, (r + 1) * ROW_CHUNK)
            acc = jnp.dot(x_ref[rows, :], wb_ref[...], preferred_element_type=F32)
            epilogue(acc, rows)

    def rope_a_cols(acc, rows, lo, hi, scale):
        cos, sin = cosa_ref[rows, :], sina_ref[rows, :]
        for j in range(lo, hi):
            sl = slice(j * HEAD_DIM, (j + 1) * HEAD_DIM)
            r = _rope_a(acc[:, sl], cos, sin)
            if scale != 1.0:
                r = r * scale
            o_ref[rows, sl] = r.astype(o_ref.dtype)

    def rope_b_cols(acc, rows, scale):
        cos, lo_, hi_ = cosb_ref[rows, :], sinb_lo_ref[rows, :], sinb_hi_ref[rows, :]
        for j in range(heads):
            sl = slice(j * HEAD_DIM, (j + 1) * HEAD_DIM)
            r = _rope_b(acc[:, sl], cos, lo_, hi_)
            if scale != 1.0:
                r = r * scale
            o_ref[rows, sl] = r.astype(o_ref.dtype)

    def ak_av(acc, rows):
        rope_a_cols(acc, rows, 0, 2, 1.0)
        o_ref[rows, 2 * HEAD_DIM:] = acc[:, 2 * HEAD_DIM:].astype(o_ref.dtype)

    def scaled(acc, rows):
        o_ref[rows, :] = (acc * a_scale).astype(o_ref.dtype)

    def plain(acc, rows):
        o_ref[rows, :] = acc.astype(o_ref.dtype)

    pl.when(n < 2)(lambda: run(lambda acc, rows: rope_a_cols(acc, rows, 0, heads, a_scale)))
    pl.when(n == 2)(lambda: run(ak_av))
    pl.when(n == 3)(lambda: run(lambda acc, rows: rope_b_cols(acc, rows, b_scale)))
    pl.when(n == 4)(lambda: run(lambda acc, rows: rope_b_cols(acc, rows, 1.0)))
    pl.when(n == 6)(lambda: run(scaled))
    pl.when((n == 5) | (n == 7) | (n == 8))(lambda: run(plain))


def _in_proj(xb, w_in, layer, tabs):
    tm, tn = PROJ_TM, PROJ_TN
    rope_spec = pl.BlockSpec((tm, HEAD_DIM), lambda n, m: (m, 0))
    return pl.pallas_call(
        _in_proj_kernel,
        grid=(IN_WIDTH // tn, SEQ // tm),
        in_specs=[
            pl.BlockSpec((tm, D_MODEL), lambda n, m: (m, 0)),
            pl.BlockSpec((None, D_MODEL, tn), lambda n, m: (layer, 0, n)),
            rope_spec, rope_spec, rope_spec, rope_spec, rope_spec,
        ],
        out_specs=pl.BlockSpec((tm, tn), lambda n, m: (m, n)),
        out_shape=jax.ShapeDtypeStruct((SEQ, IN_WIDTH), BF16),
        scratch_shapes=[pltpu.VMEM((D_MODEL, tn), BF16)],
        compiler_params=_params("arbitrary", "arbitrary"),
        name="in_proj",
    )(xb, w_in, *tabs)


def _attn_a_kernel(sink_ref, q_ref, kp_ref, kc_ref, vp_ref, vc_ref, mask_ref, o_ref, *, layer):
    i = pl.program_id(0)
    has_prev = i > 0
    dn = (((1,), (1,)), ((), ()))
    hcols = lambda hd: slice(hd * HEAD_DIM, (hd + 1) * HEAD_DIM)
    scores = []
    for g in range(A_KV_HEADS):
        q = jnp.concatenate([q_ref[:, hcols(A_GROUP * g + j)] for j in range(A_GROUP)], axis=0)
        scores.append((lax.dot_general(q, kp_ref[:, hcols(g)], dn, preferred_element_type=F32),
                       lax.dot_general(q, kc_ref[:, hcols(g)], dn, preferred_element_type=F32)))
    for g in range(A_KV_HEADS):
        s_prev, s_cur = scores[g]
        s_prev = jnp.where(has_prev, s_prev + mask_ref[:, :A_WINDOW], NEG_INF)
        s_cur = s_cur + mask_ref[:, A_WINDOW:]
        sink = jnp.concatenate(
            [jnp.full((A_TQ, 1), sink_ref[layer, A_GROUP * g + j], F32) for j in range(A_GROUP)], axis=0)
        mx = jnp.maximum(jnp.maximum(s_prev.max(-1, keepdims=True), s_cur.max(-1, keepdims=True)), sink)
        p_prev = jnp.exp(s_prev - mx)
        p_cur = jnp.exp(s_cur - mx)
        denom = p_prev.sum(-1, keepdims=True) + p_cur.sum(-1, keepdims=True) + jnp.exp(sink - mx)
        o = (jnp.dot(p_prev.astype(BF16), vp_ref[:, hcols(g)], preferred_element_type=F32)
             + jnp.dot(p_cur.astype(BF16), vc_ref[:, hcols(g)], preferred_element_type=F32))
        o = o * (1.0 / denom)
        for j in range(A_GROUP):
            o_ref[:, hcols(A_GROUP * g + j)] = o[j * A_TQ:(j + 1) * A_TQ].astype(o_ref.dtype)


def _attn_a(h, sinks, layer, mask_a):
    tq, win = A_TQ, A_WINDOW
    kv_width = A_KV_HEADS * HEAD_DIM
    kblk, vblk = AK_OFF // kv_width, AV_OFF // kv_width
    prev = lambda i: jnp.maximum(i * (tq // win) - 1, 0)
    return pl.pallas_call(
        functools.partial(_attn_a_kernel, layer=layer),
        grid=(SEQ // tq,),
        in_specs=[
            pl.BlockSpec(memory_space=pltpu.SMEM),
            pl.BlockSpec((tq, A_Q_HEADS * HEAD_DIM), lambda i: (i, 0)),
            pl.BlockSpec((win, kv_width), lambda i: (prev(i), kblk)),
            pl.BlockSpec((tq, kv_width), lambda i: (i, kblk)),
            pl.BlockSpec((win, kv_width), lambda i: (prev(i), vblk)),
            pl.BlockSpec((tq, kv_width), lambda i: (i, vblk)),
            pl.BlockSpec((A_GROUP * tq, win + tq), lambda i: (0, 0)),
        ],
        out_specs=pl.BlockSpec((tq, A_Q_HEADS * HEAD_DIM), lambda i: (i, 0)),
        out_shape=jax.ShapeDtypeStruct((SEQ, A_Q_HEADS * HEAD_DIM), BF16),
        compiler_params=_params("arbitrary"),
        name="attn_a",
    )(sinks, h, h, h, h, h, mask_a)


def _attn_b_kernel(lam_ref, g_ref, q_ref, k_ref, v_ref, o_ref, vt_s, qst_s, acc_s, s_s, *, lam_init):
    i = pl.program_id(0)
    tq, tk = B_TQ, B_TK
    hcols = lambda hd: slice(hd * HEAD_DIM, (hd + 1) * HEAD_DIM)

    @pl.when(i == 0)
    def _():
        ones = jnp.ones((B_ONES_ROWS, tk), BF16)
        for hd in range(B_HEADS):
            def transpose_block(c, carry, hd=hd):
                start = pl.multiple_of(c * tk, tk)
                blk = v_ref[pl.ds(start, tk), hcols(hd)].astype(F32)
                vt_s[hd, c, :HEAD_DIM, :] = blk.T.astype(BF16)
                vt_s[hd, c, HEAD_DIM:, :] = ones
                return carry
            lax.fori_loop(0, SEQ // tk, transpose_block, 0)

    feat = lax.broadcasted_iota(jnp.int32, (HEAD_DIM, tq), 0)
    for hd in range(B_HEADS):
        qt = q_ref[:, hcols(hd)].astype(F32).T
        qst_s[hd, :, :tq] = jnp.where(feat < B_QK_DIM, qt, 0.0).astype(BF16)
        qst_s[hd, :, tq:] = jnp.where(feat >= B_QK_DIM, qt, 0.0).astype(BF16)
    acc_s[...] = jnp.zeros_like(acc_s)

    def scores(j, hd):
        kj = k_ref[pl.ds(pl.multiple_of(j * tk, tk), tk), hcols(hd)]
        return jnp.dot(kj, qst_s[hd], preferred_element_type=F32)

    def block(j, ms, diag):
        last = diag == n_diag - 1
        out = []
        for hd in range(B_HEADS):
            s = s_s[hd]
            if diag is not None:
                kc = diag * (tk // CHUNK) + lax.broadcasted_iota(jnp.int32, s.shape, 0) // CHUNK
                qc = (lax.broadcasted_iota(jnp.int32, s.shape, 1) % tq) // CHUNK
                s = jnp.where(kc <= qc, s, NEG_INF)
            m_new = jnp.maximum(ms[hd], s.max(axis=0, keepdims=True))
            a = jnp.exp2(ms[hd] - m_new)
            p = jnp.exp2(s - m_new).astype(BF16)
            acc_s[hd] = a * acc_s[hd] + jnp.dot(vt_s[hd, j], p, preferred_element_type=F32)
            if not last:
                s_s[hd] = scores(j + 1, hd)
            out.append(m_new)
        return tuple(out)

    n_diag = tq // tk
    for hd in range(B_HEADS):
        s_s[hd] = scores(0, hd)
    ms = tuple(jnp.full((1, 2 * tq), NEG_INF, F32) for _ in range(B_HEADS))
    ms = lax.fori_loop(0, i * n_diag, lambda j, c: block(j, c, None), ms)
    for d in range(n_diag):
        ms = block(i * n_diag + d, ms, d)

    lam = (jnp.exp(jnp.sum(lam_ref[0:1, :] * lam_ref[1:2, :], axis=-1, keepdims=True))
           - jnp.exp(jnp.sum(lam_ref[2:3, :] * lam_ref[3:4, :], axis=-1, keepdims=True))
           + lam_init)
    for hd in range(B_HEADS):
        acc = acc_s[hd]
        o_all = acc[:HEAD_DIM] * (1.0 / acc[HEAD_DIM:HEAD_DIM + 1])
        o = o_all[:, :tq] - lam * o_all[:, tq:]
        o = o * lax.rsqrt(jnp.mean(jnp.square(o), axis=0, keepdims=True) + LN_EPS)
        o = o * g_ref[...] * (1.0 - lam_init)
        o_ref[:, hcols(hd)] = o.T.astype(o_ref.dtype)


def _attn_b(h, lam_vecs, sub_g, layer, lam_init):
    tq, tk = B_TQ, B_TK
    width = B_HEADS * HEAD_DIM
    resident = lambda col: pl.BlockSpec((SEQ, width), lambda i: (0, col), pipeline_mode=pl.Buffered(1))
    return pl.pallas_call(
        functools.partial(_attn_b_kernel, lam_init=lam_init),
        grid=(SEQ // tq,),
        in_specs=[
            pl.BlockSpec((None, 4, B_QK_DIM), lambda i: (layer, 0, 0)),
            pl.BlockSpec((None, HEAD_DIM, 1), lambda i: (layer, 0, 0)),
            pl.BlockSpec((tq, width), lambda i: (i, BQ_OFF // width)),
            resident(BK_OFF // width),
            resident(BV_OFF // width),
        ],
        out_specs=pl.BlockSpec((tq, width), lambda i: (i, 0)),
        out_shape=jax.ShapeDtypeStruct((SEQ, width), BF16),
        scratch_shapes=[
            pltpu.VMEM((B_HEADS, SEQ // tk, HEAD_DIM + B_ONES_ROWS, tk), BF16),
            pltpu.VMEM((B_HEADS, HEAD_DIM, 2 * tq), BF16),
            pltpu.VMEM((B_HEADS, HEAD_DIM + B_ONES_ROWS, 2 * tq), F32),
            pltpu.VMEM((B_HEADS, tk, 2 * tq), F32),
        ],
        compiler_params=_params("arbitrary"),
        name="attn_b",
    )(lam_vecs, sub_g, h, h, h)


def _attn_c_kernel(q_ref, *refs):
    k_refs = refs[:C_KBLOCKS]
    v_refs = refs[C_KBLOCKS:2 * C_KBLOCKS]
    bias_ref = refs[2 * C_KBLOCKS]
    o_ref = refs[2 * C_KBLOCKS + 1]
    i = pl.program_id(0)
    dn = (((1,), (1,)), ((), ()))
    hcols = lambda hd: slice(hd * HEAD_DIM, (hd + 1) * HEAD_DIM)
    scores = [[lax.dot_general(q_ref[:, hcols(hd)], k_refs[b][:, hcols(hd)], dn, preferred_element_type=F32)
               for b in range(C_KBLOCKS)] for hd in range(C_HEADS)]
    for hd in range(C_HEADS):
        hsl = hcols(hd)
        s = []
        for b in range(C_KBLOCKS):
            sb = scores[hd][b] + bias_ref[hd, :, b * C_TQ:(b + 1) * C_TQ]
            s.append(jnp.where(i - (C_KBLOCKS - 1) + b >= 0, sb, NEG_INF))
        mx = functools.reduce(jnp.maximum, [sb.max(-1, keepdims=True) for sb in s])
        p = [jnp.exp(sb - mx) for sb in s]
        denom = functools.reduce(lambda a, b_: a + b_, [pb.sum(-1, keepdims=True) for pb in p])
        o = functools.reduce(
            lambda a, b_: a + b_,
            [jnp.dot(p[b].astype(BF16), v_refs[b][:, hsl], preferred_element_type=F32)
             for b in range(C_KBLOCKS)])
        o_ref[:, hsl] = (o * (1.0 / denom)).astype(o_ref.dtype)


def _attn_c(h, bias_c, layer):
    tq = C_TQ
    qb, kb, vb = CQ_OFF // 512, CK_OFF // 512, CV_OFF // 512

    def kv_spec(b, col):
        return pl.BlockSpec((tq, 512), lambda i: (jnp.maximum(i - (C_KBLOCKS - 1) + b, 0), col))

    return pl.pallas_call(
        _attn_c_kernel,
        grid=(SEQ // tq,),
        in_specs=([pl.BlockSpec((tq, 512), lambda i: (i, qb))]
                  + [kv_spec(b, kb) for b in range(C_KBLOCKS)]
                  + [kv_spec(b, vb) for b in range(C_KBLOCKS)]
                  + [pl.BlockSpec((None, C_HEADS, tq, C_KBLOCKS * tq), lambda i: (layer, 0, 0, 0))]),
        out_specs=pl.BlockSpec((tq, 512), lambda i: (i, 0)),
        out_shape=jax.ShapeDtypeStruct((SEQ, C_HEADS * HEAD_DIM), BF16),
        compiler_params=_params("arbitrary"),
        name="attn_c",
    )(h, *([h] * (2 * C_KBLOCKS)), bias_c)


def _mix_kernel(x_ref, ya_ref, yb_ref, yc_ref, wga_ref, wgb_ref, wgc_ref, ba_ref, bb_ref, bc_ref,
                wa_ref, wb_ref, wc_ref, o_ref, wg_s, wa_s, wb_s, wc_s):
    m = pl.program_id(1)

    @pl.when(m == 0)
    def _():
        wg_s[0] = wga_ref[...].astype(BF16)
        wg_s[1] = wgb_ref[...].astype(BF16)
        wg_s[2] = wgc_ref[...].astype(BF16)
        wa_s[...] = wa_ref[...].astype(BF16)
        wb_s[...] = wb_ref[...].astype(BF16)
        wc_s[...] = wc_ref[...].astype(BF16)

    for r in range(MIX_TM // ROW_CHUNK):
        rows = slice(r * ROW_CHUNK, (r + 1) * ROW_CHUNK)
        x = x_ref[rows, :]

        def branch(idx, b_ref, y_ref, w_s):
            gate = jax.nn.sigmoid(jnp.dot(x, wg_s[idx], preferred_element_type=F32) + b_ref[...])
            return gate * jnp.dot(y_ref[rows, :], w_s[...], preferred_element_type=F32)

        mix = branch(0, ba_ref, ya_ref, wa_s) + branch(1, bb_ref, yb_ref, wb_s) + branch(2, bc_ref, yc_ref, wc_s)
        o_ref[rows, :] = mix.astype(o_ref.dtype)


def _mix(xb, ya, yb, yc, w_gate, b_gate, w_br_a, w_br_b, w_br_c, layer):
    tm, tn = MIX_TM, MIX_TN
    nb = D_MODEL // tn
    row = lambda width: pl.BlockSpec((tm, width), lambda n, m: (m, 0))
    gate_w = lambda k: pl.BlockSpec((None, D_MODEL, tn), lambda n, m: (layer, 0, k * nb + n))
    gate_b = lambda k: pl.BlockSpec((None, 1, tn), lambda n, m: (layer, 0, k * nb + n))
    br_w = lambda width: pl.BlockSpec((None, width, tn), lambda n, m: (layer, 0, n))
    return pl.pallas_call(
        _mix_kernel,
        grid=(nb, SEQ // tm),
        in_specs=[row(D_MODEL), row(1024), row(512), row(512),
                  gate_w(0), gate_w(1), gate_w(2), gate_b(0), gate_b(1), gate_b(2),
                  br_w(1024), br_w(512), br_w(512)],
        out_specs=pl.BlockSpec((tm, tn), lambda n, m: (m, n)),
        out_shape=jax.ShapeDtypeStruct((SEQ, D_MODEL), BF16),
        scratch_shapes=[pltpu.VMEM((3, D_MODEL, tn), BF16), pltpu.VMEM((1024, tn), BF16),
                        pltpu.VMEM((512, tn), BF16), pltpu.VMEM((512, tn), BF16)],
        compiler_params=_params("arbitrary", "arbitrary"),
        name="mix",
    )(xb, ya, yb, yc, w_gate, w_gate, w_gate, b_gate, b_gate, b_gate, w_br_a, w_br_b, w_br_c)


def _matmul_ln_kernel(y_ref, w_ref, x_ref, g_ref, b_ref, of_ref, ob_ref, *, chunk):
    for r in range(y_ref.shape[0] // chunk):
        rows = slice(r * chunk, (r + 1) * chunk)
        z = DEEPNORM_ALPHA * x_ref[rows, :] + jnp.dot(y_ref[rows, :], w_ref[...], preferred_element_type=F32)
        mu = jnp.mean(z, axis=-1, keepdims=True)
        zc = z - mu
        var = jnp.mean(jnp.square(zc), axis=-1, keepdims=True)
        out = zc * lax.rsqrt(var + LN_EPS) * g_ref[...] + b_ref[...]
        of_ref[rows, :] = out
        ob_ref[rows, :] = out.astype(ob_ref.dtype)


def _matmul_ln(y, w_bf16, x, g, b, layer, tm, name):
    k = y.shape[1]
    vec = pl.BlockSpec((None, 1, D_MODEL), lambda m: (layer, 0, 0))
    return pl.pallas_call(
        functools.partial(_matmul_ln_kernel, chunk=LN_ROW_CHUNK),
        grid=(SEQ // tm,),
        in_specs=[
            pl.BlockSpec((tm, k), lambda m: (m, 0)),
            pl.BlockSpec((None, k, D_MODEL), lambda m: (layer, 0, 0), pipeline_mode=pl.Buffered(1)),
            pl.BlockSpec((tm, D_MODEL), lambda m: (m, 0)),
            vec, vec,
        ],
        out_specs=[pl.BlockSpec((tm, D_MODEL), lambda m: (m, 0)),
                   pl.BlockSpec((tm, D_MODEL), lambda m: (m, 0))],
        out_shape=[jax.ShapeDtypeStruct((SEQ, D_MODEL), F32),
                   jax.ShapeDtypeStruct((SEQ, D_MODEL), BF16)],
        compiler_params=_params("arbitrary"),
        name=name,
    )(y, w_bf16, x, g, b)


def _ffn_in_kernel(x_ref, wg_ref, wu_ref, o_ref, wg_s, wu_s):
    m = pl.program_id(1)

    @pl.when(m == 0)
    def _():
        wg_s[...] = wg_ref[...].astype(BF16)
        wu_s[...] = wu_ref[...].astype(BF16)

    for r in range(FFN_TM // ROW_CHUNK):
        rows = slice(r * ROW_CHUNK, (r + 1) * ROW_CHUNK)
        x = x_ref[rows, :]
        gate = jnp.dot(x, wg_s[...], preferred_element_type=F32)
        up = jnp.dot(x, wu_s[...], preferred_element_type=F32)
        o_ref[rows, :] = (jax.nn.silu(gate) * up).astype(o_ref.dtype)


def _ffn_in(xb, w_ffn_in, layer):
    tm, tn = FFN_TM, FFN_TN
    nb = FFN_HIDDEN // tn
    return pl.pallas_call(
        _ffn_in_kernel,
        grid=(nb, SEQ // tm),
        in_specs=[
            pl.BlockSpec((tm, D_MODEL), lambda n, m: (m, 0)),
            pl.BlockSpec((None, D_MODEL, tn), lambda n, m: (layer, 0, n)),
            pl.BlockSpec((None, D_MODEL, tn), lambda n, m: (layer, 0, nb + n)),
        ],
        out_specs=pl.BlockSpec((tm, tn), lambda n, m: (m, n)),
        out_shape=jax.ShapeDtypeStruct((SEQ, FFN_HIDDEN), BF16),
        scratch_shapes=[pltpu.VMEM((D_MODEL, tn), BF16), pltpu.VMEM((D_MODEL, tn), BF16)],
        compiler_params=_params("arbitrary", "arbitrary"),
        name="ffn_in",
    )(xb, w_ffn_in, w_ffn_in)


def _rope_tables():
    pos = jnp.arange(SEQ, dtype=F32)

    def cs(dim):
        inv = 1.0 / (ROPE_THETA ** (jnp.arange(0, dim, 2, dtype=F32) / dim))
        ang = pos[:, None] * inv[None, :]
        ang = jnp.concatenate([ang, ang], axis=-1)
        return jnp.cos(ang), jnp.sin(ang)

    cos_a, sin_a = cs(HEAD_DIM)
    half = HEAD_DIM // 2
    sin_a = jnp.concatenate([-sin_a[:, :half], sin_a[:, half:]], axis=-1)
    cos_b, sin_b = cs(B_QK_DIM)
    cos_b2 = jnp.concatenate([cos_b, cos_b], axis=-1)
    sin_b2 = jnp.concatenate([sin_b, sin_b], axis=-1)
    first_half = (np.arange(HEAD_DIM) % B_QK_DIM) < (B_QK_DIM // 2)
    sin_lo = jnp.where(first_half[None, :], -sin_b2, 0.0)
    sin_hi = jnp.where(first_half[None, :], 0.0, sin_b2)
    return cos_a, sin_a, cos_b2, sin_lo, sin_hi


def _mask_a():
    qc = (np.arange(A_TQ) // CHUNK)[:, None]
    kc = (np.arange(A_WINDOW + A_TQ) // CHUNK)[None, :]
    ok = (kc >= qc) & (kc <= qc + A_WINDOW // CHUNK)
    m = np.where(ok, 0.0, NEG_INF).astype(np.float32)
    return jnp.asarray(np.tile(m, (A_GROUP, 1)))


def _bias_c(rel_bias):
    nq, nk = C_TQ, C_KBLOCKS * C_TQ
    nt = nq + nk - 1
    lo = REL_CLIP - (nq - 1)
    rb = rel_bias.astype(F32)
    ramp = rb[..., lo:]
    flat_part = jnp.broadcast_to(rb[..., -1:], rb.shape[:-1] + (nt - ramp.shape[-1],))
    w = jnp.concatenate([ramp, flat_part], axis=-1)
    u = jnp.concatenate([w[..., ::-1], jnp.zeros(w.shape[:-1] + (1,), F32)], axis=-1)
    flat = jnp.tile(u, nq)[..., :nq * nt]
    skew = flat.reshape(flat.shape[:-1] + (nq, nt))
    bias = skew[..., nq - 1:]
    q = np.arange(nq)[:, None]
    k = np.arange(nk)[None, :]
    qc, kc = q // CHUNK, k // CHUNK
    ok = (kc >= qc) & (kc <= qc + C_PREV_CHUNKS)
    return jnp.where(ok, bias, NEG_INF)


def kernel(x, w_in, sinks, lambda_q1, lambda_k1, lambda_q2, lambda_k2, diff_norm_g, rel_bias,
           w_br_a, w_br_b, w_br_c, w_gate, b_gate, w_out, ln1_g, ln1_b,
           w_ffn_in, w_ffn_out, ln2_g, ln2_b):
    assert x.shape == (1, SEQ, D_MODEL)
    tabs = _rope_tables()
    mask_a = _mask_a()
    xf = x.reshape(SEQ, D_MODEL)
    xb = xf.astype(BF16)
    lam_vecs = jnp.stack([lambda_q1, lambda_k1, lambda_q2, lambda_k2], axis=1).astype(F32)
    sub_g = diff_norm_g.astype(F32).reshape(DEPTH, HEAD_DIM, 1)
    bias_c = _bias_c(rel_bias)
    b_gate3 = b_gate.reshape(DEPTH, 1, -1)
    vec3 = lambda v: v.reshape(DEPTH, 1, D_MODEL)
    ln1_g, ln1_b, ln2_g, ln2_b = vec3(ln1_g), vec3(ln1_b), vec3(ln2_g), vec3(ln2_b)
    w_out_b = w_out.astype(BF16)
    w_ffn_out_b = w_ffn_out.astype(BF16)
    for l in range(DEPTH):
        lam_init = 0.8 - 0.6 * math.exp(-0.3 * l)
        h = _in_proj(xb, w_in, l, tabs)
        ya = _attn_a(h, sinks, l, mask_a)
        yb = _attn_b(h, lam_vecs, sub_g, l, lam_init)
        yc = _attn_c(h, bias_c, l)
        mix = _mix(xb, ya, yb, yc, w_gate, b_gate3, w_br_a, w_br_b, w_br_c, l)
        xf, xb = _matmul_ln(mix, w_out_b, xf, ln1_g, ln1_b, l, OUT_TM, "out_ln")
        f = _ffn_in(xb, w_ffn_in, l)
        xf, xb = _matmul_ln(f, w_ffn_out_b, xf, ln2_g, ln2_b, l, FFN_OUT_TM, "ffn_out")
    return xf.reshape(1, SEQ, D_MODEL)
```

```python
import functools
import math

import jax
import jax.numpy as jnp
import numpy as np
from jax import lax
from jax.experimental import pallas as pl
from jax.experimental.pallas import tpu as pltpu

D_MODEL = 2048
SEQ = 8192
DEPTH = 4
CHUNK = 64
HEAD_DIM = 128
A_Q_HEADS = 8
A_KV_HEADS = 2
A_GROUP = A_Q_HEADS // A_KV_HEADS
B_HEADS = 4
B_QK_DIM = 64
C_HEADS = 4
C_PREV_CHUNKS = 8
REL_CLIP = 256
FFN_HIDDEN = 5632
IN_WIDTH = 4608
ROPE_THETA = 10000.0
LN_EPS = 1e-5
DEEPNORM_ALPHA = (2 * DEPTH) ** 0.25
NEG_INF = -1e30

BF16 = jnp.bfloat16
F32 = jnp.float32

VMEM_LIMIT_BYTES = 56 * 1024 * 1024

AQ_OFF, AK_OFF, AV_OFF = 0, 1024, 1280
BQ_OFF, BK_OFF, BV_OFF = 1536, 2048, 2560
CQ_OFF, CK_OFF, CV_OFF = 3072, 3584, 4096

LN_ROW_CHUNK = 128
ROW_CHUNK = 256
PROJ_TM, PROJ_TN = 1024, 1536
MIX_TM, MIX_TN = 1024, 256
OUT_TM = 512
FFN_TM, FFN_TN = 1024, 512
FFN_OUT_TM = 256
A_TQ = 256
A_WINDOW = 128
B_TQ = 512
B_TK = 512
B_ONES_ROWS = 16
C_TQ = 256
C_KBLOCKS = (C_PREV_CHUNKS * CHUNK) // C_TQ + 1


def _params(*sem):
    return pltpu.CompilerParams(dimension_semantics=sem, vmem_limit_bytes=VMEM_LIMIT_BYTES)


def _rope_a(t, cos, sin_signed):
    return t * cos + pltpu.roll(t, HEAD_DIM // 2, 1) * sin_signed


def _rope_b(t, cos2, sin_lo, sin_hi):
    return t * cos2 + pltpu.roll(t, 96, 1) * sin_lo + pltpu.roll(t, 32, 1) * sin_hi


def _in_proj_kernel(x_ref, w_ref, cosa_ref, sina_ref, cosb_ref, sinb_lo_ref, sinb_hi_ref,
                    o_ref, wb_ref):
    n = pl.program_id(0)
    m = pl.program_id(1)

    @pl.when(m == 0)
    def _():
        wb_ref[...] = w_ref[...].astype(BF16)

    a_scale = HEAD_DIM ** -0.5
    b_scale = B_QK_DIM ** -0.5 * math.log2(math.e)

    def run(epilogue):
        for r in range(PROJ_TM // ROW_CHUNK):
            rows = slice(r * ROW_CHUNK, (r + 1) * ROW_CHUNK)
            acc = jnp.dot(x_ref[rows, :], wb_ref[...], preferred_element_type=F32)
            epilogue(acc, rows)

    def head_cols(acc, rows, lo, hi, fn, scale):
        for j in range(lo, hi):
            sl = slice(j * HEAD_DIM, (j + 1) * HEAD_DIM)
            r = fn(acc[:, sl])
            if scale != 1.0:
                r = r * scale
            o_ref[rows, sl] = r.astype(o_ref.dtype)

    def plain_cols(acc, rows, lo, scale=1.0):
        t = acc[:, lo:]
        if scale != 1.0:
            t = t * scale
        o_ref[rows, lo:] = t.astype(o_ref.dtype)

    def mixer_a(acc, rows):
        cos, sin = cosa_ref[rows, :], sina_ref[rows, :]
        rope = lambda t: _rope_a(t, cos, sin)
        head_cols(acc, rows, 0, A_Q_HEADS, rope, a_scale)
        head_cols(acc, rows, A_Q_HEADS, A_Q_HEADS + A_KV_HEADS, rope, 1.0)
        plain_cols(acc, rows, (A_Q_HEADS + A_KV_HEADS) * HEAD_DIM)

    def mixer_b(acc, rows):
        cos, lo_, hi_ = cosb_ref[rows, :], sinb_lo_ref[rows, :], sinb_hi_ref[rows, :]
        rope = lambda t: _rope_b(t, cos, lo_, hi_)
        head_cols(acc, rows, 0, B_HEADS, rope, b_scale)
        head_cols(acc, rows, B_HEADS, 2 * B_HEADS, rope, 1.0)
        plain_cols(acc, rows, 2 * B_HEADS * HEAD_DIM)

    def mixer_c(acc, rows):
        o_ref[rows, :C_HEADS * HEAD_DIM] = (acc[:, :C_HEADS * HEAD_DIM] * a_scale).astype(o_ref.dtype)
        plain_cols(acc, rows, C_HEADS * HEAD_DIM)

    pl.when(n == 0)(lambda: run(mixer_a))
    pl.when(n == 1)(lambda: run(mixer_b))
    pl.when(n == 2)(lambda: run(mixer_c))


def _in_proj(xb, w_in, layer, tabs):
    tm, tn = PROJ_TM, PROJ_TN
    rope_spec = pl.BlockSpec((tm, HEAD_DIM), lambda n, m: (m, 0))
    return pl.pallas_call(
        _in_proj_kernel,
        grid=(IN_WIDTH // tn, SEQ // tm),
        in_specs=[
            pl.BlockSpec((tm, D_MODEL), lambda n, m: (m, 0)),
            pl.BlockSpec((None, D_MODEL, tn), lambda n, m: (layer, 0, n)),
            rope_spec, rope_spec, rope_spec, rope_spec, rope_spec,
        ],
        out_specs=pl.BlockSpec((tm, tn), lambda n, m: (m, n)),
        out_shape=jax.ShapeDtypeStruct((SEQ, IN_WIDTH), BF16),
        scratch_shapes=[pltpu.VMEM((D_MODEL, tn), BF16)],
        compiler_params=_params("arbitrary", "arbitrary"),
        name="in_proj",
    )(xb, w_in, *tabs)


def _attn_a_kernel(sink_ref, q_ref, kp_ref, kc_ref, vp_ref, vc_ref, mask_ref, o_ref, *, layer):
    i = pl.program_id(0)
    has_prev = i > 0
    dn = (((1,), (1,)), ((), ()))
    hcols = lambda hd: slice(hd * HEAD_DIM, (hd + 1) * HEAD_DIM)
    scores = []
    for g in range(A_KV_HEADS):
        q = jnp.concatenate([q_ref[:, hcols(A_GROUP * g + j)] for j in range(A_GROUP)], axis=0)
        scores.append((lax.dot_general(q, kp_ref[:, hcols(g)], dn, preferred_element_type=F32),
                       lax.dot_general(q, kc_ref[:, hcols(g)], dn, preferred_element_type=F32)))
    for g in range(A_KV_HEADS):
        s_prev, s_cur = scores[g]
        s_prev = jnp.where(has_prev, s_prev + mask_ref[:, :A_WINDOW], NEG_INF)
        s_cur = s_cur + mask_ref[:, A_WINDOW:]
        sink = jnp.concatenate(
            [jnp.full((A_TQ, 1), sink_ref[layer, A_GROUP * g + j], F32) for j in range(A_GROUP)], axis=0)
        mx = jnp.maximum(jnp.maximum(s_prev.max(-1, keepdims=True), s_cur.max(-1, keepdims=True)), sink)
        p_prev = jnp.exp(s_prev - mx)
        p_cur = jnp.exp(s_cur - mx)
        denom = p_prev.sum(-1, keepdims=True) + p_cur.sum(-1, keepdims=True) + jnp.exp(sink - mx)
        o = (jnp.dot(p_prev.astype(BF16), vp_ref[:, hcols(g)], preferred_element_type=F32)
             + jnp.dot(p_cur.astype(BF16), vc_ref[:, hcols(g)], preferred_element_type=F32))
        o = o * (1.0 / denom)
        for j in range(A_GROUP):
            o_ref[:, hcols(A_GROUP * g + j)] = o[j * A_TQ:(j + 1) * A_TQ].astype(o_ref.dtype)


def _attn_a(h, sinks, layer, mask_a):
    tq, win = A_TQ, A_WINDOW
    kv_width = A_KV_HEADS * HEAD_DIM
    kblk, vblk = AK_OFF // kv_width, AV_OFF // kv_width
    prev = lambda i: jnp.maximum(i * (tq // win) - 1, 0)
    return pl.pallas_call(
        functools.partial(_attn_a_kernel, layer=layer),
        grid=(SEQ // tq,),
        in_specs=[
            pl.BlockSpec(memory_space=pltpu.SMEM),
            pl.BlockSpec((tq, A_Q_HEADS * HEAD_DIM), lambda i: (i, 0)),
            pl.BlockSpec((win, kv_width), lambda i: (prev(i), kblk)),
            pl.BlockSpec((tq, kv_width), lambda i: (i, kblk)),
            pl.BlockSpec((win, kv_width), lambda i: (prev(i), vblk)),
            pl.BlockSpec((tq, kv_width), lambda i: (i, vblk)),
            pl.BlockSpec((A_GROUP * tq, win + tq), lambda i: (0, 0)),
        ],
        out_specs=pl.BlockSpec((tq, A_Q_HEADS * HEAD_DIM), lambda i: (i, 0)),
        out_shape=jax.ShapeDtypeStruct((SEQ, A_Q_HEADS * HEAD_DIM), BF16),
        compiler_params=_params("arbitrary"),
        name="attn_a",
    )(sinks, h, h, h, h, h, mask_a)


def _attn_b_kernel(lam_ref, g_ref, q_ref, k_ref, v_ref, o_ref, vt_s, qst_s, acc_s, s_s, *, lam_init):
    i = pl.program_id(0)
    tq, tk = B_TQ, B_TK
    hcols = lambda hd: slice(hd * HEAD_DIM, (hd + 1) * HEAD_DIM)

    @pl.when(i == 0)
    def _():
        ones = jnp.ones((B_ONES_ROWS, tk), BF16)
        for hd in range(B_HEADS):
            def transpose_block(c, carry, hd=hd):
                start = pl.multiple_of(c * tk, tk)
                blk = v_ref[pl.ds(start, tk), hcols(hd)].astype(F32)
                vt_s[hd, c, :HEAD_DIM, :] = blk.T.astype(BF16)
                vt_s[hd, c, HEAD_DIM:, :] = ones
                return carry
            lax.fori_loop(0, SEQ // tk, transpose_block, 0)

    feat = lax.broadcasted_iota(jnp.int32, (HEAD_DIM, tq), 0)
    for hd in range(B_HEADS):
        qt = q_ref[:, hcols(hd)].astype(F32).T
        qst_s[hd, :, :tq] = jnp.where(feat < B_QK_DIM, qt, 0.0).astype(BF16)
        qst_s[hd, :, tq:] = jnp.where(feat >= B_QK_DIM, qt, 0.0).astype(BF16)
    acc_s[...] = jnp.zeros_like(acc_s)

    def scores(j, hd):
        kj = k_ref[pl.ds(pl.multiple_of(j * tk, tk), tk), hcols(hd)]
        return jnp.dot(kj, qst_s[hd], preferred_element_type=F32)

    def block(j, ms, diag):
        last = diag == n_diag - 1
        out = []
        for hd in range(B_HEADS):
            s = s_s[hd]
            if diag is not None:
                kc = diag * (tk // CHUNK) + lax.broadcasted_iota(jnp.int32, s.shape, 0) // CHUNK
                qc = (lax.broadcasted_iota(jnp.int32, s.shape, 1) % tq) // CHUNK
                s = jnp.where(kc <= qc, s, NEG_INF)
            m_new = jnp.maximum(ms[hd], s.max(axis=0, keepdims=True))
            a = jnp.exp2(ms[hd] - m_new)
            p = jnp.exp2(s - m_new).astype(BF16)
            acc_s[hd] = a * acc_s[hd] + jnp.dot(vt_s[hd, j], p, preferred_element_type=F32)
            if not last:
                s_s[hd] = scores(j + 1, hd)
            out.append(m_new)
        return tuple(out)

    n_diag = tq // tk
    for hd in range(B_HEADS):
        s_s[hd] = scores(0, hd)
    ms = tuple(jnp.full((1, 2 * tq), NEG_INF, F32) for _ in range(B_HEADS))
    ms = lax.fori_loop(0, i * n_diag, lambda j, c: block(j, c, None), ms)
    for d in range(n_diag):
        ms = block(i * n_diag + d, ms, d)

    lam = (jnp.exp(jnp.sum(lam_ref[0:1, :] * lam_ref[1:2, :], axis=-1, keepdims=True))
           - jnp.exp(jnp.sum(lam_ref[2:3, :] * lam_ref[3:4, :], axis=-1, keepdims=True))
           + lam_init)
    for hd in range(B_HEADS):
        acc = acc_s[hd]
        o_all = acc[:HEAD_DIM] * (1.0 / acc[HEAD_DIM:HEAD_DIM + 1])
        o = o_all[:, :tq] - lam * o_all[:, tq:]
        o = o * lax.rsqrt(jnp.mean(jnp.square(o), axis=0, keepdims=True) + LN_EPS)
        o = o * g_ref[...] * (1.0 - lam_init)
        o_ref[:, hcols(hd)] = o.T.astype(o_ref.dtype)


def _attn_b(h, lam_vecs, sub_g, layer, lam_init):
    tq, tk = B_TQ, B_TK
    width = B_HEADS * HEAD_DIM
    resident = lambda col: pl.BlockSpec((SEQ, width), lambda i: (0, col), pipeline_mode=pl.Buffered(1))
    return pl.pallas_call(
        functools.partial(_attn_b_kernel, lam_init=lam_init),
        grid=(SEQ // tq,),
        in_specs=[
            pl.BlockSpec((None, 4, B_QK_DIM), lambda i: (layer, 0, 0)),
            pl.BlockSpec((None, HEAD_DIM, 1), lambda i: (layer, 0, 0)),
            pl.BlockSpec((tq, width), lambda i: (i, BQ_OFF // width)),
            resident(BK_OFF // width),
            resident(BV_OFF // width),
        ],
        out_specs=pl.BlockSpec((tq, width), lambda i: (i, 0)),
        out_shape=jax.ShapeDtypeStruct((SEQ, width), BF16),
        scratch_shapes=[
            pltpu.VMEM((B_HEADS, SEQ // tk, HEAD_DIM + B_ONES_ROWS, tk), BF16),
            pltpu.VMEM((B_HEADS, HEAD_DIM, 2 * tq), BF16),
            pltpu.VMEM((B_HEADS, HEAD_DIM + B_ONES_ROWS, 2 * tq), F32),
            pltpu.VMEM((B_HEADS, tk, 2 * tq), F32),
        ],
        compiler_params=_params("arbitrary"),
        name="attn_b",
    )(lam_vecs, sub_g, h, h, h)


def _attn_c_kernel(q_ref, *refs):
    k_refs = refs[:C_KBLOCKS]
    v_refs = refs[C_KBLOCKS:2 * C_KBLOCKS]
    bias_ref = refs[2 * C_KBLOCKS]
    o_ref = refs[2 * C_KBLOCKS + 1]
    i = pl.program_id(0)
    dn = (((1,), (1,)), ((), ()))
    hcols = lambda hd: slice(hd * HEAD_DIM, (hd + 1) * HEAD_DIM)
    scores = [[lax.dot_general(q_ref[:, hcols(hd)], k_refs[b][:, hcols(hd)], dn, preferred_element_type=F32)
               for b in range(C_KBLOCKS)] for hd in range(C_HEADS)]
    for hd in range(C_HEADS):
        hsl = hcols(hd)
        s = []
        for b in range(C_KBLOCKS):
            sb = scores[hd][b] + bias_ref[hd, :, b * C_TQ:(b + 1) * C_TQ]
            s.append(jnp.where(i - (C_KBLOCKS - 1) + b >= 0, sb, NEG_INF))
        mx = functools.reduce(jnp.maximum, [sb.max(-1, keepdims=True) for sb in s])
        p = [jnp.exp(sb - mx) for sb in s]
        denom = functools.reduce(lambda a, b_: a + b_, [pb.sum(-1, keepdims=True) for pb in p])
        o = functools.reduce(
            lambda a, b_: a + b_,
            [jnp.dot(p[b].astype(BF16), v_refs[b][:, hsl], preferred_element_type=F32)
             for b in range(C_KBLOCKS)])
        o_ref[:, hsl] = (o * (1.0 / denom)).astype(o_ref.dtype)


def _attn_c(h, bias_c, layer):
    tq = C_TQ
    qb, kb, vb = CQ_OFF // 512, CK_OFF // 512, CV_OFF // 512

    def kv_spec(b, col):
        return pl.BlockSpec((tq, 512), lambda i: (jnp.maximum(i - (C_KBLOCKS - 1) + b, 0), col))

    return pl.pallas_call(
        _attn_c_kernel,
        grid=(SEQ // tq,),
        in_specs=([pl.BlockSpec((tq, 512), lambda i: (i, qb))]
                  + [kv_spec(b, kb) for b in range(C_KBLOCKS)]
                  + [kv_spec(b, vb) for b in range(C_KBLOCKS)]
                  + [pl.BlockSpec((None, C_HEADS, tq, C_KBLOCKS * tq), lambda i: (layer, 0, 0, 0))]),
        out_specs=pl.BlockSpec((tq, 512), lambda i: (i, 0)),
        out_shape=jax.ShapeDtypeStruct((SEQ, C_HEADS * HEAD_DIM), BF16),
        compiler_params=_params("arbitrary"),
        name="attn_c",
    )(h, *([h] * (2 * C_KBLOCKS)), bias_c)


def _mix_kernel(x_ref, ya_ref, yb_ref, yc_ref, wga_ref, wgb_ref, wgc_ref, ba_ref, bb_ref, bc_ref,
                wa_ref, wb_ref, wc_ref, o_ref, wg_s, wa_s, wb_s, wc_s):
    m = pl.program_id(1)

    @pl.when(m == 0)
    def _():
        wg_s[0] = wga_ref[...].astype(BF16)
        wg_s[1] = wgb_ref[...].astype(BF16)
        wg_s[2] = wgc_ref[...].astype(BF16)
        wa_s[...] = wa_ref[...].astype(BF16)
        wb_s[...] = wb_ref[...].astype(BF16)
        wc_s[...] = wc_ref[...].astype(BF16)

    for r in range(MIX_TM // ROW_CHUNK):
        rows = slice(r * ROW_CHUNK, (r + 1) * ROW_CHUNK)
        x = x_ref[rows, :]

        def branch(idx, b_ref, y_ref, w_s):
            gate = jax.nn.sigmoid(jnp.dot(x, wg_s[idx], preferred_element_type=F32) + b_ref[...])
            return gate * jnp.dot(y_ref[rows, :], w_s[...], preferred_element_type=F32)

        mix = branch(0, ba_ref, ya_ref, wa_s) + branch(1, bb_ref, yb_ref, wb_s) + branch(2, bc_ref, yc_ref, wc_s)
        o_ref[rows, :] = mix.astype(o_ref.dtype)


def _mix(xb, ya, yb, yc, w_gate, b_gate, w_br_a, w_br_b, w_br_c, layer):
    tm, tn = MIX_TM, MIX_TN
    nb = D_MODEL // tn
    row = lambda width: pl.BlockSpec((tm, width), lambda n, m: (m, 0))
    gate_w = lambda k: pl.BlockSpec((None, D_MODEL, tn), lambda n, m: (layer, 0, k * nb + n))
    gate_b = lambda k: pl.BlockSpec((None, 1, tn), lambda n, m: (layer, 0, k * nb + n))
    br_w = lambda width: pl.BlockSpec((None, width, tn), lambda n, m: (layer, 0, n))
    return pl.pallas_call(
        _mix_kernel,
        grid=(nb, SEQ // tm),
        in_specs=[row(D_MODEL), row(1024), row(512), row(512),
                  gate_w(0), gate_w(1), gate_w(2), gate_b(0), gate_b(1), gate_b(2),
                  br_w(1024), br_w(512), br_w(512)],
        out_specs=pl.BlockSpec((tm, tn), lambda n, m: (m, n)),
        out_shape=jax.ShapeDtypeStruct((SEQ, D_MODEL), BF16),
        scratch_shapes=[pltpu.VMEM((3, D_MODEL, tn), BF16), pltpu.VMEM((1024, tn), BF16),
                        pltpu.VMEM((512, tn), BF16), pltpu.VMEM((512, tn), BF16)],
        compiler_params=_params("arbitrary", "arbitrary"),
        name="mix",
    )(xb, ya, yb, yc, w_gate, w_gate, w_gate, b_gate, b_gate, b_gate, w_br_a, w_br_b, w_br_c)


def _matmul_ln_kernel(y_ref, w_ref, x_ref, g_ref, b_ref, of_ref, ob_ref, *, chunk):
    for r in range(y_ref.shape[0] // chunk):
        rows = slice(r * chunk, (r + 1) * chunk)
        z = DEEPNORM_ALPHA * x_ref[rows, :] + jnp.dot(y_ref[rows, :], w_ref[...], preferred_element_type=F32)
        mu = jnp.mean(z, axis=-1, keepdims=True)
        zc = z - mu
        var = jnp.mean(jnp.square(zc), axis=-1, keepdims=True)
        out = zc * lax.rsqrt(var + LN_EPS) * g_ref[...] + b_ref[...]
        of_ref[rows, :] = out
        ob_ref[rows, :] = out.astype(ob_ref.dtype)


def _matmul_ln(y, w_bf16, x, g, b, layer, tm, name):
    k = y.shape[1]
    vec = pl.BlockSpec((None, 1, D_MODEL), lambda m: (layer, 0, 0))
    return pl.pallas_call(
        functools.partial(_matmul_ln_kernel, chunk=LN_ROW_CHUNK),
        grid=(SEQ // tm,),
        in_specs=[
            pl.BlockSpec((tm, k), lambda m: (m, 0)),
            pl.BlockSpec((None, k, D_MODEL), lambda m: (layer, 0, 0), pipeline_mode=pl.Buffered(1)),
            pl.BlockSpec((tm, D_MODEL), lambda m: (m, 0)),
            vec, vec,
        ],
        out_specs=[pl.BlockSpec((tm, D_MODEL), lambda m: (m, 0)),
                   pl.BlockSpec((tm, D_MODEL), lambda m: (m, 0))],
        out_shape=[jax.ShapeDtypeStruct((SEQ, D_MODEL), F32),
                   jax.ShapeDtypeStruct((SEQ, D_MODEL), BF16)],
        compiler_params=_params("arbitrary"),
        name=name,
    )(y, w_bf16, x, g, b)


def _ffn_in_kernel(x_ref, wg_ref, wu_ref, o_ref, wg_s, wu_s):
    m = pl.program_id(1)

    @pl.when(m == 0)
    def _():
        wg_s[...] = wg_ref[...].astype(BF16)
        wu_s[...] = wu_ref[...].astype(BF16)

    for r in range(FFN_TM // ROW_CHUNK):
        rows = slice(r * ROW_CHUNK, (r + 1) * ROW_CHUNK)
        x = x_ref[rows, :]
        gate = jnp.dot(x, wg_s[...], preferred_element_type=F32)
        up = jnp.dot(x, wu_s[...], preferred_element_type=F32)
        o_ref[rows, :] = (jax.nn.silu(gate) * up).astype(o_ref.dtype)


def _ffn_in(xb, w_ffn_in, layer):
    tm, tn = FFN_TM, FFN_TN
    nb = FFN_HIDDEN // tn
    return pl.pallas_call(
        _ffn_in_kernel,
        grid=(nb, SEQ // tm),
        in_specs=[
            pl.BlockSpec((tm, D_MODEL), lambda n, m: (m, 0)),
            pl.BlockSpec((None, D_MODEL, tn), lambda n, m: (layer, 0, n)),
            pl.BlockSpec((None, D_MODEL, tn), lambda n, m: (layer, 0, nb + n)),
        ],
        out_specs=pl.BlockSpec((tm, tn), lambda n, m: (m, n)),
        out_shape=jax.ShapeDtypeStruct((SEQ, FFN_HIDDEN), BF16),
        scratch_shapes=[pltpu.VMEM((D_MODEL, tn), BF16), pltpu.VMEM((D_MODEL, tn), BF16)],
        compiler_params=_params("arbitrary", "arbitrary"),
        name="ffn_in",
    )(xb, w_ffn_in, w_ffn_in)


def _rope_tables():
    pos = jnp.arange(SEQ, dtype=F32)

    def cs(dim):
        inv = 1.0 / (ROPE_THETA ** (jnp.arange(0, dim, 2, dtype=F32) / dim))
        ang = pos[:, None] * inv[None, :]
        ang = jnp.concatenate([ang, ang], axis=-1)
        return jnp.cos(ang), jnp.sin(ang)

    cos_a, sin_a = cs(HEAD_DIM)
    half = HEAD_DIM // 2
    sin_a = jnp.concatenate([-sin_a[:, :half], sin_a[:, half:]], axis=-1)
    cos_b, sin_b = cs(B_QK_DIM)
    cos_b2 = jnp.concatenate([cos_b, cos_b], axis=-1)
    sin_b2 = jnp.concatenate([sin_b, sin_b], axis=-1)
    first_half = (np.arange(HEAD_DIM) % B_QK_DIM) < (B_QK_DIM // 2)
    sin_lo = jnp.where(first_half[None, :], -sin_b2, 0.0)
    sin_hi = jnp.where(first_half[None, :], 0.0, sin_b2)
    return cos_a, sin_a, cos_b2, sin_lo, sin_hi


def _mask_a():
    qc = (np.arange(A_TQ) // CHUNK)[:, None]
    kc = (np.arange(A_WINDOW + A_TQ) // CHUNK)[None, :]
    ok = (kc >= qc) & (kc <= qc + A_WINDOW // CHUNK)
    m = np.where(ok, 0.0, NEG_INF).astype(np.float32)
    return jnp.asarray(np.tile(m, (A_GROUP, 1)))


def _bias_c(rel_bias):
    nq, nk = C_TQ, C_KBLOCKS * C_TQ
    nt = nq + nk - 1
    lo = REL_CLIP - (nq - 1)
    rb = rel_bias.astype(F32)
    ramp = rb[..., lo:]
    flat_part = jnp.broadcast_to(rb[..., -1:], rb.shape[:-1] + (nt - ramp.shape[-1],))
    w = jnp.concatenate([ramp, flat_part], axis=-1)
    u = jnp.concatenate([w[..., ::-1], jnp.zeros(w.shape[:-1] + (1,), F32)], axis=-1)
    flat = jnp.tile(u, nq)[..., :nq * nt]
    skew = flat.reshape(flat.shape[:-1] + (nq, nt))
    bias = skew[..., nq - 1:]
    q = np.arange(nq)[:, None]
    k = np.arange(nk)[None, :]
    qc, kc = q // CHUNK, k // CHUNK
    ok = (kc >= qc) & (kc <= qc + C_PREV_CHUNKS)
    return jnp.where(ok, bias, NEG_INF)


def kernel(x, w_in, sinks, lambda_q1, lambda_k1, lambda_q2, lambda_k2, diff_norm_g, rel_bias,
           w_br_a, w_br_b, w_br_c, w_gate, b_gate, w_out, ln1_g, ln1_b,
           w_ffn_in, w_ffn_out, ln2_g, ln2_b):
    assert x.shape == (1, SEQ, D_MODEL)
    tabs = _rope_tables()
    mask_a = _mask_a()
    xf = x.reshape(SEQ, D_MODEL)
    xb = xf.astype(BF16)
    lam_vecs = jnp.stack([lambda_q1, lambda_k1, lambda_q2, lambda_k2], axis=1).astype(F32)
    sub_g = diff_norm_g.astype(F32).reshape(DEPTH, HEAD_DIM, 1)
    bias_c = _bias_c(rel_bias)
    b_gate3 = b_gate.reshape(DEPTH, 1, -1)
    vec3 = lambda v: v.reshape(DEPTH, 1, D_MODEL)
    ln1_g, ln1_b, ln2_g, ln2_b = vec3(ln1_g), vec3(ln1_b), vec3(ln2_g), vec3(ln2_b)
    w_out_b = w_out.astype(BF16)
    w_ffn_out_b = w_ffn_out.astype(BF16)
    for l in range(DEPTH):
        lam_init = 0.8 - 0.6 * math.exp(-0.3 * l)
        h = _in_proj(xb, w_in, l, tabs)
        ya = _attn_a(h, sinks, l, mask_a)
        yb = _attn_b(h, lam_vecs, sub_g, l, lam_init)
        yc = _attn_c(h, bias_c, l)
        mix = _mix(xb, ya, yb, yc, w_gate, b_gate3, w_br_a, w_br_b, w_br_c, l)
        xf, xb = _matmul_ln(mix, w_out_b, xf, ln1_g, ln1_b, l, OUT_TM, "out_ln")
        f = _ffn_in(xb, w_ffn_in, l)
        xf, xb = _matmul_ln(f, w_ffn_out_b, xf, ln2_g, ln2_b, l, FFN_OUT_TM, "ffn_out")
    return xf.reshape(1, SEQ, D_MODEL)
```

```python
import functools
import math

import jax
import jax.numpy as jnp
import numpy as np
from jax import lax
from jax.experimental import pallas as pl
from jax.experimental.pallas import tpu as pltpu

D_MODEL = 2048
SEQ = 8192
DEPTH = 4
CHUNK = 64
HEAD_DIM = 128
A_Q_HEADS = 8
A_KV_HEADS = 2
A_GROUP = A_Q_HEADS // A_KV_HEADS
B_HEADS = 4
B_QK_DIM = 64
C_HEADS = 4
C_PREV_CHUNKS = 8
REL_CLIP = 256
FFN_HIDDEN = 5632
IN_WIDTH = 4608
ROPE_THETA = 10000.0
LN_EPS = 1e-5
DEEPNORM_ALPHA = (2 * DEPTH) ** 0.25
NEG_INF = -1e30

BF16 = jnp.bfloat16
F32 = jnp.float32

VMEM_LIMIT_BYTES = 56 * 1024 * 1024

AQ_OFF, AK_OFF, AV_OFF = 0, 1024, 1280
BQ_OFF, BK_OFF, BV_OFF = 1536, 2048, 2560
CQ_OFF, CK_OFF, CV_OFF = 3072, 3584, 4096

LN_ROW_CHUNK = 128
ROW_CHUNK = 256
PROJ_TM, PROJ_TN = 1024, 1536
MIX_TM, MIX_TN = 512, 512
OUT_TM = 512
FFN_TM, FFN_TN = 1024, 512
FFN_OUT_TM = 256
A_TQ = 256
A_WINDOW = 128
B_TQ = 512
B_TK = 512
B_ONES_ROWS = 16
C_TQ = 256
C_KBLOCKS = (C_PREV_CHUNKS * CHUNK) // C_TQ + 1


def _params(*sem):
    return pltpu.CompilerParams(dimension_semantics=sem, vmem_limit_bytes=VMEM_LIMIT_BYTES)


def _rope_a(t, cos, sin_signed):
    return t * cos + pltpu.roll(t, HEAD_DIM // 2, 1) * sin_signed


def _rope_b(t, cos2, sin_lo, sin_hi):
    return t * cos2 + pltpu.roll(t, 96, 1) * sin_lo + pltpu.roll(t, 32, 1) * sin_hi


def _in_proj_kernel(x_ref, w_ref, cosa_ref, sina_ref, cosb_ref, sinb_lo_ref, sinb_hi_ref,
                    o_ref, wb_ref):
    n = pl.program_id(0)
    m = pl.program_id(1)

    @pl.when(m == 0)
    def _():
        wb_ref[...] = w_ref[...].astype(BF16)

    a_scale = HEAD_DIM ** -0.5
    b_scale = B_QK_DIM ** -0.5 * math.log2(math.e)

    def run(epilogue):
        for r in range(PROJ_TM // ROW_CHUNK):
            rows = slice(r * ROW_CHUNK, (r + 1) * ROW_CHUNK)
            acc = jnp.dot(x_ref[rows, :], wb_ref[...], preferred_element_type=F32)
            epilogue(acc, rows)

    def head_cols(acc, rows, lo, hi, fn, scale):
        for j in range(lo, hi):
            sl = slice(j * HEAD_DIM, (j + 1) * HEAD_DIM)
            r = fn(acc[:, sl])
            if scale != 1.0:
                r = r * scale
            o_ref[rows, sl] = r.astype(o_ref.dtype)

    def plain_cols(acc, rows, lo, scale=1.0):
        t = acc[:, lo:]
        if scale != 1.0:
            t = t * scale
        o_ref[rows, lo:] = t.astype(o_ref.dtype)

    def mixer_a(acc, rows):
        cos, sin = cosa_ref[rows, :], sina_ref[rows, :]
        rope = lambda t: _rope_a(t, cos, sin)
        head_cols(acc, rows, 0, A_Q_HEADS, rope, a_scale)
        head_cols(acc, rows, A_Q_HEADS, A_Q_HEADS + A_KV_HEADS, rope, 1.0)
        plain_cols(acc, rows, (A_Q_HEADS + A_KV_HEADS) * HEAD_DIM)

    def mixer_b(acc, rows):
        cos, lo_, hi_ = cosb_ref[rows, :], sinb_lo_ref[rows, :], sinb_hi_ref[rows, :]
        rope = lambda t: _rope_b(t, cos, lo_, hi_)
        head_cols(acc, rows, 0, B_HEADS, rope, b_scale)
        head_cols(acc, rows, B_HEADS, 2 * B_HEADS, rope, 1.0)
        plain_cols(acc, rows, 2 * B_HEADS * HEAD_DIM)

    def mixer_c(acc, rows):
        o_ref[rows, :C_HEADS * HEAD_DIM] = (acc[:, :C_HEADS * HEAD_DIM] * a_scale).astype(o_ref.dtype)
        plain_cols(acc, rows, C_HEADS * HEAD_DIM)

    pl.when(n == 0)(lambda: run(mixer_a))
    pl.when(n == 1)(lambda: run(mixer_b))
    pl.when(n == 2)(lambda: run(mixer_c))


def _in_proj(xb, w_in, layer, tabs):
    tm, tn = PROJ_TM, PROJ_TN
    rope_spec = pl.BlockSpec((tm, HEAD_DIM), lambda n, m: (m, 0))
    return pl.pallas_call(
        _in_proj_kernel,
        grid=(IN_WIDTH // tn, SEQ // tm),
        in_specs=[
            pl.BlockSpec((tm, D_MODEL), lambda n, m: (m, 0)),
            pl.BlockSpec((None, D_MODEL, tn), lambda n, m: (layer, 0, n)),
            rope_spec, rope_spec, rope_spec, rope_spec, rope_spec,
        ],
        out_specs=pl.BlockSpec((tm, tn), lambda n, m: (m, n)),
        out_shape=jax.ShapeDtypeStruct((SEQ, IN_WIDTH), BF16),
        scratch_shapes=[pltpu.VMEM((D_MODEL, tn), BF16)],
        compiler_params=_params("arbitrary", "arbitrary"),
        name="in_proj",
    )(xb, w_in, *tabs)


def _attn_a_kernel(sink_ref, q_ref, kp_ref, kc_ref, vp_ref, vc_ref, mask_ref, o_ref, *, layer):
    i = pl.program_id(0)
    has_prev = i > 0
    dn = (((1,), (1,)), ((), ()))
    hcols = lambda hd: slice(hd * HEAD_DIM, (hd + 1) * HEAD_DIM)
    scores = []
    for g in range(A_KV_HEADS):
        q = jnp.concatenate([q_ref[:, hcols(A_GROUP * g + j)] for j in range(A_GROUP)], axis=0)
        scores.append((lax.dot_general(q, kp_ref[:, hcols(g)], dn, preferred_element_type=F32),
                       lax.dot_general(q, kc_ref[:, hcols(g)], dn, preferred_element_type=F32)))
    for g in range(A_KV_HEADS):
        s_prev, s_cur = scores[g]
        s_prev = jnp.where(has_prev, s_prev + mask_ref[:, :A_WINDOW], NEG_INF)
        s_cur = s_cur + mask_ref[:, A_WINDOW:]
        sink = jnp.concatenate(
            [jnp.full((A_TQ, 1), sink_ref[layer, A_GROUP * g + j], F32) for j in range(A_GROUP)], axis=0)
        mx = jnp.maximum(jnp.maximum(s_prev.max(-1, keepdims=True), s_cur.max(-1, keepdims=True)), sink)
        p_prev = jnp.exp(s_prev - mx)
        p_cur = jnp.exp(s_cur - mx)
        denom = p_prev.sum(-1, keepdims=True) + p_cur.sum(-1, keepdims=True) + jnp.exp(sink - mx)
        o = (jnp.dot(p_prev.astype(BF16), vp_ref[:, hcols(g)], preferred_element_type=F32)
             + jnp.dot(p_cur.astype(BF16), vc_ref[:, hcols(g)], preferred_element_type=F32))
        o = o * (1.0 / denom)
        for j in range(A_GROUP):
            o_ref[:, hcols(A_GROUP * g + j)] = o[j * A_TQ:(j + 1) * A_TQ].astype(o_ref.dtype)


def _attn_a(h, sinks, layer, mask_a):
    tq, win = A_TQ, A_WINDOW
    kv_width = A_KV_HEADS * HEAD_DIM
    kblk, vblk = AK_OFF // kv_width, AV_OFF // kv_width
    prev = lambda i: jnp.maximum(i * (tq // win) - 1, 0)
    return pl.pallas_call(
        functools.partial(_attn_a_kernel, layer=layer),
        grid=(SEQ // tq,),
        in_specs=[
            pl.BlockSpec(memory_space=pltpu.SMEM),
            pl.BlockSpec((tq, A_Q_HEADS * HEAD_DIM), lambda i: (i, 0)),
            pl.BlockSpec((win, kv_width), lambda i: (prev(i), kblk)),
            pl.BlockSpec((tq, kv_width), lambda i: (i, kblk)),
            pl.BlockSpec((win, kv_width), lambda i: (prev(i), vblk)),
            pl.BlockSpec((tq, kv_width), lambda i: (i, vblk)),
            pl.BlockSpec((A_GROUP * tq, win + tq), lambda i: (0, 0)),
        ],
        out_specs=pl.BlockSpec((tq, A_Q_HEADS * HEAD_DIM), lambda i: (i, 0)),
        out_shape=jax.ShapeDtypeStruct((SEQ, A_Q_HEADS * HEAD_DIM), BF16),
        compiler_params=_params("arbitrary"),
        name="attn_a",
    )(sinks, h, h, h, h, h, mask_a)


def _attn_b_kernel(lam_ref, g_ref, q_ref, k_ref, v_ref, o_ref, vt_s, qst_s, acc_s, s_s, m_s, *, lam_init):
    i = pl.program_id(0)
    tq, tk = B_TQ, B_TK
    hcols = lambda hd: slice(hd * HEAD_DIM, (hd + 1) * HEAD_DIM)

    @pl.when(i == 0)
    def _():
        ones = jnp.ones((B_ONES_ROWS, tk), BF16)
        for hd in range(B_HEADS):
            def transpose_block(c, carry, hd=hd):
                start = pl.multiple_of(c * tk, tk)
                blk = v_ref[pl.ds(start, tk), hcols(hd)].astype(F32)
                vt_s[hd, c, :HEAD_DIM, :] = blk.T.astype(BF16)
                vt_s[hd, c, HEAD_DIM:, :] = ones
                return carry
            lax.fori_loop(0, SEQ // tk, transpose_block, 0)

    feat = lax.broadcasted_iota(jnp.int32, (HEAD_DIM, tq), 0)
    for hd in range(B_HEADS):
        qt = q_ref[:, hcols(hd)].astype(F32).T
        qst_s[hd, :, :tq] = jnp.where(feat < B_QK_DIM, qt, 0.0).astype(BF16)
        qst_s[hd, :, tq:] = jnp.where(feat >= B_QK_DIM, qt, 0.0).astype(BF16)
    acc_s[...] = jnp.zeros_like(acc_s)

    def scores(j, hd):
        kj = k_ref[pl.ds(pl.multiple_of(j * tk, tk), tk), hcols(hd)]
        return jnp.dot(kj, qst_s[hd], preferred_element_type=F32)

    def block(j, src, dst, diagonal=False):
        for hd in range(B_HEADS):
            if dst is not None:
                s_s[dst, hd] = scores(j + 1, hd)
            s = s_s[src, hd]
            if diagonal:
                kc = lax.broadcasted_iota(jnp.int32, s.shape, 0) // CHUNK
                qc = (lax.broadcasted_iota(jnp.int32, s.shape, 1) % tq) // CHUNK
                s = jnp.where(kc <= qc, s, NEG_INF)
            m_old = m_s[hd]
            m_new = jnp.maximum(m_old, s.max(axis=0, keepdims=True))
            m_s[hd] = m_new
            a = jnp.exp2(m_old - m_new)
            p = jnp.exp2(s - m_new).astype(BF16)
            acc_s[hd] = a * acc_s[hd] + jnp.dot(vt_s[hd, j], p, preferred_element_type=F32)

    assert tq == tk
    for hd in range(B_HEADS):
        s_s[0, hd] = scores(0, hd)
    m_s[...] = jnp.full(m_s.shape, NEG_INF, F32)

    def pair(t, carry):
        block(2 * t, 0, 1)
        block(2 * t + 1, 1, 0)
        return carry

    lax.fori_loop(0, i // 2, pair, 0)

    @pl.when(i % 2 == 0)
    def _():
        block(i, 0, None, diagonal=True)

    @pl.when(i % 2 == 1)
    def _():
        block(i - 1, 0, 1)
        block(i, 1, None, diagonal=True)

    lam = (jnp.exp(jnp.sum(lam_ref[0:1, :] * lam_ref[1:2, :], axis=-1, keepdims=True))
           - jnp.exp(jnp.sum(lam_ref[2:3, :] * lam_ref[3:4, :], axis=-1, keepdims=True))
           + lam_init)
    for hd in range(B_HEADS):
        acc = acc_s[hd]
        o_all = acc[:HEAD_DIM] * (1.0 / acc[HEAD_DIM:HEAD_DIM + 1])
        o = o_all[:, :tq] - lam * o_all[:, tq:]
        o = o * lax.rsqrt(jnp.mean(jnp.square(o), axis=0, keepdims=True) + LN_EPS)
        o = o * g_ref[...] * (1.0 - lam_init)
        o_ref[:, hcols(hd)] = o.T.astype(o_ref.dtype)


def _attn_b(h, lam_vecs, sub_g, layer, lam_init):
    tq, tk = B_TQ, B_TK
    width = B_HEADS * HEAD_DIM
    resident = lambda col: pl.BlockSpec((SEQ, width), lambda i: (0, col), pipeline_mode=pl.Buffered(1))
    return pl.pallas_call(
        functools.partial(_attn_b_kernel, lam_init=lam_init),
        grid=(SEQ // tq,),
        in_specs=[
            pl.BlockSpec((None, 4, B_QK_DIM), lambda i: (layer, 0, 0)),
            pl.BlockSpec((None, HEAD_DIM, 1), lambda i: (layer, 0, 0)),
            pl.BlockSpec((tq, width), lambda i: (i, BQ_OFF // width)),
            resident(BK_OFF // width),
            resident(BV_OFF // width),
        ],
        out_specs=pl.BlockSpec((tq, width), lambda i: (i, 0)),
        out_shape=jax.ShapeDtypeStruct((SEQ, width), BF16),
        scratch_shapes=[
            pltpu.VMEM((B_HEADS, SEQ // tk, HEAD_DIM + B_ONES_ROWS, tk), BF16),
            pltpu.VMEM((B_HEADS, HEAD_DIM, 2 * tq), BF16),
            pltpu.VMEM((B_HEADS, HEAD_DIM + B_ONES_ROWS, 2 * tq), F32),
            pltpu.VMEM((2, B_HEADS, tk, 2 * tq), F32),
            pltpu.VMEM((B_HEADS, 1, 2 * tq), F32),
        ],
        compiler_params=_params("arbitrary"),
        name="attn_b",
    )(lam_vecs, sub_g, h, h, h)


def _attn_c_kernel(q_ref, *refs):
    k_refs = refs[:C_KBLOCKS]
    v_refs = refs[C_KBLOCKS:2 * C_KBLOCKS]
    bias_ref = refs[2 * C_KBLOCKS]
    o_ref = refs[2 * C_KBLOCKS + 1]
    i = pl.program_id(0)
    dn = (((1,), (1,)), ((), ()))
    hcols = lambda hd: slice(hd * HEAD_DIM, (hd + 1) * HEAD_DIM)
    scores = [[lax.dot_general(q_ref[:, hcols(hd)], k_refs[b][:, hcols(hd)], dn, preferred_element_type=F32)
               for b in range(C_KBLOCKS)] for hd in range(C_HEADS)]
    for hd in range(C_HEADS):
        hsl = hcols(hd)
        s = []
        for b in range(C_KBLOCKS):
            sb = scores[hd][b] + bias_ref[hd, :, b * C_TQ:(b + 1) * C_TQ]
            s.append(jnp.where(i - (C_KBLOCKS - 1) + b >= 0, sb, NEG_INF))
        mx = functools.reduce(jnp.maximum, [sb.max(-1, keepdims=True) for sb in s])
        p = [jnp.exp(sb - mx) for sb in s]
        denom = functools.reduce(lambda a, b_: a + b_, [pb.sum(-1, keepdims=True) for pb in p])
        o = functools.reduce(
            lambda a, b_: a + b_,
            [jnp.dot(p[b].astype(BF16), v_refs[b][:, hsl], preferred_element_type=F32)
             for b in range(C_KBLOCKS)])
        o_ref[:, hsl] = (o * (1.0 / denom)).astype(o_ref.dtype)


def _attn_c(h, bias_c, layer):
    tq = C_TQ
    qb, kb, vb = CQ_OFF // 512, CK_OFF // 512, CV_OFF // 512

    def kv_spec(b, col):
        return pl.BlockSpec((tq, 512), lambda i: (jnp.maximum(i - (C_KBLOCKS - 1) + b, 0), col))

    return pl.pallas_call(
        _attn_c_kernel,
        grid=(SEQ // tq,),
        in_specs=([pl.BlockSpec((tq, 512), lambda i: (i, qb))]
                  + [kv_spec(b, kb) for b in range(C_KBLOCKS)]
                  + [kv_spec(b, vb) for b in range(C_KBLOCKS)]
                  + [pl.BlockSpec((None, C_HEADS, tq, C_KBLOCKS * tq), lambda i: (layer, 0, 0, 0))]),
        out_specs=pl.BlockSpec((tq, 512), lambda i: (i, 0)),
        out_shape=jax.ShapeDtypeStruct((SEQ, C_HEADS * HEAD_DIM), BF16),
        compiler_params=_params("arbitrary"),
        name="attn_c",
    )(h, *([h] * (2 * C_KBLOCKS)), bias_c)


def _mix_kernel(x_ref, ya_ref, yb_ref, yc_ref, wga_ref, wgb_ref, wgc_ref, ba_ref, bb_ref, bc_ref,
                wa_ref, wb_ref, wc_ref, o_ref, wg_s, wa_s, wb_s, wc_s):
    m = pl.program_id(1)

    @pl.when(m == 0)
    def _():
        wg_s[0] = wga_ref[...].astype(BF16)
        wg_s[1] = wgb_ref[...].astype(BF16)
        wg_s[2] = wgc_ref[...].astype(BF16)
        wa_s[...] = wa_ref[...].astype(BF16)
        wb_s[...] = wb_ref[...].astype(BF16)
        wc_s[...] = wc_ref[...].astype(BF16)

    for r in range(MIX_TM // ROW_CHUNK):
        rows = slice(r * ROW_CHUNK, (r + 1) * ROW_CHUNK)
        x = x_ref[rows, :]

        def branch(idx, b_ref, y_ref, w_s):
            gate = jax.nn.sigmoid(jnp.dot(x, wg_s[idx], preferred_element_type=F32) + b_ref[...])
            return gate * jnp.dot(y_ref[rows, :], w_s[...], preferred_element_type=F32)

        mix = branch(0, ba_ref, ya_ref, wa_s) + branch(1, bb_ref, yb_ref, wb_s) + branch(2, bc_ref, yc_ref, wc_s)
        o_ref[rows, :] = mix.astype(o_ref.dtype)


def _mix(xb, ya, yb, yc, w_gate, b_gate, w_br_a, w_br_b, w_br_c, layer):
    tm, tn = MIX_TM, MIX_TN
    nb = D_MODEL // tn
    row = lambda width: pl.BlockSpec((tm, width), lambda n, m: (m, 0))
    gate_w = lambda k: pl.BlockSpec((None, D_MODEL, tn), lambda n, m: (layer, 0, k * nb + n))
    gate_b = lambda k: pl.BlockSpec((None, 1, tn), lambda n, m: (layer, 0, k * nb + n))
    br_w = lambda width: pl.BlockSpec((None, width, tn), lambda n, m: (layer, 0, n))
    return pl.pallas_call(
        _mix_kernel,
        grid=(nb, SEQ // tm),
        in_specs=[row(D_MODEL), row(1024), row(512), row(512),
                  gate_w(0), gate_w(1), gate_w(2), gate_b(0), gate_b(1), gate_b(2),
                  br_w(1024), br_w(512), br_w(512)],
        out_specs=pl.BlockSpec((tm, tn), lambda n, m: (m, n)),
        out_shape=jax.ShapeDtypeStruct((SEQ, D_MODEL), BF16),
        scratch_shapes=[pltpu.VMEM((3, D_MODEL, tn), BF16), pltpu.VMEM((1024, tn), BF16),
                        pltpu.VMEM((512, tn), BF16), pltpu.VMEM((512, tn), BF16)],
        compiler_params=_params("arbitrary", "arbitrary"),
        name="mix",
    )(xb, ya, yb, yc, w_gate, w_gate, w_gate, b_gate, b_gate, b_gate, w_br_a, w_br_b, w_br_c)


def _matmul_ln_kernel(y_ref, w_ref, x_ref, g_ref, b_ref, of_ref, ob_ref, *, chunk):
    for r in range(y_ref.shape[0] // chunk):
        rows = slice(r * chunk, (r + 1) * chunk)
        z = DEEPNORM_ALPHA * x_ref[rows, :] + jnp.dot(y_ref[rows, :], w_ref[...], preferred_element_type=F32)
        mu = jnp.mean(z, axis=-1, keepdims=True)
        zc = z - mu
        var = jnp.mean(jnp.square(zc), axis=-1, keepdims=True)
        out = zc * lax.rsqrt(var + LN_EPS) * g_ref[...] + b_ref[...]
        of_ref[rows, :] = out
        ob_ref[rows, :] = out.astype(ob_ref.dtype)


def _matmul_ln(y, w_bf16, x, g, b, layer, tm, name):
    k = y.shape[1]
    vec = pl.BlockSpec((None, 1, D_MODEL), lambda m: (layer, 0, 0))
    return pl.pallas_call(
        functools.partial(_matmul_ln_kernel, chunk=LN_ROW_CHUNK),
        grid=(SEQ // tm,),
        in_specs=[
            pl.BlockSpec((tm, k), lambda m: (m, 0)),
            pl.BlockSpec((None, k, D_MODEL), lambda m: (layer, 0, 0), pipeline_mode=pl.Buffered(1)),
            pl.BlockSpec((tm, D_MODEL), lambda m: (m, 0)),
            vec, vec,
        ],
        out_specs=[pl.BlockSpec((tm, D_MODEL), lambda m: (m, 0)),
                   pl.BlockSpec((tm, D_MODEL), lambda m: (m, 0))],
        out_shape=[jax.ShapeDtypeStruct((SEQ, D_MODEL), F32),
                   jax.ShapeDtypeStruct((SEQ, D_MODEL), BF16)],
        compiler_params=_params("arbitrary"),
        name=name,
    )(y, w_bf16, x, g, b)


def _ffn_in_kernel(x_ref, wg_ref, wu_ref, o_ref, wg_s, wu_s):
    m = pl.program_id(1)

    @pl.when(m == 0)
    def _():
        wg_s[...] = wg_ref[...].astype(BF16)
        wu_s[...] = wu_ref[...].astype(BF16)

    for r in range(FFN_TM // ROW_CHUNK):
        rows = slice(r * ROW_CHUNK, (r + 1) * ROW_CHUNK)
        x = x_ref[rows, :]
        gate = jnp.dot(x, wg_s[...], preferred_element_type=F32)
        up = jnp.dot(x, wu_s[...], preferred_element_type=F32)
        o_ref[rows, :] = (jax.nn.silu(gate) * up).astype(o_ref.dtype)


def _ffn_in(xb, w_ffn_in, layer):
    tm, tn = FFN_TM, FFN_TN
    nb = FFN_HIDDEN // tn
    return pl.pallas_call(
        _ffn_in_kernel,
        grid=(nb, SEQ // tm),
        in_specs=[
            pl.BlockSpec((tm, D_MODEL), lambda n, m: (m, 0)),
            pl.BlockSpec((None, D_MODEL, tn), lambda n, m: (layer, 0, n)),
            pl.BlockSpec((None, D_MODEL, tn), lambda n, m: (layer, 0, nb + n)),
        ],
        out_specs=pl.BlockSpec((tm, tn), lambda n, m: (m, n)),
        out_shape=jax.ShapeDtypeStruct((SEQ, FFN_HIDDEN), BF16),
        scratch_shapes=[pltpu.VMEM((D_MODEL, tn), BF16), pltpu.VMEM((D_MODEL, tn), BF16)],
        compiler_params=_params("arbitrary", "arbitrary"),
        name="ffn_in",
    )(xb, w_ffn_in, w_ffn_in)


def _rope_tables():
    pos = jnp.arange(SEQ, dtype=F32)

    def cs(dim):
        inv = 1.0 / (ROPE_THETA ** (jnp.arange(0, dim, 2, dtype=F32) / dim))
        ang = pos[:, None] * inv[None, :]
        ang = jnp.concatenate([ang, ang], axis=-1)
        return jnp.cos(ang), jnp.sin(ang)

    cos_a, sin_a = cs(HEAD_DIM)
    half = HEAD_DIM // 2
    sin_a = jnp.concatenate([-sin_a[:, :half], sin_a[:, half:]], axis=-1)
    cos_b, sin_b = cs(B_QK_DIM)
    cos_b2 = jnp.concatenate([cos_b, cos_b], axis=-1)
    sin_b2 = jnp.concatenate([sin_b, sin_b], axis=-1)
    first_half = (np.arange(HEAD_DIM) % B_QK_DIM) < (B_QK_DIM // 2)
    sin_lo = jnp.where(first_half[None, :], -sin_b2, 0.0)
    sin_hi = jnp.where(first_half[None, :], 0.0, sin_b2)
    return cos_a, sin_a, cos_b2, sin_lo, sin_hi


def _mask_a():
    qc = (np.arange(A_TQ) // CHUNK)[:, None]
    kc = (np.arange(A_WINDOW + A_TQ) // CHUNK)[None, :]
    ok = (kc >= qc) & (kc <= qc + A_WINDOW // CHUNK)
    m = np.where(ok, 0.0, NEG_INF).astype(np.float32)
    return jnp.asarray(np.tile(m, (A_GROUP, 1)))


def _bias_c(rel_bias):
    nq, nk = C_TQ, C_KBLOCKS * C_TQ
    nt = nq + nk - 1
    lo = REL_CLIP - (nq - 1)
    rb = rel_bias.astype(F32)
    ramp = rb[..., lo:]
    flat_part = jnp.broadcast_to(rb[..., -1:], rb.shape[:-1] + (nt - ramp.shape[-1],))
    w = jnp.concatenate([ramp, flat_part], axis=-1)
    u = jnp.concatenate([w[..., ::-1], jnp.zeros(w.shape[:-1] + (1,), F32)], axis=-1)
    flat = jnp.tile(u, nq)[..., :nq * nt]
    skew = flat.reshape(flat.shape[:-1] + (nq, nt))
    bias = skew[..., nq - 1:]
    q = np.arange(nq)[:, None]
    k = np.arange(nk)[None, :]
    qc, kc = q // CHUNK, k // CHUNK
    ok = (kc >= qc) & (kc <= qc + C_PREV_CHUNKS)
    return jnp.where(ok, bias, NEG_INF)


def kernel(x, w_in, sinks, lambda_q1, lambda_k1, lambda_q2, lambda_k2, diff_norm_g, rel_bias,
           w_br_a, w_br_b, w_br_c, w_gate, b_gate, w_out, ln1_g, ln1_b,
           w_ffn_in, w_ffn_out, ln2_g, ln2_b):
    assert x.shape == (1, SEQ, D_MODEL)
    tabs = _rope_tables()
    mask_a = _mask_a()
    xf = x.reshape(SEQ, D_MODEL)
    xb = xf.astype(BF16)
    lam_vecs = jnp.stack([lambda_q1, lambda_k1, lambda_q2, lambda_k2], axis=1).astype(F32)
    sub_g = diff_norm_g.astype(F32).reshape(DEPTH, HEAD_DIM, 1)
    bias_c = _bias_c(rel_bias)
    b_gate3 = b_gate.reshape(DEPTH, 1, -1)
    vec3 = lambda v: v.reshape(DEPTH, 1, D_MODEL)
    ln1_g, ln1_b, ln2_g, ln2_b = vec3(ln1_g), vec3(ln1_b), vec3(ln2_g), vec3(ln2_b)
    w_out_b = w_out.astype(BF16)
    w_ffn_out_b = w_ffn_out.astype(BF16)
    for l in range(DEPTH):
        lam_init = 0.8 - 0.6 * math.exp(-0.3 * l)
        h = _in_proj(xb, w_in, l, tabs)
        ya = _attn_a(h, sinks, l, mask_a)
        yb = _attn_b(h, lam_vecs, sub_g, l, lam_init)
        yc = _attn_c(h, bias_c, l)
        mix = _mix(xb, ya, yb, yc, w_gate, b_gate3, w_br_a, w_br_b, w_br_c, l)
        xf, xb = _matmul_ln(mix, w_out_b, xf, ln1_g, ln1_b, l, OUT_TM, "out_ln")
        f = _ffn_in(xb, w_ffn_in, l)
        xf, xb = _matmul_ln(f, w_ffn_out_b, xf, ln2_g, ln2_b, l, FFN_OUT_TM, "ffn_out")
    return xf.reshape(1, SEQ, D_MODEL)
```

```python
import functools
import math

import jax
import jax.numpy as jnp
import numpy as np
from jax import lax
from jax.experimental import pallas as pl
from jax.experimental.pallas import tpu as pltpu

D_MODEL = 2048
SEQ = 8192
DEPTH = 4
CHUNK = 64
HEAD_DIM = 128
A_Q_HEADS = 8
A_KV_HEADS = 2
A_GROUP = A_Q_HEADS // A_KV_HEADS
B_HEADS = 4
B_QK_DIM = 64
C_HEADS = 4
C_PREV_CHUNKS = 8
REL_CLIP = 256
FFN_HIDDEN = 5632
IN_WIDTH = 4608
ROPE_THETA = 10000.0
LN_EPS = 1e-5
DEEPNORM_ALPHA = (2 * DEPTH) ** 0.25
NEG_INF = -1e30

BF16 = jnp.bfloat16
F32 = jnp.float32

VMEM_LIMIT_BYTES = 56 * 1024 * 1024

AQ_OFF, AK_OFF, AV_OFF = 0, 1024, 1280
BQ_OFF, BK_OFF, BV_OFF = 1536, 2048, 2560
CQ_OFF, CK_OFF, CV_OFF = 3072, 3584, 4096

LN_ROW_CHUNK = 128
ROW_CHUNK = 256
PROJ_TM, PROJ_TN = 1024, 1536
MIX_TM, MIX_TN = 512, 512
OUT_TM = 512
FFN_TM, FFN_TN = 1024, 512
FFN_OUT_TM = 256
A_TQ = 256
A_WINDOW = 128
B_TQ = 512
B_TK = 512
B_ONES_ROWS = 16
C_TQ = 256
C_KBLOCKS = (C_PREV_CHUNKS * CHUNK) // C_TQ + 1


def _params(*sem):
    return pltpu.CompilerParams(dimension_semantics=sem, vmem_limit_bytes=VMEM_LIMIT_BYTES)


def _rope_a(t, cos, sin_signed):
    return t * cos + pltpu.roll(t, HEAD_DIM // 2, 1) * sin_signed


def _rope_b(t, cos2, sin_lo, sin_hi):
    return t * cos2 + pltpu.roll(t, 96, 1) * sin_lo + pltpu.roll(t, 32, 1) * sin_hi


def _in_proj_kernel(x_ref, w_ref, cosa_ref, sina_ref, cosb_ref, sinb_lo_ref, sinb_hi_ref,
                    o_ref, wb_ref):
    n = pl.program_id(0)
    m = pl.program_id(1)

    @pl.when(m == 0)
    def _():
        wb_ref[...] = w_ref[...].astype(BF16)

    a_scale = HEAD_DIM ** -0.5
    b_scale = B_QK_DIM ** -0.5 * math.log2(math.e)

    def run(epilogue):
        for r in range(PROJ_TM // ROW_CHUNK):
            rows = slice(r * ROW_CHUNK, (r + 1) * ROW_CHUNK)
            acc = jnp.dot(x_ref[rows, :], wb_ref[...], preferred_element_type=F32)
            epilogue(acc, rows)

    def head_cols(acc, rows, lo, hi, fn, scale):
        for j in range(lo, hi):
            sl = slice(j * HEAD_DIM, (j + 1) * HEAD_DIM)
            r = fn(acc[:, sl])
            if scale != 1.0:
                r = r * scale
            o_ref[rows, sl] = r.astype(o_ref.dtype)

    def plain_cols(acc, rows, lo, scale=1.0):
        t = acc[:, lo:]
        if scale != 1.0:
            t = t * scale
        o_ref[rows, lo:] = t.astype(o_ref.dtype)

    def mixer_a(acc, rows):
        cos, sin = cosa_ref[rows, :], sina_ref[rows, :]
        rope = lambda t: _rope_a(t, cos, sin)
        head_cols(acc, rows, 0, A_Q_HEADS, rope, a_scale)
        head_cols(acc, rows, A_Q_HEADS, A_Q_HEADS + A_KV_HEADS, rope, 1.0)
        plain_cols(acc, rows, (A_Q_HEADS + A_KV_HEADS) * HEAD_DIM)

    def mixer_b(acc, rows):
        cos, lo_, hi_ = cosb_ref[rows, :], sinb_lo_ref[rows, :], sinb_hi_ref[rows, :]
        rope = lambda t: _rope_b(t, cos, lo_, hi_)
        head_cols(acc, rows, 0, B_HEADS, rope, b_scale)
        head_cols(acc, rows, B_HEADS, 2 * B_HEADS, rope, 1.0)
        plain_cols(acc, rows, 2 * B_HEADS * HEAD_DIM)

    def mixer_c(acc, rows):
        o_ref[rows, :C_HEADS * HEAD_DIM] = (acc[:, :C_HEADS * HEAD_DIM] * a_scale).astype(o_ref.dtype)
        plain_cols(acc, rows, C_HEADS * HEAD_DIM)

    pl.when(n == 0)(lambda: run(mixer_a))
    pl.when(n == 1)(lambda: run(mixer_b))
    pl.when(n == 2)(lambda: run(mixer_c))


def _in_proj(xb, w_in, layer, tabs):
    tm, tn = PROJ_TM, PROJ_TN
    rope_spec = pl.BlockSpec((tm, HEAD_DIM), lambda n, m: (m, 0))
    return pl.pallas_call(
        _in_proj_kernel,
        grid=(IN_WIDTH // tn, SEQ // tm),
        in_specs=[
            pl.BlockSpec((tm, D_MODEL), lambda n, m: (m, 0)),
            pl.BlockSpec((None, D_MODEL, tn), lambda n, m: (layer, 0, n)),
            rope_spec, rope_spec, rope_spec, rope_spec, rope_spec,
        ],
        out_specs=pl.BlockSpec((tm, tn), lambda n, m: (m, n)),
        out_shape=jax.ShapeDtypeStruct((SEQ, IN_WIDTH), BF16),
        scratch_shapes=[pltpu.VMEM((D_MODEL, tn), BF16)],
        compiler_params=_params("arbitrary", "arbitrary"),
        name="in_proj",
    )(xb, w_in, *tabs)


def _attn_a_kernel(sink_ref, q_ref, kp_ref, kc_ref, vp_ref, vc_ref, mask_ref, o_ref, *, layer):
    i = pl.program_id(0)
    has_prev = i > 0
    dn = (((1,), (1,)), ((), ()))
    hcols = lambda hd: slice(hd * HEAD_DIM, (hd + 1) * HEAD_DIM)
    scores = []
    for g in range(A_KV_HEADS):
        q = jnp.concatenate([q_ref[:, hcols(A_GROUP * g + j)] for j in range(A_GROUP)], axis=0)
        scores.append((lax.dot_general(q, kp_ref[:, hcols(g)], dn, preferred_element_type=F32),
                       lax.dot_general(q, kc_ref[:, hcols(g)], dn, preferred_element_type=F32)))
    for g in range(A_KV_HEADS):
        s_prev, s_cur = scores[g]
        s_prev = jnp.where(has_prev, s_prev + mask_ref[:, :A_WINDOW], NEG_INF)
        s_cur = s_cur + mask_ref[:, A_WINDOW:]
        sink = jnp.concatenate(
            [jnp.full((A_TQ, HEAD_DIM), sink_ref[layer, A_GROUP * g + j], F32) for j in range(A_GROUP)], axis=0)
        folded = functools.reduce(
            jnp.maximum, [s_prev] + [s_cur[:, c:c + HEAD_DIM] for c in range(0, A_TQ, HEAD_DIM)])
        mx = jnp.maximum(jnp.broadcast_to(folded.max(-1, keepdims=True), sink.shape), sink)
        p_prev = jnp.exp(s_prev - mx)
        p_cur = jnp.exp(s_cur - jnp.concatenate([mx] * (A_TQ // HEAD_DIM), axis=1))
        ones = lambda rows: jnp.ones((rows, HEAD_DIM), BF16)
        v_prev = jnp.concatenate([vp_ref[:, hcols(g)], ones(A_WINDOW)], axis=1)
        v_cur = jnp.concatenate([vc_ref[:, hcols(g)], ones(A_TQ)], axis=1)
        o_ext = (jnp.dot(p_prev.astype(BF16), v_prev, preferred_element_type=F32)
                 + jnp.dot(p_cur.astype(BF16), v_cur, preferred_element_type=F32))
        denom = o_ext[:, HEAD_DIM:] + jnp.exp(sink - mx)
        o = o_ext[:, :HEAD_DIM] * (1.0 / denom)
        for j in range(A_GROUP):
            o_ref[:, hcols(A_GROUP * g + j)] = o[j * A_TQ:(j + 1) * A_TQ].astype(o_ref.dtype)


def _attn_a(h, sinks, layer, mask_a):
    tq, win = A_TQ, A_WINDOW
    kv_width = A_KV_HEADS * HEAD_DIM
    kblk, vblk = AK_OFF // kv_width, AV_OFF // kv_width
    prev = lambda i: jnp.maximum(i * (tq // win) - 1, 0)
    return pl.pallas_call(
        functools.partial(_attn_a_kernel, layer=layer),
        grid=(SEQ // tq,),
        in_specs=[
            pl.BlockSpec(memory_space=pltpu.SMEM),
            pl.BlockSpec((tq, A_Q_HEADS * HEAD_DIM), lambda i: (i, 0)),
            pl.BlockSpec((win, kv_width), lambda i: (prev(i), kblk)),
            pl.BlockSpec((tq, kv_width), lambda i: (i, kblk)),
            pl.BlockSpec((win, kv_width), lambda i: (prev(i), vblk)),
            pl.BlockSpec((tq, kv_width), lambda i: (i, vblk)),
            pl.BlockSpec((A_GROUP * tq, win + tq), lambda i: (0, 0)),
        ],
        out_specs=pl.BlockSpec((tq, A_Q_HEADS * HEAD_DIM), lambda i: (i, 0)),
        out_shape=jax.ShapeDtypeStruct((SEQ, A_Q_HEADS * HEAD_DIM), BF16),
        compiler_params=_params("arbitrary"),
        name="attn_a",
    )(sinks, h, h, h, h, h, mask_a)


def _attn_b_kernel(lam_ref, g_ref, q_ref, k_ref, v_ref, o_ref, vt_s, qst_s, acc_s, s_s, m_s, *, lam_init):
    i = pl.program_id(0)
    tq, tk = B_TQ, B_TK
    hcols = lambda hd: slice(hd * HEAD_DIM, (hd + 1) * HEAD_DIM)

    @pl.when(i == 0)
    def _():
        ones = jnp.ones((B_ONES_ROWS, tk), BF16)
        for hd in range(B_HEADS):
            def transpose_block(c, carry, hd=hd):
                start = pl.multiple_of(c * tk, tk)
                blk = v_ref[pl.ds(start, tk), hcols(hd)].astype(F32)
                vt_s[hd, c, :HEAD_DIM, :] = blk.T.astype(BF16)
                vt_s[hd, c, HEAD_DIM:, :] = ones
                return carry
            lax.fori_loop(0, SEQ // tk, transpose_block, 0)

    feat = lax.broadcasted_iota(jnp.int32, (HEAD_DIM, tq), 0)
    for hd in range(B_HEADS):
        qt = q_ref[:, hcols(hd)].astype(F32).T
        qst_s[hd, :, :tq] = jnp.where(feat < B_QK_DIM, qt, 0.0).astype(BF16)
        qst_s[hd, :, tq:] = jnp.where(feat >= B_QK_DIM, qt, 0.0).astype(BF16)
    acc_s[...] = jnp.zeros_like(acc_s)

    def scores(j, hd):
        kj = k_ref[pl.ds(pl.multiple_of(j * tk, tk), tk), hcols(hd)]
        return jnp.dot(kj, qst_s[hd], preferred_element_type=F32)

    def block(j, src, dst, diagonal=False):
        for hd in range(B_HEADS):
            if dst is not None:
                s_s[dst, hd] = scores(j + 1, hd)
            s = s_s[src, hd]
            if diagonal:
                kc = lax.broadcasted_iota(jnp.int32, s.shape, 0) // CHUNK
                qc = (lax.broadcasted_iota(jnp.int32, s.shape, 1) % tq) // CHUNK
                s = jnp.where(kc <= qc, s, NEG_INF)
            m_old = m_s[hd]
            m_new = jnp.maximum(m_old, s.max(axis=0, keepdims=True))
            m_s[hd] = m_new
            a = jnp.exp2(m_old - m_new)
            p = jnp.exp2(s - m_new).astype(BF16)
            acc_s[hd] = a * acc_s[hd] + jnp.dot(vt_s[hd, j], p, preferred_element_type=F32)

    assert tq == tk
    for hd in range(B_HEADS):
        s_s[0, hd] = scores(0, hd)
    m_s[...] = jnp.full(m_s.shape, NEG_INF, F32)

    def pair(t, carry):
        block(2 * t, 0, 1)
        block(2 * t + 1, 1, 0)
        return carry

    lax.fori_loop(0, i // 2, pair, 0)

    @pl.when(i % 2 == 0)
    def _():
        block(i, 0, None, diagonal=True)

    @pl.when(i % 2 == 1)
    def _():
        block(i - 1, 0, 1)
        block(i, 1, None, diagonal=True)

    lam = (jnp.exp(jnp.sum(lam_ref[0:1, :] * lam_ref[1:2, :], axis=-1, keepdims=True))
           - jnp.exp(jnp.sum(lam_ref[2:3, :] * lam_ref[3:4, :], axis=-1, keepdims=True))
           + lam_init)
    for hd in range(B_HEADS):
        acc = acc_s[hd]
        o_all = acc[:HEAD_DIM] * (1.0 / acc[HEAD_DIM:HEAD_DIM + 1])
        o = o_all[:, :tq] - lam * o_all[:, tq:]
        o = o * lax.rsqrt(jnp.mean(jnp.square(o), axis=0, keepdims=True) + LN_EPS)
        o = o * g_ref[...] * (1.0 - lam_init)
        o_ref[:, hcols(hd)] = o.T.astype(o_ref.dtype)


def _attn_b(h, lam_vecs, sub_g, layer, lam_init):
    tq, tk = B_TQ, B_TK
    width = B_HEADS * HEAD_DIM
    resident = lambda col: pl.BlockSpec((SEQ, width), lambda i: (0, col), pipeline_mode=pl.Buffered(1))
    return pl.pallas_call(
        functools.partial(_attn_b_kernel, lam_init=lam_init),
        grid=(SEQ // tq,),
        in_specs=[
            pl.BlockSpec((None, 4, B_QK_DIM), lambda i: (layer, 0, 0)),
            pl.BlockSpec((None, HEAD_DIM, 1), lambda i: (layer, 0, 0)),
            pl.BlockSpec((tq, width), lambda i: (i, BQ_OFF // width)),
            resident(BK_OFF // width),
            resident(BV_OFF // width),
        ],
        out_specs=pl.BlockSpec((tq, width), lambda i: (i, 0)),
        out_shape=jax.ShapeDtypeStruct((SEQ, width), BF16),
        scratch_shapes=[
            pltpu.VMEM((B_HEADS, SEQ // tk, HEAD_DIM + B_ONES_ROWS, tk), BF16),
            pltpu.VMEM((B_HEADS, HEAD_DIM, 2 * tq), BF16),
            pltpu.VMEM((B_HEADS, HEAD_DIM + B_ONES_ROWS, 2 * tq), F32),
            pltpu.VMEM((2, B_HEADS, tk, 2 * tq), F32),
            pltpu.VMEM((B_HEADS, 1, 2 * tq), F32),
        ],
        compiler_params=_params("arbitrary"),
        name="attn_b",
    )(lam_vecs, sub_g, h, h, h)


def _attn_c_kernel(q_ref, *refs):
    k_refs = refs[:C_KBLOCKS]
    v_refs = refs[C_KBLOCKS:2 * C_KBLOCKS]
    bias_ref = refs[2 * C_KBLOCKS]
    o_ref = refs[2 * C_KBLOCKS + 1]
    i = pl.program_id(0)
    dn = (((1,), (1,)), ((), ()))
    hcols = lambda hd: slice(hd * HEAD_DIM, (hd + 1) * HEAD_DIM)
    scores = [[lax.dot_general(q_ref[:, hcols(hd)], k_refs[b][:, hcols(hd)], dn, preferred_element_type=F32)
               for b in range(C_KBLOCKS)] for hd in range(C_HEADS)]
    for hd in range(C_HEADS):
        hsl = hcols(hd)
        s = []
        for b in range(C_KBLOCKS):
            sb = scores[hd][b] + bias_ref[hd, :, b * C_TQ:(b + 1) * C_TQ]
            s.append(jnp.where(i - (C_KBLOCKS - 1) + b >= 0, sb, NEG_INF))
        mx = functools.reduce(jnp.maximum, [sb.max(-1, keepdims=True) for sb in s])
        p = [jnp.exp(sb - mx) for sb in s]
        ones = jnp.ones((C_TQ, HEAD_DIM), BF16)
        o_ext = functools.reduce(
            lambda a, b_: a + b_,
            [jnp.dot(p[b].astype(BF16), jnp.concatenate([v_refs[b][:, hsl], ones], axis=1),
                     preferred_element_type=F32) for b in range(C_KBLOCKS)])
        o_ref[:, hsl] = (o_ext[:, :HEAD_DIM] * (1.0 / o_ext[:, HEAD_DIM:])).astype(o_ref.dtype)


def _attn_c(h, bias_c, layer):
    tq = C_TQ
    qb, kb, vb = CQ_OFF // 512, CK_OFF // 512, CV_OFF // 512

    def kv_spec(b, col):
        return pl.BlockSpec((tq, 512), lambda i: (jnp.maximum(i - (C_KBLOCKS - 1) + b, 0), col))

    return pl.pallas_call(
        _attn_c_kernel,
        grid=(SEQ // tq,),
        in_specs=([pl.BlockSpec((tq, 512), lambda i: (i, qb))]
                  + [kv_spec(b, kb) for b in range(C_KBLOCKS)]
                  + [kv_spec(b, vb) for b in range(C_KBLOCKS)]
                  + [pl.BlockSpec((None, C_HEADS, tq, C_KBLOCKS * tq), lambda i: (layer, 0, 0, 0))]),
        out_specs=pl.BlockSpec((tq, 512), lambda i: (i, 0)),
        out_shape=jax.ShapeDtypeStruct((SEQ, C_HEADS * HEAD_DIM), BF16),
        compiler_params=_params("arbitrary"),
        name="attn_c",
    )(h, *([h] * (2 * C_KBLOCKS)), bias_c)


def _mix_kernel(x_ref, ya_ref, yb_ref, yc_ref, wga_ref, wgb_ref, wgc_ref, ba_ref, bb_ref, bc_ref,
                wa_ref, wb_ref, wc_ref, o_ref, wg_s, wa_s, wb_s, wc_s):
    m = pl.program_id(1)

    @pl.when(m == 0)
    def _():
        wg_s[0] = wga_ref[...].astype(BF16)
        wg_s[1] = wgb_ref[...].astype(BF16)
        wg_s[2] = wgc_ref[...].astype(BF16)
        wa_s[...] = wa_ref[...].astype(BF16)
        wb_s[...] = wb_ref[...].astype(BF16)
        wc_s[...] = wc_ref[...].astype(BF16)

    for r in range(MIX_TM // ROW_CHUNK):
        rows = slice(r * ROW_CHUNK, (r + 1) * ROW_CHUNK)
        x = x_ref[rows, :]

        def branch(idx, b_ref, y_ref, w_s):
            gate = jax.nn.sigmoid(jnp.dot(x, wg_s[idx], preferred_element_type=F32) + b_ref[...])
            return gate * jnp.dot(y_ref[rows, :], w_s[...], preferred_element_type=F32)

        mix = branch(0, ba_ref, ya_ref, wa_s) + branch(1, bb_ref, yb_ref, wb_s) + branch(2, bc_ref, yc_ref, wc_s)
        o_ref[rows, :] = mix.astype(o_ref.dtype)


def _mix(xb, ya, yb, yc, w_gate, b_gate, w_br_a, w_br_b, w_br_c, layer):
    tm, tn = MIX_TM, MIX_TN
    nb = D_MODEL // tn
    row = lambda width: pl.BlockSpec((tm, width), lambda n, m: (m, 0))
    gate_w = lambda k: pl.BlockSpec((None, D_MODEL, tn), lambda n, m: (layer, 0, k * nb + n))
    gate_b = lambda k: pl.BlockSpec((None, 1, tn), lambda n, m: (layer, 0, k * nb + n))
    br_w = lambda width: pl.BlockSpec((None, width, tn), lambda n, m: (layer, 0, n))
    return pl.pallas_call(
        _mix_kernel,
        grid=(nb, SEQ // tm),
        in_specs=[row(D_MODEL), row(1024), row(512), row(512),
                  gate_w(0), gate_w(1), gate_w(2), gate_b(0), gate_b(1), gate_b(2),
                  br_w(1024), br_w(512), br_w(512)],
        out_specs=pl.BlockSpec((tm, tn), lambda n, m: (m, n)),
        out_shape=jax.ShapeDtypeStruct((SEQ, D_MODEL), BF16),
        scratch_shapes=[pltpu.VMEM((3, D_MODEL, tn), BF16), pltpu.VMEM((1024, tn), BF16),
                        pltpu.VMEM((512, tn), BF16), pltpu.VMEM((512, tn), BF16)],
        compiler_params=_params("arbitrary", "arbitrary"),
        name="mix",
    )(xb, ya, yb, yc, w_gate, w_gate, w_gate, b_gate, b_gate, b_gate, w_br_a, w_br_b, w_br_c)


def _matmul_ln_kernel(y_ref, w_ref, x_ref, g_ref, b_ref, of_ref, ob_ref, *, chunk):
    for r in range(y_ref.shape[0] // chunk):
        rows = slice(r * chunk, (r + 1) * chunk)
        z = DEEPNORM_ALPHA * x_ref[rows, :] + jnp.dot(y_ref[rows, :], w_ref[...], preferred_element_type=F32)
        mu = jnp.mean(z, axis=-1, keepdims=True)
        zc = z - mu
        var = jnp.mean(jnp.square(zc), axis=-1, keepdims=True)
        out = zc * lax.rsqrt(var + LN_EPS) * g_ref[...] + b_ref[...]
        of_ref[rows, :] = out
        ob_ref[rows, :] = out.astype(ob_ref.dtype)


def _matmul_ln(y, w_bf16, x, g, b, layer, tm, name):
    k = y.shape[1]
    vec = pl.BlockSpec((None, 1, D_MODEL), lambda m: (layer, 0, 0))
    return pl.pallas_call(
        functools.partial(_matmul_ln_kernel, chunk=LN_ROW_CHUNK),
        grid=(SEQ // tm,),
        in_specs=[
            pl.BlockSpec((tm, k), lambda m: (m, 0)),
            pl.BlockSpec((None, k, D_MODEL), lambda m: (layer, 0, 0), pipeline_mode=pl.Buffered(1)),
            pl.BlockSpec((tm, D_MODEL), lambda m: (m, 0)),
            vec, vec,
        ],
        out_specs=[pl.BlockSpec((tm, D_MODEL), lambda m: (m, 0)),
                   pl.BlockSpec((tm, D_MODEL), lambda m: (m, 0))],
        out_shape=[jax.ShapeDtypeStruct((SEQ, D_MODEL), F32),
                   jax.ShapeDtypeStruct((SEQ, D_MODEL), BF16)],
        compiler_params=_params("arbitrary"),
        name=name,
    )(y, w_bf16, x, g, b)


def _ffn_in_kernel(x_ref, wg_ref, wu_ref, o_ref, wg_s, wu_s):
    m = pl.program_id(1)

    @pl.when(m == 0)
    def _():
        wg_s[...] = wg_ref[...].astype(BF16)
        wu_s[...] = wu_ref[...].astype(BF16)

    for r in range(FFN_TM // ROW_CHUNK):
        rows = slice(r * ROW_CHUNK, (r + 1) * ROW_CHUNK)
        x = x_ref[rows, :]
        gate = jnp.dot(x, wg_s[...], preferred_element_type=F32)
        up = jnp.dot(x, wu_s[...], preferred_element_type=F32)
        o_ref[rows, :] = (jax.nn.silu(gate) * up).astype(o_ref.dtype)


def _ffn_in(xb, w_ffn_in, layer):
    tm, tn = FFN_TM, FFN_TN
    nb = FFN_HIDDEN // tn
    return pl.pallas_call(
        _ffn_in_kernel,
        grid=(nb, SEQ // tm),
        in_specs=[
            pl.BlockSpec((tm, D_MODEL), lambda n, m: (m, 0)),
            pl.BlockSpec((None, D_MODEL, tn), lambda n, m: (layer, 0, n)),
            pl.BlockSpec((None, D_MODEL, tn), lambda n, m: (layer, 0, nb + n)),
        ],
        out_specs=pl.BlockSpec((tm, tn), lambda n, m: (m, n)),
        out_shape=jax.ShapeDtypeStruct((SEQ, FFN_HIDDEN), BF16),
        scratch_shapes=[pltpu.VMEM((D_MODEL, tn), BF16), pltpu.VMEM((D_MODEL, tn), BF16)],
        compiler_params=_params("arbitrary", "arbitrary"),
        name="ffn_in",
    )(xb, w_ffn_in, w_ffn_in)


def _rope_tables():
    pos = jnp.arange(SEQ, dtype=F32)

    def cs(dim):
        inv = 1.0 / (ROPE_THETA ** (jnp.arange(0, dim, 2, dtype=F32) / dim))
        ang = pos[:, None] * inv[None, :]
        ang = jnp.concatenate([ang, ang], axis=-1)
        return jnp.cos(ang), jnp.sin(ang)

    cos_a, sin_a = cs(HEAD_DIM)
    half = HEAD_DIM // 2
    sin_a = jnp.concatenate([-sin_a[:, :half], sin_a[:, half:]], axis=-1)
    cos_b, sin_b = cs(B_QK_DIM)
    cos_b2 = jnp.concatenate([cos_b, cos_b], axis=-1)
    sin_b2 = jnp.concatenate([sin_b, sin_b], axis=-1)
    first_half = (np.arange(HEAD_DIM) % B_QK_DIM) < (B_QK_DIM // 2)
    sin_lo = jnp.where(first_half[None, :], -sin_b2, 0.0)
    sin_hi = jnp.where(first_half[None, :], 0.0, sin_b2)
    return cos_a, sin_a, cos_b2, sin_lo, sin_hi


def _mask_a():
    qc = (np.arange(A_TQ) // CHUNK)[:, None]
    kc = (np.arange(A_WINDOW + A_TQ) // CHUNK)[None, :]
    ok = (kc >= qc) & (kc <= qc + A_WINDOW // CHUNK)
    m = np.where(ok, 0.0, NEG_INF).astype(np.float32)
    return jnp.asarray(np.tile(m, (A_GROUP, 1)))


def _bias_c(rel_bias):
    nq, nk = C_TQ, C_KBLOCKS * C_TQ
    nt = nq + nk - 1
    lo = REL_CLIP - (nq - 1)
    rb = rel_bias.astype(F32)
    ramp = rb[..., lo:]
    flat_part = jnp.broadcast_to(rb[..., -1:], rb.shape[:-1] + (nt - ramp.shape[-1],))
    w = jnp.concatenate([ramp, flat_part], axis=-1)
    u = jnp.concatenate([w[..., ::-1], jnp.zeros(w.shape[:-1] + (1,), F32)], axis=-1)
    flat = jnp.tile(u, nq)[..., :nq * nt]
    skew = flat.reshape(flat.shape[:-1] + (nq, nt))
    bias = skew[..., nq - 1:]
    q = np.arange(nq)[:, None]
    k = np.arange(nk)[None, :]
    qc, kc = q // CHUNK, k // CHUNK
    ok = (kc >= qc) & (kc <= qc + C_PREV_CHUNKS)
    return jnp.where(ok, bias, NEG_INF)


def kernel(x, w_in, sinks, lambda_q1, lambda_k1, lambda_q2, lambda_k2, diff_norm_g, rel_bias,
           w_br_a, w_br_b, w_br_c, w_gate, b_gate, w_out, ln1_g, ln1_b,
           w_ffn_in, w_ffn_out, ln2_g, ln2_b):
    assert x.shape == (1, SEQ, D_MODEL)
    tabs = _rope_tables()
    mask_a = _mask_a()
    xf = x.reshape(SEQ, D_MODEL)
    xb = xf.astype(BF16)
    lam_vecs = jnp.stack([lambda_q1, lambda_k1, lambda_q2, lambda_k2], axis=1).astype(F32)
    sub_g = diff_norm_g.astype(F32).reshape(DEPTH, HEAD_DIM, 1)
    bias_c = _bias_c(rel_bias)
    b_gate3 = b_gate.reshape(DEPTH, 1, -1)
    vec3 = lambda v: v.reshape(DEPTH, 1, D_MODEL)
    ln1_g, ln1_b, ln2_g, ln2_b = vec3(ln1_g), vec3(ln1_b), vec3(ln2_g), vec3(ln2_b)
    w_out_b = w_out.astype(BF16)
    w_ffn_out_b = w_ffn_out.astype(BF16)
    for l in range(DEPTH):
        lam_init = 0.8 - 0.6 * math.exp(-0.3 * l)
        h = _in_proj(xb, w_in, l, tabs)
        ya = _attn_a(h, sinks, l, mask_a)
        yb = _attn_b(h, lam_vecs, sub_g, l, lam_init)
        yc = _attn_c(h, bias_c, l)
        mix = _mix(xb, ya, yb, yc, w_gate, b_gate3, w_br_a, w_br_b, w_br_c, l)
        xf, xb = _matmul_ln(mix, w_out_b, xf, ln1_g, ln1_b, l, OUT_TM, "out_ln")
        f = _ffn_in(xb, w_ffn_in, l)
        xf, xb = _matmul_ln(f, w_ffn_out_b, xf, ln2_g, ln2_b, l, FFN_OUT_TM, "ffn_out")
    return xf.reshape(1, SEQ, D_MODEL)
```

```python
import functools
import math

import jax
import jax.numpy as jnp
import numpy as np
from jax import lax
from jax.experimental import pallas as pl
from jax.experimental.pallas import tpu as pltpu

D_MODEL = 2048
SEQ = 8192
DEPTH = 4
CHUNK = 64
HEAD_DIM = 128
A_Q_HEADS = 8
A_KV_HEADS = 2
A_GROUP = A_Q_HEADS // A_KV_HEADS
B_HEADS = 4
B_QK_DIM = 64
C_HEADS = 4
C_PREV_CHUNKS = 8
REL_CLIP = 256
FFN_HIDDEN = 5632
IN_WIDTH = 4608
ROPE_THETA = 10000.0
LN_EPS = 1e-5
DEEPNORM_ALPHA = (2 * DEPTH) ** 0.25
NEG_INF = -1e30

BF16 = jnp.bfloat16
F32 = jnp.float32

VMEM_LIMIT_BYTES = 56 * 1024 * 1024

AQ_OFF, AK_OFF, AV_OFF = 0, 1024, 1280
BQ_OFF, BK_OFF, BV_OFF = 1536, 2048, 2560
CQ_OFF, CK_OFF, CV_OFF = 3072, 3584, 4096

LN_ROW_CHUNK = 128
ROW_CHUNK = 256
PROJ_TM, PROJ_TN = 1024, 1536
MIX_TM, MIX_TN = 512, 512
OUT_TM = 512
FFN_TM, FFN_TN = 1024, 512
FFN_OUT_TM = 256
A_TQ = 256
A_WINDOW = 128
B_TQ = 512
B_TK = 512
B_ONES_ROWS = 16
C_TQ = 256
C_KBLOCKS = (C_PREV_CHUNKS * CHUNK) // C_TQ + 1


def _params(*sem):
    return pltpu.CompilerParams(dimension_semantics=sem, vmem_limit_bytes=VMEM_LIMIT_BYTES)


def _rope_a(t, cos, sin_signed):
    return t * cos + pltpu.roll(t, HEAD_DIM // 2, 1) * sin_signed


def _rope_b(t, cos2, sin_lo, sin_hi):
    return t * cos2 + pltpu.roll(t, 96, 1) * sin_lo + pltpu.roll(t, 32, 1) * sin_hi


def _in_proj_kernel(x_ref, w_ref, cosa_ref, sina_ref, cosb_ref, sinb_lo_ref, sinb_hi_ref,
                    o_ref, wb_ref):
    n = pl.program_id(0)
    m = pl.program_id(1)

    @pl.when(m == 0)
    def _():
        wb_ref[...] = w_ref[...].astype(BF16)

    a_scale = HEAD_DIM ** -0.5
    b_scale = B_QK_DIM ** -0.5 * math.log2(math.e)

    def run(epilogue):
        for r in range(PROJ_TM // ROW_CHUNK):
            rows = slice(r * ROW_CHUNK, (r + 1) * ROW_CHUNK)
            acc = jnp.dot(x_ref[rows, :], wb_ref[...], preferred_element_type=F32)
            epilogue(acc, rows)

    def head_cols(acc, rows, lo, hi, fn, scale):
        for j in range(lo, hi):
            sl = slice(j * HEAD_DIM, (j + 1) * HEAD_DIM)
            r = fn(acc[:, sl])
            if scale != 1.0:
                r = r * scale
            o_ref[rows, sl] = r.astype(o_ref.dtype)

    def plain_cols(acc, rows, lo, scale=1.0):
        t = acc[:, lo:]
        if scale != 1.0:
            t = t * scale
        o_ref[rows, lo:] = t.astype(o_ref.dtype)

    def mixer_a(acc, rows):
        cos, sin = cosa_ref[rows, :], sina_ref[rows, :]
        rope = lambda t: _rope_a(t, cos, sin)
        head_cols(acc, rows, 0, A_Q_HEADS, rope, a_scale)
        head_cols(acc, rows, A_Q_HEADS, A_Q_HEADS + A_KV_HEADS, rope, 1.0)
        plain_cols(acc, rows, (A_Q_HEADS + A_KV_HEADS) * HEAD_DIM)

    def mixer_b(acc, rows):
        cos, lo_, hi_ = cosb_ref[rows, :], sinb_lo_ref[rows, :], sinb_hi_ref[rows, :]
        rope = lambda t: _rope_b(t, cos, lo_, hi_)
        head_cols(acc, rows, 0, B_HEADS, rope, b_scale)
        head_cols(acc, rows, B_HEADS, 2 * B_HEADS, rope, 1.0)
        plain_cols(acc, rows, 2 * B_HEADS * HEAD_DIM)

    def mixer_c(acc, rows):
        o_ref[rows, :C_HEADS * HEAD_DIM] = (acc[:, :C_HEADS * HEAD_DIM] * a_scale).astype(o_ref.dtype)
        plain_cols(acc, rows, C_HEADS * HEAD_DIM)

    pl.when(n == 0)(lambda: run(mixer_a))
    pl.when(n == 1)(lambda: run(mixer_b))
    pl.when(n == 2)(lambda: run(mixer_c))


def _in_proj(xb, w_in, layer, tabs):
    tm, tn = PROJ_TM, PROJ_TN
    rope_spec = pl.BlockSpec((tm, HEAD_DIM), lambda n, m: (m, 0))
    return pl.pallas_call(
        _in_proj_kernel,
        grid=(IN_WIDTH // tn, SEQ // tm),
        in_specs=[
            pl.BlockSpec((tm, D_MODEL), lambda n, m: (m, 0)),
            pl.BlockSpec((None, D_MODEL, tn), lambda n, m: (layer, 0, n)),
            rope_spec, rope_spec, rope_spec, rope_spec, rope_spec,
        ],
        out_specs=pl.BlockSpec((tm, tn), lambda n, m: (m, n)),
        out_shape=jax.ShapeDtypeStruct((SEQ, IN_WIDTH), BF16),
        scratch_shapes=[pltpu.VMEM((D_MODEL, tn), BF16)],
        compiler_params=_params("arbitrary", "arbitrary"),
        name="in_proj",
    )(xb, w_in, *tabs)


def _attn_a_kernel(sink_ref, q_ref, kp_ref, kc_ref, vp_ref, vc_ref, mask_ref, o_ref, *, layer):
    i = pl.program_id(0)
    has_prev = i > 0
    dn = (((1,), (1,)), ((), ()))
    hcols = lambda hd: slice(hd * HEAD_DIM, (hd + 1) * HEAD_DIM)
    scores = []
    for g in range(A_KV_HEADS):
        q = jnp.concatenate([q_ref[:, hcols(A_GROUP * g + j)] for j in range(A_GROUP)], axis=0)
        scores.append((lax.dot_general(q, kp_ref[:, hcols(g)], dn, preferred_element_type=F32),
                       lax.dot_general(q, kc_ref[:, hcols(g)], dn, preferred_element_type=F32)))
    for g in range(A_KV_HEADS):
        s_prev, s_cur = scores[g]
        s_prev = jnp.where(has_prev, s_prev + mask_ref[:, :A_WINDOW], NEG_INF)
        s_cur = s_cur + mask_ref[:, A_WINDOW:]
        sink = jnp.concatenate(
            [jnp.full((A_TQ, HEAD_DIM), sink_ref[layer, A_GROUP * g + j], F32) for j in range(A_GROUP)], axis=0)
        folded = functools.reduce(
            jnp.maximum, [s_prev] + [s_cur[:, c:c + HEAD_DIM] for c in range(0, A_TQ, HEAD_DIM)])
        mx = jnp.maximum(jnp.broadcast_to(folded.max(-1, keepdims=True), sink.shape), sink)
        p_prev = jnp.exp(s_prev - mx)
        p_cur = jnp.exp(s_cur - jnp.concatenate([mx] * (A_TQ // HEAD_DIM), axis=1))
        ones = lambda rows: jnp.ones((rows, HEAD_DIM), BF16)
        v_prev = jnp.concatenate([vp_ref[:, hcols(g)], ones(A_WINDOW)], axis=1)
        v_cur = jnp.concatenate([vc_ref[:, hcols(g)], ones(A_TQ)], axis=1)
        o_ext = (jnp.dot(p_prev.astype(BF16), v_prev, preferred_element_type=F32)
                 + jnp.dot(p_cur.astype(BF16), v_cur, preferred_element_type=F32))
        denom = o_ext[:, HEAD_DIM:] + jnp.exp(sink - mx)
        o = o_ext[:, :HEAD_DIM] * (1.0 / denom)
        for j in range(A_GROUP):
            o_ref[:, hcols(A_GROUP * g + j)] = o[j * A_TQ:(j + 1) * A_TQ].astype(o_ref.dtype)


def _attn_a(h, sinks, layer, mask_a):
    tq, win = A_TQ, A_WINDOW
    kv_width = A_KV_HEADS * HEAD_DIM
    kblk, vblk = AK_OFF // kv_width, AV_OFF // kv_width
    prev = lambda i: jnp.maximum(i * (tq // win) - 1, 0)
    return pl.pallas_call(
        functools.partial(_attn_a_kernel, layer=layer),
        grid=(SEQ // tq,),
        in_specs=[
            pl.BlockSpec(memory_space=pltpu.SMEM),
            pl.BlockSpec((tq, A_Q_HEADS * HEAD_DIM), lambda i: (i, 0)),
            pl.BlockSpec((win, kv_width), lambda i: (prev(i), kblk)),
            pl.BlockSpec((tq, kv_width), lambda i: (i, kblk)),
            pl.BlockSpec((win, kv_width), lambda i: (prev(i), vblk)),
            pl.BlockSpec((tq, kv_width), lambda i: (i, vblk)),
            pl.BlockSpec((A_GROUP * tq, win + tq), lambda i: (0, 0)),
        ],
        out_specs=pl.BlockSpec((tq, A_Q_HEADS * HEAD_DIM), lambda i: (i, 0)),
        out_shape=jax.ShapeDtypeStruct((SEQ, A_Q_HEADS * HEAD_DIM), BF16),
        compiler_params=_params("arbitrary"),
        name="attn_a",
    )(sinks, h, h, h, h, h, mask_a)


def _attn_b_kernel(lam_ref, g_ref, q_ref, k_ref, v_ref, o_ref, vt_s, qst_s, acc_s, s_s, m_s, *, lam_init):
    i = pl.program_id(0)
    tq, tk = B_TQ, B_TK
    hcols = lambda hd: slice(hd * HEAD_DIM, (hd + 1) * HEAD_DIM)

    @pl.when(i == 0)
    def _():
        ones = jnp.ones((B_ONES_ROWS, tk), BF16)
        for hd in range(B_HEADS):
            def transpose_block(c, carry, hd=hd):
                start = pl.multiple_of(c * tk, tk)
                blk = v_ref[pl.ds(start, tk), hcols(hd)].astype(F32)
                vt_s[hd, c, :HEAD_DIM, :] = blk.T.astype(BF16)
                vt_s[hd, c, HEAD_DIM:, :] = ones
                return carry
            lax.fori_loop(0, SEQ // tk, transpose_block, 0)

    feat = lax.broadcasted_iota(jnp.int32, (HEAD_DIM, tq), 0)
    for hd in range(B_HEADS):
        qt = q_ref[:, hcols(hd)].astype(F32).T
        qst_s[hd, :, :tq] = jnp.where(feat < B_QK_DIM, qt, 0.0).astype(BF16)
        qst_s[hd, :, tq:] = jnp.where(feat >= B_QK_DIM, qt, 0.0).astype(BF16)
    acc_s[...] = jnp.zeros_like(acc_s)

    def scores(j, hd):
        kj = k_ref[pl.ds(pl.multiple_of(j * tk, tk), tk), hcols(hd)]
        return jnp.dot(kj, qst_s[hd], preferred_element_type=F32)

    def block(j, src, dst, diagonal=False):
        for hd in range(B_HEADS):
            if dst is not None:
                s_s[dst, hd] = scores(j + 1, hd)
            s = s_s[src, hd]
            if diagonal:
                kc = lax.broadcasted_iota(jnp.int32, s.shape, 0) // CHUNK
                qc = (lax.broadcasted_iota(jnp.int32, s.shape, 1) % tq) // CHUNK
                s = jnp.where(kc <= qc, s, NEG_INF)
            m_old = m_s[hd]
            m_new = jnp.maximum(m_old, s.max(axis=0, keepdims=True))
            m_s[hd] = m_new
            a = jnp.exp2(m_old - m_new)
            p = jnp.exp2(s - m_new).astype(BF16)
            acc_s[hd] = a * acc_s[hd] + jnp.dot(vt_s[hd, j], p, preferred_element_type=F32)

    assert tq == tk
    for hd in range(B_HEADS):
        s_s[0, hd] = scores(0, hd)
    m_s[...] = jnp.full(m_s.shape, NEG_INF, F32)

    def pair(t, carry):
        block(2 * t, 0, 1)
        block(2 * t + 1, 1, 0)
        return carry

    lax.fori_loop(0, i // 2, pair, 0)

    @pl.when(i % 2 == 0)
    def _():
        block(i, 0, None, diagonal=True)

    @pl.when(i % 2 == 1)
    def _():
        block(i - 1, 0, 1)
        block(i, 1, None, diagonal=True)

    lam = (jnp.exp(jnp.sum(lam_ref[0:1, :] * lam_ref[1:2, :], axis=-1, keepdims=True))
           - jnp.exp(jnp.sum(lam_ref[2:3, :] * lam_ref[3:4, :], axis=-1, keepdims=True))
           + lam_init)
    for hd in range(B_HEADS):
        acc = acc_s[hd]
        o_all = acc[:HEAD_DIM] * (1.0 / acc[HEAD_DIM:HEAD_DIM + 1])
        o = o_all[:, :tq] - lam * o_all[:, tq:]
        o = o * lax.rsqrt(jnp.mean(jnp.square(o), axis=0, keepdims=True) + LN_EPS)
        o = o * g_ref[...] * (1.0 - lam_init)
        o_ref[:, hcols(hd)] = o.T.astype(o_ref.dtype)


def _attn_b(h, lam_vecs, sub_g, layer, lam_init):
    tq, tk = B_TQ, B_TK
    width = B_HEADS * HEAD_DIM
    resident = lambda col: pl.BlockSpec((SEQ, width), lambda i: (0, col), pipeline_mode=pl.Buffered(1))
    return pl.pallas_call(
        functools.partial(_attn_b_kernel, lam_init=lam_init),
        grid=(SEQ // tq,),
        in_specs=[
            pl.BlockSpec((None, 4, B_QK_DIM), lambda i: (layer, 0, 0)),
            pl.BlockSpec((None, HEAD_DIM, 1), lambda i: (layer, 0, 0)),
            pl.BlockSpec((tq, width), lambda i: (i, BQ_OFF // width)),
            resident(BK_OFF // width),
            resident(BV_OFF // width),
        ],
        out_specs=pl.BlockSpec((tq, width), lambda i: (i, 0)),
        out_shape=jax.ShapeDtypeStruct((SEQ, width), BF16),
        scratch_shapes=[
            pltpu.VMEM((B_HEADS, SEQ // tk, HEAD_DIM + B_ONES_ROWS, tk), BF16),
            pltpu.VMEM((B_HEADS, HEAD_DIM, 2 * tq), BF16),
            pltpu.VMEM((B_HEADS, HEAD_DIM + B_ONES_ROWS, 2 * tq), F32),
            pltpu.VMEM((2, B_HEADS, tk, 2 * tq), F32),
            pltpu.VMEM((B_HEADS, 1, 2 * tq), F32),
        ],
        compiler_params=_params("arbitrary"),
        name="attn_b",
    )(lam_vecs, sub_g, h, h, h)


def _attn_c_kernel(q_ref, *refs):
    k_refs = refs[:C_KBLOCKS]
    v_refs = refs[C_KBLOCKS:2 * C_KBLOCKS]
    bias_ref = refs[2 * C_KBLOCKS]
    o_ref = refs[2 * C_KBLOCKS + 1]
    i = pl.program_id(0)
    dn = (((1,), (1,)), ((), ()))
    hcols = lambda hd: slice(hd * HEAD_DIM, (hd + 1) * HEAD_DIM)
    scores = [[lax.dot_general(q_ref[:, hcols(hd)], k_refs[b][:, hcols(hd)], dn, preferred_element_type=F32)
               for b in range(C_KBLOCKS)] for hd in range(C_HEADS)]
    for hd in range(C_HEADS):
        hsl = hcols(hd)
        s = []
        for b in range(C_KBLOCKS):
            sb = scores[hd][b] + bias_ref[hd, :, b * C_TQ:(b + 1) * C_TQ]
            s.append(jnp.where(i - (C_KBLOCKS - 1) + b >= 0, sb, NEG_INF))
        mx = functools.reduce(jnp.maximum, [sb.max(-1, keepdims=True) for sb in s])
        p = [jnp.exp(sb - mx) for sb in s]
        ones = jnp.ones((C_TQ, HEAD_DIM), BF16)
        o_ext = functools.reduce(
            lambda a, b_: a + b_,
            [jnp.dot(p[b].astype(BF16), jnp.concatenate([v_refs[b][:, hsl], ones], axis=1),
                     preferred_element_type=F32) for b in range(C_KBLOCKS)])
        o_ref[:, hsl] = (o_ext[:, :HEAD_DIM] * (1.0 / o_ext[:, HEAD_DIM:])).astype(o_ref.dtype)


def _attn_c(h, bias_c, layer):
    tq = C_TQ
    qb, kb, vb = CQ_OFF // 512, CK_OFF // 512, CV_OFF // 512

    def kv_spec(b, col):
        return pl.BlockSpec((tq, 512), lambda i: (jnp.maximum(i - (C_KBLOCKS - 1) + b, 0), col))

    return pl.pallas_call(
        _attn_c_kernel,
        grid=(SEQ // tq,),
        in_specs=([pl.BlockSpec((tq, 512), lambda i: (i, qb))]
                  + [kv_spec(b, kb) for b in range(C_KBLOCKS)]
                  + [kv_spec(b, vb) for b in range(C_KBLOCKS)]
                  + [pl.BlockSpec((None, C_HEADS, tq, C_KBLOCKS * tq), lambda i: (layer, 0, 0, 0))]),
        out_specs=pl.BlockSpec((tq, 512), lambda i: (i, 0)),
        out_shape=jax.ShapeDtypeStruct((SEQ, C_HEADS * HEAD_DIM), BF16),
        compiler_params=_params("arbitrary"),
        name="attn_c",
    )(h, *([h] * (2 * C_KBLOCKS)), bias_c)


def _mix_kernel(x_ref, ya_ref, yb_ref, yc_ref, wga_ref, wgb_ref, wgc_ref, ba_ref, bb_ref, bc_ref,
                wa_ref, wb_ref, wc_ref, o_ref, wg_s, wa_s, wb_s, wc_s):
    m = pl.program_id(1)

    @pl.when(m == 0)
    def _():
        wg_s[0] = wga_ref[...].astype(BF16)
        wg_s[1] = wgb_ref[...].astype(BF16)
        wg_s[2] = wgc_ref[...].astype(BF16)
        wa_s[...] = wa_ref[...].astype(BF16)
        wb_s[...] = wb_ref[...].astype(BF16)
        wc_s[...] = wc_ref[...].astype(BF16)

    for r in range(MIX_TM // ROW_CHUNK):
        rows = slice(r * ROW_CHUNK, (r + 1) * ROW_CHUNK)
        x = x_ref[rows, :]

        def branch(idx, b_ref, y_ref, w_s):
            gate = jax.nn.sigmoid(jnp.dot(x, wg_s[idx], preferred_element_type=F32) + b_ref[...])
            return gate * jnp.dot(y_ref[rows, :], w_s[...], preferred_element_type=F32)

        mix = branch(0, ba_ref, ya_ref, wa_s) + branch(1, bb_ref, yb_ref, wb_s) + branch(2, bc_ref, yc_ref, wc_s)
        o_ref[rows, :] = mix.astype(o_ref.dtype)


def _mix(xb, ya, yb, yc, w_gate, b_gate, w_br_a, w_br_b, w_br_c, layer):
    tm, tn = MIX_TM, MIX_TN
    nb = D_MODEL // tn
    row = lambda width: pl.BlockSpec((tm, width), lambda n, m: (m, 0))
    gate_w = lambda k: pl.BlockSpec((None, D_MODEL, tn), lambda n, m: (layer, 0, k * nb + n))
    gate_b = lambda k: pl.BlockSpec((None, 1, tn), lambda n, m: (layer, 0, k * nb + n))
    br_w = lambda width: pl.BlockSpec((None, width, tn), lambda n, m: (layer, 0, n))
    return pl.pallas_call(
        _mix_kernel,
        grid=(nb, SEQ // tm),
        in_specs=[row(D_MODEL), row(1024), row(512), row(512),
                  gate_w(0), gate_w(1), gate_w(2), gate_b(0), gate_b(1), gate_b(2),
                  br_w(1024), br_w(512), br_w(512)],
        out_specs=pl.BlockSpec((tm, tn), lambda n, m: (m, n)),
        out_shape=jax.ShapeDtypeStruct((SEQ, D_MODEL), BF16),
        scratch_shapes=[pltpu.VMEM((3, D_MODEL, tn), BF16), pltpu.VMEM((1024, tn), BF16),
                        pltpu.VMEM((512, tn), BF16), pltpu.VMEM((512, tn), BF16)],
        compiler_params=_params("arbitrary", "arbitrary"),
        name="mix",
    )(xb, ya, yb, yc, w_gate, w_gate, w_gate, b_gate, b_gate, b_gate, w_br_a, w_br_b, w_br_c)


def _matmul_ln_kernel(y_ref, w_ref, x_ref, g_ref, b_ref, of_ref, ob_ref, *wb_scratch, chunk):
    if wb_scratch:
        wb_ref, = wb_scratch

        @pl.when(pl.program_id(0) == 0)
        def _():
            wb_ref[...] = w_ref[...].astype(BF16)
    else:
        wb_ref = w_ref
    for r in range(y_ref.shape[0] // chunk):
        rows = slice(r * chunk, (r + 1) * chunk)
        z = DEEPNORM_ALPHA * x_ref[rows, :] + jnp.dot(y_ref[rows, :], wb_ref[...], preferred_element_type=F32)
        mu = jnp.mean(z, axis=-1, keepdims=True)
        zc = z - mu
        var = jnp.mean(jnp.square(zc), axis=-1, keepdims=True)
        out = zc * lax.rsqrt(var + LN_EPS) * g_ref[...] + b_ref[...]
        of_ref[rows, :] = out
        ob_ref[rows, :] = out.astype(ob_ref.dtype)


def _matmul_ln(y, w, x, g, b, layer, tm, name):
    k = y.shape[1]
    vec = pl.BlockSpec((None, 1, D_MODEL), lambda m: (layer, 0, 0))
    if w.ndim == 3:
        w_spec = pl.BlockSpec((None, k, D_MODEL), lambda m: (layer, 0, 0), pipeline_mode=pl.Buffered(1))
        scratch = [pltpu.VMEM((k, D_MODEL), BF16)]
    else:
        w_spec = pl.BlockSpec((k, D_MODEL), lambda m: (0, 0), pipeline_mode=pl.Buffered(1))
        scratch = []
    return pl.pallas_call(
        functools.partial(_matmul_ln_kernel, chunk=LN_ROW_CHUNK),
        grid=(SEQ // tm,),
        in_specs=[
            pl.BlockSpec((tm, k), lambda m: (m, 0)),
            w_spec,
            pl.BlockSpec((tm, D_MODEL), lambda m: (m, 0)),
            vec, vec,
        ],
        out_specs=[pl.BlockSpec((tm, D_MODEL), lambda m: (m, 0)),
                   pl.BlockSpec((tm, D_MODEL), lambda m: (m, 0))],
        out_shape=[jax.ShapeDtypeStruct((SEQ, D_MODEL), F32),
                   jax.ShapeDtypeStruct((SEQ, D_MODEL), BF16)],
        scratch_shapes=scratch,
        compiler_params=_params("arbitrary"),
        name=name,
    )(y, w, x, g, b)


def _ffn_in_kernel(x_ref, wg_ref, wu_ref, wo_ref, o_ref, wo_b_ref, wg_s, wu_s):
    m = pl.program_id(1)

    @pl.when(m == 0)
    def _():
        wg_s[...] = wg_ref[...].astype(BF16)
        wu_s[...] = wu_ref[...].astype(BF16)
        wo_b_ref[...] = wo_ref[...].astype(BF16)

    for r in range(FFN_TM // ROW_CHUNK):
        rows = slice(r * ROW_CHUNK, (r + 1) * ROW_CHUNK)
        x = x_ref[rows, :]
        gate = jnp.dot(x, wg_s[...], preferred_element_type=F32)
        up = jnp.dot(x, wu_s[...], preferred_element_type=F32)
        o_ref[rows, :] = (jax.nn.silu(gate) * up).astype(o_ref.dtype)


def _ffn_in(xb, w_ffn_in, w_ffn_out, layer):
    tm, tn = FFN_TM, FFN_TN
    nb = FFN_HIDDEN // tn
    return pl.pallas_call(
        _ffn_in_kernel,
        grid=(nb, SEQ // tm),
        in_specs=[
            pl.BlockSpec((tm, D_MODEL), lambda n, m: (m, 0)),
            pl.BlockSpec((None, D_MODEL, tn), lambda n, m: (layer, 0, n)),
            pl.BlockSpec((None, D_MODEL, tn), lambda n, m: (layer, 0, nb + n)),
            pl.BlockSpec((None, tn, D_MODEL), lambda n, m: (layer, n, 0)),
        ],
        out_specs=[pl.BlockSpec((tm, tn), lambda n, m: (m, n)),
                   pl.BlockSpec((tn, D_MODEL), lambda n, m: (n, 0))],
        out_shape=[jax.ShapeDtypeStruct((SEQ, FFN_HIDDEN), BF16),
                   jax.ShapeDtypeStruct((FFN_HIDDEN, D_MODEL), BF16)],
        scratch_shapes=[pltpu.VMEM((D_MODEL, tn), BF16), pltpu.VMEM((D_MODEL, tn), BF16)],
        compiler_params=_params("arbitrary", "arbitrary"),
        name="ffn_in",
    )(xb, w_ffn_in, w_ffn_in, w_ffn_out)


def _rope_tables():
    pos = jnp.arange(SEQ, dtype=F32)

    def cs(dim):
        inv = 1.0 / (ROPE_THETA ** (jnp.arange(0, dim, 2, dtype=F32) / dim))
        ang = pos[:, None] * inv[None, :]
        ang = jnp.concatenate([ang, ang], axis=-1)
        return jnp.cos(ang), jnp.sin(ang)

    cos_a, sin_a = cs(HEAD_DIM)
    half = HEAD_DIM // 2
    sin_a = jnp.concatenate([-sin_a[:, :half], sin_a[:, half:]], axis=-1)
    cos_b, sin_b = cs(B_QK_DIM)
    cos_b2 = jnp.concatenate([cos_b, cos_b], axis=-1)
    sin_b2 = jnp.concatenate([sin_b, sin_b], axis=-1)
    first_half = (np.arange(HEAD_DIM) % B_QK_DIM) < (B_QK_DIM // 2)
    sin_lo = jnp.where(first_half[None, :], -sin_b2, 0.0)
    sin_hi = jnp.where(first_half[None, :], 0.0, sin_b2)
    return cos_a, sin_a, cos_b2, sin_lo, sin_hi


def _mask_a():
    qc = (np.arange(A_TQ) // CHUNK)[:, None]
    kc = (np.arange(A_WINDOW + A_TQ) // CHUNK)[None, :]
    ok = (kc >= qc) & (kc <= qc + A_WINDOW // CHUNK)
    m = np.where(ok, 0.0, NEG_INF).astype(np.float32)
    return jnp.asarray(np.tile(m, (A_GROUP, 1)))


def _bias_c(rel_bias):
    nq, nk = C_TQ, C_KBLOCKS * C_TQ
    nt = nq + nk - 1
    lo = REL_CLIP - (nq - 1)
    rb = rel_bias.astype(F32)
    ramp = rb[..., lo:]
    flat_part = jnp.broadcast_to(rb[..., -1:], rb.shape[:-1] + (nt - ramp.shape[-1],))
    w = jnp.concatenate([ramp, flat_part], axis=-1)
    u = jnp.concatenate([w[..., ::-1], jnp.zeros(w.shape[:-1] + (1,), F32)], axis=-1)
    flat = jnp.tile(u, nq)[..., :nq * nt]
    skew = flat.reshape(flat.shape[:-1] + (nq, nt))
    bias = skew[..., nq - 1:]
    q = np.arange(nq)[:, None]
    k = np.arange(nk)[None, :]
    qc, kc = q // CHUNK, k // CHUNK
    ok = (kc >= qc) & (kc <= qc + C_PREV_CHUNKS)
    return jnp.where(ok, bias, NEG_INF)


def kernel(x, w_in, sinks, lambda_q1, lambda_k1, lambda_q2, lambda_k2, diff_norm_g, rel_bias,
           w_br_a, w_br_b, w_br_c, w_gate, b_gate, w_out, ln1_g, ln1_b,
           w_ffn_in, w_ffn_out, ln2_g, ln2_b):
    assert x.shape == (1, SEQ, D_MODEL)
    tabs = _rope_tables()
    mask_a = _mask_a()
    xf = x.reshape(SEQ, D_MODEL)
    xb = xf.astype(BF16)
    lam_vecs = jnp.stack([lambda_q1, lambda_k1, lambda_q2, lambda_k2], axis=1).astype(F32)
    sub_g = diff_norm_g.astype(F32).reshape(DEPTH, HEAD_DIM, 1)
    bias_c = _bias_c(rel_bias)
    b_gate3 = b_gate.reshape(DEPTH, 1, -1)
    vec3 = lambda v: v.reshape(DEPTH, 1, D_MODEL)
    ln1_g, ln1_b, ln2_g, ln2_b = vec3(ln1_g), vec3(ln1_b), vec3(ln2_g), vec3(ln2_b)
    for l in range(DEPTH):
        lam_init = 0.8 - 0.6 * math.exp(-0.3 * l)
        h = _in_proj(xb, w_in, l, tabs)
        ya = _attn_a(h, sinks, l, mask_a)
        yb = _attn_b(h, lam_vecs, sub_g, l, lam_init)
        yc = _attn_c(h, bias_c, l)
        mix = _mix(xb, ya, yb, yc, w_gate, b_gate3, w_br_a, w_br_b, w_br_c, l)
        xf, xb = _matmul_ln(mix, w_out, xf, ln1_g, ln1_b, l, OUT_TM, "out_ln")
        f, w_ffn_out_b = _ffn_in(xb, w_ffn_in, w_ffn_out, l)
        xf, xb = _matmul_ln(f, w_ffn_out_b, xf, ln2_g, ln2_b, l, FFN_OUT_TM, "ffn_out")
    return xf.reshape(1, SEQ, D_MODEL)
```

```python
import functools
import math

import jax
import jax.numpy as jnp
import numpy as np
from jax import lax
from jax.experimental import pallas as pl
from jax.experimental.pallas import tpu as pltpu

D_MODEL = 2048
SEQ = 8192
DEPTH = 4
CHUNK = 64
HEAD_DIM = 128
A_Q_HEADS = 8
A_KV_HEADS = 2
A_GROUP = A_Q_HEADS // A_KV_HEADS
B_HEADS = 4
B_QK_DIM = 64
C_HEADS = 4
C_PREV_CHUNKS = 8
REL_CLIP = 256
FFN_HIDDEN = 5632
IN_WIDTH = 4608
ROPE_THETA = 10000.0
LN_EPS = 1e-5
DEEPNORM_ALPHA = (2 * DEPTH) ** 0.25
NEG_INF = -1e30

BF16 = jnp.bfloat16
F32 = jnp.float32

VMEM_LIMIT_BYTES = 56 * 1024 * 1024

AQ_OFF, AK_OFF, AV_OFF = 0, 1024, 1280
BQ_OFF, BK_OFF, BV_OFF = 1536, 2048, 2560
CQ_OFF, CK_OFF, CV_OFF = 3072, 3584, 4096

LN_ROW_CHUNK = 128
ROW_CHUNK = 256
PROJ_TM, PROJ_TN = 1024, 1536
MIX_TM, MIX_TN = 512, 512
OUT_TM = 512
FFN_TM, FFN_TN = 1024, 512
FFN_WO_PHASE = 4
FFN_OUT_TM = 256
A_TQ = 256
A_WINDOW = 128
B_TQ = 512
B_TK = 512
B_ONES_ROWS = 16
C_TQ = 256
C_KBLOCKS = (C_PREV_CHUNKS * CHUNK) // C_TQ + 1


def _params(*sem):
    return pltpu.CompilerParams(dimension_semantics=sem, vmem_limit_bytes=VMEM_LIMIT_BYTES)


def _rope_a(t, cos, sin_signed):
    return t * cos + pltpu.roll(t, HEAD_DIM // 2, 1) * sin_signed


def _rope_b(t, cos2, sin_lo, sin_hi):
    return t * cos2 + pltpu.roll(t, 96, 1) * sin_lo + pltpu.roll(t, 32, 1) * sin_hi


def _in_proj_kernel(x_ref, w_ref, cosa_ref, sina_ref, cosb_ref, sinb_lo_ref, sinb_hi_ref,
                    o_ref, wb_ref):
    n = pl.program_id(0)
    m = pl.program_id(1)

    @pl.when(m == 0)
    def _():
        wb_ref[...] = w_ref[...].astype(BF16)

    a_scale = HEAD_DIM ** -0.5
    b_scale = B_QK_DIM ** -0.5 * math.log2(math.e)

    def run(epilogue):
        for r in range(PROJ_TM // ROW_CHUNK):
            rows = slice(r * ROW_CHUNK, (r + 1) * ROW_CHUNK)
            acc = jnp.dot(x_ref[rows, :], wb_ref[...], preferred_element_type=F32)
            epilogue(acc, rows)

    def head_cols(acc, rows, lo, hi, fn, scale):
        for j in range(lo, hi):
            sl = slice(j * HEAD_DIM, (j + 1) * HEAD_DIM)
            r = fn(acc[:, sl])
            if scale != 1.0:
                r = r * scale
            o_ref[rows, sl] = r.astype(o_ref.dtype)

    def plain_cols(acc, rows, lo, scale=1.0):
        t = acc[:, lo:]
        if scale != 1.0:
            t = t * scale
        o_ref[rows, lo:] = t.astype(o_ref.dtype)

    def mixer_a(acc, rows):
        cos, sin = cosa_ref[rows, :], sina_ref[rows, :]
        rope = lambda t: _rope_a(t, cos, sin)
        head_cols(acc, rows, 0, A_Q_HEADS, rope, a_scale)
        head_cols(acc, rows, A_Q_HEADS, A_Q_HEADS + A_KV_HEADS, rope, 1.0)
        plain_cols(acc, rows, (A_Q_HEADS + A_KV_HEADS) * HEAD_DIM)

    def mixer_b(acc, rows):
        cos, lo_, hi_ = cosb_ref[rows, :], sinb_lo_ref[rows, :], sinb_hi_ref[rows, :]
        rope = lambda t: _rope_b(t, cos, lo_, hi_)
        head_cols(acc, rows, 0, B_HEADS, rope, b_scale)
        head_cols(acc, rows, B_HEADS, 2 * B_HEADS, rope, 1.0)
        plain_cols(acc, rows, 2 * B_HEADS * HEAD_DIM)

    def mixer_c(acc, rows):
        o_ref[rows, :C_HEADS * HEAD_DIM] = (acc[:, :C_HEADS * HEAD_DIM] * a_scale).astype(o_ref.dtype)
        plain_cols(acc, rows, C_HEADS * HEAD_DIM)

    pl.when(n == 0)(lambda: run(mixer_a))
    pl.when(n == 1)(lambda: run(mixer_b))
    pl.when(n == 2)(lambda: run(mixer_c))


def _in_proj(xb, w_in, layer, tabs):
    tm, tn = PROJ_TM, PROJ_TN
    rope_spec = pl.BlockSpec((tm, HEAD_DIM), lambda n, m: (m, 0))
    return pl.pallas_call(
        _in_proj_kernel,
        grid=(IN_WIDTH // tn, SEQ // tm),
        in_specs=[
            pl.BlockSpec((tm, D_MODEL), lambda n, m: (m, 0)),
            pl.BlockSpec((None, D_MODEL, tn), lambda n, m: (layer, 0, n)),
            rope_spec, rope_spec, rope_spec, rope_spec, rope_spec,
        ],
        out_specs=pl.BlockSpec((tm, tn), lambda n, m: (m, n)),
        out_shape=jax.ShapeDtypeStruct((SEQ, IN_WIDTH), BF16),
        scratch_shapes=[pltpu.VMEM((D_MODEL, tn), BF16)],
        compiler_params=_params("arbitrary", "arbitrary"),
        name="in_proj",
    )(xb, w_in, *tabs)


def _attn_a_kernel(sink_ref, q_ref, kp_ref, kc_ref, vp_ref, vc_ref, mask_ref, o_ref, *, layer):
    i = pl.program_id(0)
    has_prev = i > 0
    dn = (((1,), (1,)), ((), ()))
    hcols = lambda hd: slice(hd * HEAD_DIM, (hd + 1) * HEAD_DIM)
    scores = []
    for g in range(A_KV_HEADS):
        q = jnp.concatenate([q_ref[:, hcols(A_GROUP * g + j)] for j in range(A_GROUP)], axis=0)
        scores.append((lax.dot_general(q, kp_ref[:, hcols(g)], dn, preferred_element_type=F32),
                       lax.dot_general(q, kc_ref[:, hcols(g)], dn, preferred_element_type=F32)))
    for g in range(A_KV_HEADS):
        s_prev, s_cur = scores[g]
        s_prev = jnp.where(has_prev, s_prev + mask_ref[:, :A_WINDOW], NEG_INF)
        s_cur = s_cur + mask_ref[:, A_WINDOW:]
        sink = jnp.concatenate(
            [jnp.full((A_TQ, HEAD_DIM), sink_ref[layer, A_GROUP * g + j], F32) for j in range(A_GROUP)], axis=0)
        folded = functools.reduce(
            jnp.maximum, [s_prev] + [s_cur[:, c:c + HEAD_DIM] for c in range(0, A_TQ, HEAD_DIM)])
        mx = jnp.maximum(jnp.broadcast_to(folded.max(-1, keepdims=True), sink.shape), sink)
        p_prev = jnp.exp(s_prev - mx)
        p_cur = jnp.exp(s_cur - jnp.concatenate([mx] * (A_TQ // HEAD_DIM), axis=1))
        ones = lambda rows: jnp.ones((rows, HEAD_DIM), BF16)
        v_prev = jnp.concatenate([vp_ref[:, hcols(g)], ones(A_WINDOW)], axis=1)
        v_cur = jnp.concatenate([vc_ref[:, hcols(g)], ones(A_TQ)], axis=1)
        o_ext = (jnp.dot(p_prev.astype(BF16), v_prev, preferred_element_type=F32)
                 + jnp.dot(p_cur.astype(BF16), v_cur, preferred_element_type=F32))
        denom = o_ext[:, HEAD_DIM:] + jnp.exp(sink - mx)
        o = o_ext[:, :HEAD_DIM] * (1.0 / denom)
        for j in range(A_GROUP):
            o_ref[:, hcols(A_GROUP * g + j)] = o[j * A_TQ:(j + 1) * A_TQ].astype(o_ref.dtype)


def _attn_a(h, sinks, layer, mask_a):
    tq, win = A_TQ, A_WINDOW
    kv_width = A_KV_HEADS * HEAD_DIM
    kblk, vblk = AK_OFF // kv_width, AV_OFF // kv_width
    prev = lambda i: jnp.maximum(i * (tq // win) - 1, 0)
    return pl.pallas_call(
        functools.partial(_attn_a_kernel, layer=layer),
        grid=(SEQ // tq,),
        in_specs=[
            pl.BlockSpec(memory_space=pltpu.SMEM),
            pl.BlockSpec((tq, A_Q_HEADS * HEAD_DIM), lambda i: (i, 0)),
            pl.BlockSpec((win, kv_width), lambda i: (prev(i), kblk)),
            pl.BlockSpec((tq, kv_width), lambda i: (i, kblk)),
            pl.BlockSpec((win, kv_width), lambda i: (prev(i), vblk)),
            pl.BlockSpec((tq, kv_width), lambda i: (i, vblk)),
            pl.BlockSpec((A_GROUP * tq, win + tq), lambda i: (0, 0)),
        ],
        out_specs=pl.BlockSpec((tq, A_Q_HEADS * HEAD_DIM), lambda i: (i, 0)),
        out_shape=jax.ShapeDtypeStruct((SEQ, A_Q_HEADS * HEAD_DIM), BF16),
        compiler_params=_params("arbitrary"),
        name="attn_a",
    )(sinks, h, h, h, h, h, mask_a)


def _attn_b_kernel(lam_ref, g_ref, q_ref, k_ref, v_ref, o_ref, vt_s, qst_s, acc_s, s_s, m_s, *, lam_init):
    i = pl.program_id(0)
    tq, tk = B_TQ, B_TK
    hcols = lambda hd: slice(hd * HEAD_DIM, (hd + 1) * HEAD_DIM)

    @pl.when(i == 0)
    def _():
        ones = jnp.ones((B_ONES_ROWS, tk), BF16)
        for hd in range(B_HEADS):
            def transpose_block(c, carry, hd=hd):
                start = pl.multiple_of(c * tk, tk)
                blk = v_ref[pl.ds(start, tk), hcols(hd)].astype(F32)
                vt_s[hd, c, :HEAD_DIM, :] = blk.T.astype(BF16)
                vt_s[hd, c, HEAD_DIM:, :] = ones
                return carry
            lax.fori_loop(0, SEQ // tk, transpose_block, 0)

    feat = lax.broadcasted_iota(jnp.int32, (HEAD_DIM, tq), 0)
    for hd in range(B_HEADS):
        qt = q_ref[:, hcols(hd)].astype(F32).T
        qst_s[hd, :, :tq] = jnp.where(feat < B_QK_DIM, qt, 0.0).astype(BF16)
        qst_s[hd, :, tq:] = jnp.where(feat >= B_QK_DIM, qt, 0.0).astype(BF16)
    acc_s[...] = jnp.zeros_like(acc_s)

    def scores(j, hd):
        kj = k_ref[pl.ds(pl.multiple_of(j * tk, tk), tk), hcols(hd)]
        return jnp.dot(kj, qst_s[hd], preferred_element_type=F32)

    def block(j, src, dst, diagonal=False):
        for hd in range(B_HEADS):
            if dst is not None:
                s_s[dst, hd] = scores(j + 1, hd)
            s = s_s[src, hd]
            if diagonal:
                kc = lax.broadcasted_iota(jnp.int32, s.shape, 0) // CHUNK
                qc = (lax.broadcasted_iota(jnp.int32, s.shape, 1) % tq) // CHUNK
                s = jnp.where(kc <= qc, s, NEG_INF)
            m_old = m_s[hd]
            m_new = jnp.maximum(m_old, s.max(axis=0, keepdims=True))
            m_s[hd] = m_new
            a = jnp.exp2(m_old - m_new)
            p = jnp.exp2(s - m_new).astype(BF16)
            acc_s[hd] = a * acc_s[hd] + jnp.dot(vt_s[hd, j], p, preferred_element_type=F32)

    assert tq == tk
    for hd in range(B_HEADS):
        s_s[0, hd] = scores(0, hd)
    m_s[...] = jnp.full(m_s.shape, NEG_INF, F32)

    def pair(t, carry):
        block(2 * t, 0, 1)
        block(2 * t + 1, 1, 0)
        return carry

    lax.fori_loop(0, i // 2, pair, 0)

    @pl.when(i % 2 == 0)
    def _():
        block(i, 0, None, diagonal=True)

    @pl.when(i % 2 == 1)
    def _():
        block(i - 1, 0, 1)
        block(i, 1, None, diagonal=True)

    lam = (jnp.exp(jnp.sum(lam_ref[0:1, :] * lam_ref[1:2, :], axis=-1, keepdims=True))
           - jnp.exp(jnp.sum(lam_ref[2:3, :] * lam_ref[3:4, :], axis=-1, keepdims=True))
           + lam_init)
    for hd in range(B_HEADS):
        acc = acc_s[hd]
        o_all = acc[:HEAD_DIM] * (1.0 / acc[HEAD_DIM:HEAD_DIM + 1])
        o = o_all[:, :tq] - lam * o_all[:, tq:]
        o = o * lax.rsqrt(jnp.mean(jnp.square(o), axis=0, keepdims=True) + LN_EPS)
        o = o * g_ref[...] * (1.0 - lam_init)
        o_ref[:, hcols(hd)] = o.T.astype(o_ref.dtype)


def _attn_b(h, lam_vecs, sub_g, layer, lam_init):
    tq, tk = B_TQ, B_TK
    width = B_HEADS * HEAD_DIM
    resident = lambda col: pl.BlockSpec((SEQ, width), lambda i: (0, col), pipeline_mode=pl.Buffered(1))
    return pl.pallas_call(
        functools.partial(_attn_b_kernel, lam_init=lam_init),
        grid=(SEQ // tq,),
        in_specs=[
            pl.BlockSpec((None, 4, B_QK_DIM), lambda i: (layer, 0, 0)),
            pl.BlockSpec((None, HEAD_DIM, 1), lambda i: (layer, 0, 0)),
            pl.BlockSpec((tq, width), lambda i: (i, BQ_OFF // width)),
            resident(BK_OFF // width),
            resident(BV_OFF // width),
        ],
        out_specs=pl.BlockSpec((tq, width), lambda i: (i, 0)),
        out_shape=jax.ShapeDtypeStruct((SEQ, width), BF16),
        scratch_shapes=[
            pltpu.VMEM((B_HEADS, SEQ // tk, HEAD_DIM + B_ONES_ROWS, tk), BF16),
            pltpu.VMEM((B_HEADS, HEAD_DIM, 2 * tq), BF16),
            pltpu.VMEM((B_HEADS, HEAD_DIM + B_ONES_ROWS, 2 * tq), F32),
            pltpu.VMEM((2, B_HEADS, tk, 2 * tq), F32),
            pltpu.VMEM((B_HEADS, 1, 2 * tq), F32),
        ],
        compiler_params=_params("arbitrary"),
        name="attn_b",
    )(lam_vecs, sub_g, h, h, h)


def _attn_c_kernel(q_ref, *refs):
    k_refs = refs[:C_KBLOCKS]
    v_refs = refs[C_KBLOCKS:2 * C_KBLOCKS]
    diag_ref, o_ref, bias_ref = refs[2 * C_KBLOCKS:]
    i = pl.program_id(0)
    dn = (((1,), (1,)), ((), ()))
    hcols = lambda hd: slice(hd * HEAD_DIM, (hd + 1) * HEAD_DIM)
    nk = C_KBLOCKS * C_TQ

    @pl.when(i == 0)
    def _():
        qc = lax.broadcasted_iota(jnp.int32, (C_TQ, nk), 0) // CHUNK
        kc = lax.broadcasted_iota(jnp.int32, (C_TQ, nk), 1) // CHUNK
        band = (kc >= qc) & (kc <= qc + C_PREV_CHUNKS)
        for hd in range(C_HEADS):
            rows = jnp.broadcast_to(diag_ref[hd:hd + 1, :], (C_TQ, C_TQ + nk))
            skew = pltpu.roll(rows, 0, 1, stride=1, stride_axis=0)
            bias_ref[hd] = jnp.where(band, skew[:, :nk], NEG_INF)
    scores = [[lax.dot_general(q_ref[:, hcols(hd)], k_refs[b][:, hcols(hd)], dn, preferred_element_type=F32)
               for b in range(C_KBLOCKS)] for hd in range(C_HEADS)]
    for hd in range(C_HEADS):
        hsl = hcols(hd)
        s = []
        for b in range(C_KBLOCKS):
            sb = scores[hd][b] + bias_ref[hd, :, b * C_TQ:(b + 1) * C_TQ]
            s.append(jnp.where(i - (C_KBLOCKS - 1) + b >= 0, sb, NEG_INF))
        mx = functools.reduce(jnp.maximum, [sb.max(-1, keepdims=True) for sb in s])
        p = [jnp.exp(sb - mx) for sb in s]
        ones = jnp.ones((C_TQ, HEAD_DIM), BF16)
        o_ext = functools.reduce(
            lambda a, b_: a + b_,
            [jnp.dot(p[b].astype(BF16), jnp.concatenate([v_refs[b][:, hsl], ones], axis=1),
                     preferred_element_type=F32) for b in range(C_KBLOCKS)])
        o_ref[:, hsl] = (o_ext[:, :HEAD_DIM] * (1.0 / o_ext[:, HEAD_DIM:])).astype(o_ref.dtype)


def _attn_c(h, bias_diag, layer):
    tq = C_TQ
    qb, kb, vb = CQ_OFF // 512, CK_OFF // 512, CV_OFF // 512

    def kv_spec(b, col):
        return pl.BlockSpec((tq, 512), lambda i: (jnp.maximum(i - (C_KBLOCKS - 1) + b, 0), col))

    return pl.pallas_call(
        _attn_c_kernel,
        grid=(SEQ // tq,),
        in_specs=([pl.BlockSpec((tq, 512), lambda i: (i, qb))]
                  + [kv_spec(b, kb) for b in range(C_KBLOCKS)]
                  + [kv_spec(b, vb) for b in range(C_KBLOCKS)]
                  + [pl.BlockSpec((None, C_HEADS, (C_KBLOCKS + 1) * tq), lambda i: (layer, 0, 0))]),
        out_specs=pl.BlockSpec((tq, 512), lambda i: (i, 0)),
        out_shape=jax.ShapeDtypeStruct((SEQ, C_HEADS * HEAD_DIM), BF16),
        scratch_shapes=[pltpu.VMEM((C_HEADS, tq, C_KBLOCKS * tq), F32)],
        compiler_params=_params("arbitrary"),
        name="attn_c",
    )(h, *([h] * (2 * C_KBLOCKS)), bias_diag)


def _mix_kernel(x_ref, ya_ref, yb_ref, yc_ref, wga_ref, wgb_ref, wgc_ref, ba_ref, bb_ref, bc_ref,
                wa_ref, wb_ref, wc_ref, o_ref, wg_s, wa_s, wb_s, wc_s):
    m = pl.program_id(1)

    @pl.when(m == 0)
    def _():
        wg_s[0] = wga_ref[...].astype(BF16)
        wg_s[1] = wgb_ref[...].astype(BF16)
        wg_s[2] = wgc_ref[...].astype(BF16)
        wa_s[...] = wa_ref[...].astype(BF16)
        wb_s[...] = wb_ref[...].astype(BF16)
        wc_s[...] = wc_ref[...].astype(BF16)

    for r in range(MIX_TM // ROW_CHUNK):
        rows = slice(r * ROW_CHUNK, (r + 1) * ROW_CHUNK)
        x = x_ref[rows, :]

        def branch(idx, b_ref, y_ref, w_s):
            gate = jax.nn.sigmoid(jnp.dot(x, wg_s[idx], preferred_element_type=F32) + b_ref[...])
            return gate * jnp.dot(y_ref[rows, :], w_s[...], preferred_element_type=F32)

        mix = branch(0, ba_ref, ya_ref, wa_s) + branch(1, bb_ref, yb_ref, wb_s) + branch(2, bc_ref, yc_ref, wc_s)
        o_ref[rows, :] = mix.astype(o_ref.dtype)


def _mix(xb, ya, yb, yc, w_gate, b_gate, w_br_a, w_br_b, w_br_c, layer):
    tm, tn = MIX_TM, MIX_TN
    nb = D_MODEL // tn
    row = lambda width: pl.BlockSpec((tm, width), lambda n, m: (m, 0))
    gate_w = lambda k: pl.BlockSpec((None, D_MODEL, tn), lambda n, m: (layer, 0, k * nb + n))
    gate_b = lambda k: pl.BlockSpec((None, 1, tn), lambda n, m: (layer, 0, k * nb + n))
    br_w = lambda width: pl.BlockSpec((None, width, tn), lambda n, m: (layer, 0, n))
    return pl.pallas_call(
        _mix_kernel,
        grid=(nb, SEQ // tm),
        in_specs=[row(D_MODEL), row(1024), row(512), row(512),
                  gate_w(0), gate_w(1), gate_w(2), gate_b(0), gate_b(1), gate_b(2),
                  br_w(1024), br_w(512), br_w(512)],
        out_specs=pl.BlockSpec((tm, tn), lambda n, m: (m, n)),
        out_shape=jax.ShapeDtypeStruct((SEQ, D_MODEL), BF16),
        scratch_shapes=[pltpu.VMEM((3, D_MODEL, tn), BF16), pltpu.VMEM((1024, tn), BF16),
                        pltpu.VMEM((512, tn), BF16), pltpu.VMEM((512, tn), BF16)],
        compiler_params=_params("arbitrary", "arbitrary"),
        name="mix",
    )(xb, ya, yb, yc, w_gate, w_gate, w_gate, b_gate, b_gate, b_gate, w_br_a, w_br_b, w_br_c)


def _matmul_ln_kernel(y_ref, w_ref, x_ref, g_ref, b_ref, of_ref, ob_ref, *wb_scratch, chunk):
    if wb_scratch:
        wb_ref, = wb_scratch

        @pl.when(pl.program_id(0) == 0)
        def _():
            wb_ref[...] = w_ref[...].astype(BF16)
    else:
        wb_ref = w_ref
    for r in range(y_ref.shape[0] // chunk):
        rows = slice(r * chunk, (r + 1) * chunk)
        z = DEEPNORM_ALPHA * x_ref[rows, :] + jnp.dot(y_ref[rows, :], wb_ref[...], preferred_element_type=F32)
        mu = jnp.mean(z, axis=-1, keepdims=True)
        zc = z - mu
        var = jnp.mean(jnp.square(zc), axis=-1, keepdims=True)
        out = zc * lax.rsqrt(var + LN_EPS) * g_ref[...] + b_ref[...]
        of_ref[rows, :] = out
        ob_ref[rows, :] = out.astype(ob_ref.dtype)


def _matmul_ln(y, w, x, g, b, layer, tm, name):
    k = y.shape[1]
    vec = pl.BlockSpec((None, 1, D_MODEL), lambda m: (layer, 0, 0))
    if w.ndim == 3:
        w_spec = pl.BlockSpec((None, k, D_MODEL), lambda m: (layer, 0, 0), pipeline_mode=pl.Buffered(1))
        scratch = [pltpu.VMEM((k, D_MODEL), BF16)]
    else:
        w_spec = pl.BlockSpec((k, D_MODEL), lambda m: (0, 0), pipeline_mode=pl.Buffered(1))
        scratch = []
    return pl.pallas_call(
        functools.partial(_matmul_ln_kernel, chunk=LN_ROW_CHUNK),
        grid=(SEQ // tm,),
        in_specs=[
            pl.BlockSpec((tm, k), lambda m: (m, 0)),
            w_spec,
            pl.BlockSpec((tm, D_MODEL), lambda m: (m, 0)),
            vec, vec,
        ],
        out_specs=[pl.BlockSpec((tm, D_MODEL), lambda m: (m, 0)),
                   pl.BlockSpec((tm, D_MODEL), lambda m: (m, 0))],
        out_shape=[jax.ShapeDtypeStruct((SEQ, D_MODEL), F32),
                   jax.ShapeDtypeStruct((SEQ, D_MODEL), BF16)],
        scratch_shapes=scratch,
        compiler_params=_params("arbitrary"),
        name=name,
    )(y, w, x, g, b)


def _ffn_in_kernel(x_ref, wg_ref, wu_ref, wo_ref, o_ref, wo_b_ref, wg_s, wu_s):
    m = pl.program_id(1)

    @pl.when(m == 0)
    def _():
        wg_s[...] = wg_ref[...].astype(BF16)
        wu_s[...] = wu_ref[...].astype(BF16)

    @pl.when((m == FFN_WO_PHASE) | ((pl.program_id(0) == 0) & (m == 0)))
    def _():
        wo_b_ref[...] = wo_ref[...].astype(BF16)

    for r in range(FFN_TM // ROW_CHUNK):
        rows = slice(r * ROW_CHUNK, (r + 1) * ROW_CHUNK)
        x = x_ref[rows, :]
        gate = jnp.dot(x, wg_s[...], preferred_element_type=F32)
        up = jnp.dot(x, wu_s[...], preferred_element_type=F32)
        o_ref[rows, :] = (jax.nn.silu(gate) * up).astype(o_ref.dtype)


def _ffn_in(xb, w_ffn_in, w_ffn_out, layer):
    tm, tn = FFN_TM, FFN_TN
    nb = FFN_HIDDEN // tn

    def _wo_block(n, m):
        return jnp.minimum(n + (m >= FFN_WO_PHASE).astype(jnp.int32), nb - 1)

    return pl.pallas_call(
        _ffn_in_kernel,
        grid=(nb, SEQ // tm),
        in_specs=[
            pl.BlockSpec((tm, D_MODEL), lambda n, m: (m, 0)),
            pl.BlockSpec((None, D_MODEL, tn), lambda n, m: (layer, 0, n)),
            pl.BlockSpec((None, D_MODEL, tn), lambda n, m: (layer, 0, nb + n)),
            pl.BlockSpec((None, tn, D_MODEL), lambda n, m: (layer, _wo_block(n, m), 0)),
        ],
        out_specs=[pl.BlockSpec((tm, tn), lambda n, m: (m, n)),
                   pl.BlockSpec((tn, D_MODEL), lambda n, m: (_wo_block(n, m), 0))],
        out_shape=[jax.ShapeDtypeStruct((SEQ, FFN_HIDDEN), BF16),
                   jax.ShapeDtypeStruct((FFN_HIDDEN, D_MODEL), BF16)],
        scratch_shapes=[pltpu.VMEM((D_MODEL, tn), BF16), pltpu.VMEM((D_MODEL, tn), BF16)],
        compiler_params=_params("arbitrary", "arbitrary"),
        name="ffn_in",
    )(xb, w_ffn_in, w_ffn_in, w_ffn_out)


def _rope_tables():
    pos = jnp.arange(SEQ, dtype=F32)

    def cs(dim):
        inv = 1.0 / (ROPE_THETA ** (jnp.arange(0, dim, 2, dtype=F32) / dim))
        ang = pos[:, None] * inv[None, :]
        ang = jnp.concatenate([ang, ang], axis=-1)
        return jnp.cos(ang), jnp.sin(ang)

    cos_a, sin_a = cs(HEAD_DIM)
    half = HEAD_DIM // 2
    sin_a = jnp.concatenate([-sin_a[:, :half], sin_a[:, half:]], axis=-1)
    cos_b, sin_b = cs(B_QK_DIM)
    cos_b2 = jnp.concatenate([cos_b, cos_b], axis=-1)
    sin_b2 = jnp.concatenate([sin_b, sin_b], axis=-1)
    first_half = (np.arange(HEAD_DIM) % B_QK_DIM) < (B_QK_DIM // 2)
    sin_lo = jnp.where(first_half[None, :], -sin_b2, 0.0)
    sin_hi = jnp.where(first_half[None, :], 0.0, sin_b2)
    return cos_a, sin_a, cos_b2, sin_lo, sin_hi


def _mask_a():
    qc = (np.arange(A_TQ) // CHUNK)[:, None]
    kc = (np.arange(A_WINDOW + A_TQ) // CHUNK)[None, :]
    ok = (kc >= qc) & (kc <= qc + A_WINDOW // CHUNK)
    m = np.where(ok, 0.0, NEG_INF).astype(np.float32)
    return jnp.asarray(np.tile(m, (A_GROUP, 1)))


def _bias_diagonals(rel_bias):
    nq, nk = C_TQ, C_KBLOCKS * C_TQ
    rb = rel_bias.astype(F32)
    n_clip = (nk - nq) - REL_CLIP
    far = jnp.broadcast_to(rb[..., -1:], rb.shape[:-1] + (n_clip,))
    near = rb[..., REL_CLIP - (nq - 1):][..., ::-1]
    d_nonneg = jnp.concatenate([far, near], axis=-1)
    assert d_nonneg.shape[-1] == nk
    d_neg = jnp.broadcast_to(rb[..., -1:], rb.shape[:-1] + (nq,))
    return jnp.concatenate([d_nonneg, d_neg], axis=-1)


def kernel(x, w_in, sinks, lambda_q1, lambda_k1, lambda_q2, lambda_k2, diff_norm_g, rel_bias,
           w_br_a, w_br_b, w_br_c, w_gate, b_gate, w_out, ln1_g, ln1_b,
           w_ffn_in, w_ffn_out, ln2_g, ln2_b):
    assert x.shape == (1, SEQ, D_MODEL)
    tabs = _rope_tables()
    mask_a = _mask_a()
    xf = x.reshape(SEQ, D_MODEL)
    xb = xf.astype(BF16)
    lam_vecs = jnp.stack([lambda_q1, lambda_k1, lambda_q2, lambda_k2], axis=1).astype(F32)
    sub_g = diff_norm_g.astype(F32).reshape(DEPTH, HEAD_DIM, 1)
    bias_diag = _bias_diagonals(rel_bias)
    b_gate3 = b_gate.reshape(DEPTH, 1, -1)
    vec3 = lambda v: v.reshape(DEPTH, 1, D_MODEL)
    ln1_g, ln1_b, ln2_g, ln2_b = vec3(ln1_g), vec3(ln1_b), vec3(ln2_g), vec3(ln2_b)
    for l in range(DEPTH):
        lam_init = 0.8 - 0.6 * math.exp(-0.3 * l)
        h = _in_proj(xb, w_in, l, tabs)
        ya = _attn_a(h, sinks, l, mask_a)
        yb = _attn_b(h, lam_vecs, sub_g, l, lam_init)
        yc = _attn_c(h, bias_diag, l)
        mix = _mix(xb, ya, yb, yc, w_gate, b_gate3, w_br_a, w_br_b, w_br_c, l)
        xf, xb = _matmul_ln(mix, w_out, xf, ln1_g, ln1_b, l, OUT_TM, "out_ln")
        f, w_ffn_out_b = _ffn_in(xb, w_ffn_in, w_ffn_out, l)
        xf, xb = _matmul_ln(f, w_ffn_out_b, xf, ln2_g, ln2_b, l, FFN_OUT_TM, "ffn_out")
    return xf.reshape(1, SEQ, D_MODEL)
```

```python
import functools
import math

import jax
import jax.numpy as jnp
import numpy as np
from jax import lax
from jax.experimental import pallas as pl
from jax.experimental.pallas import tpu as pltpu

D_MODEL = 2048
SEQ = 8192
DEPTH = 4
CHUNK = 64
HEAD_DIM = 128
A_Q_HEADS = 8
A_KV_HEADS = 2
A_GROUP = A_Q_HEADS // A_KV_HEADS
B_HEADS = 4
B_QK_DIM = 64
C_HEADS = 4
C_PREV_CHUNKS = 8
REL_CLIP = 256
FFN_HIDDEN = 5632
IN_WIDTH = 4608
ROPE_THETA = 10000.0
LN_EPS = 1e-5
DEEPNORM_ALPHA = (2 * DEPTH) ** 0.25
NEG_INF = -1e30

BF16 = jnp.bfloat16
F32 = jnp.float32

VMEM_LIMIT_BYTES = 56 * 1024 * 1024

AQ_OFF, AK_OFF, AV_OFF = 0, 1024, 1280
BQ_OFF, BK_OFF, BV_OFF = 1536, 2048, 2560
CQ_OFF, CK_OFF, CV_OFF = 3072, 3584, 4096

LN_ROW_CHUNK = 128
ROW_CHUNK = 256
PROJ_TM, PROJ_TN = 1024, 1536
MIX_TM, MIX_TN = 512, 512
OUT_TM = 512
FFN_TM, FFN_TN = 1024, 512
FFN_WO_PHASE = 4
FFN_OUT_TM = 256
A_TQ = 256
A_WINDOW = 128
B_TQ = 512
B_TK = 512
B_ONES_ROWS = 16
C_TQ = 256
C_KBLOCKS = (C_PREV_CHUNKS * CHUNK) // C_TQ + 1


def _params(*sem):
    return pltpu.CompilerParams(dimension_semantics=sem, vmem_limit_bytes=VMEM_LIMIT_BYTES)


def _rope_a(t, cos, sin_signed):
    return t * cos + pltpu.roll(t, HEAD_DIM // 2, 1) * sin_signed


def _rope_b(t, cos2, sin_lo, sin_hi):
    return t * cos2 + pltpu.roll(t, 96, 1) * sin_lo + pltpu.roll(t, 32, 1) * sin_hi


def _in_proj_kernel(x_ref, w_ref, cosa_ref, sina_ref, cosb_ref, sinb_lo_ref, sinb_hi_ref,
                    o_ref, wb_ref):
    n = pl.program_id(0)
    m = pl.program_id(1)

    @pl.when(m == 0)
    def _():
        wb_ref[...] = w_ref[...].astype(BF16)

    a_scale = HEAD_DIM ** -0.5
    b_scale = B_QK_DIM ** -0.5 * math.log2(math.e)

    def run(epilogue):
        for r in range(PROJ_TM // ROW_CHUNK):
            rows = slice(r * ROW_CHUNK, (r + 1) * ROW_CHUNK)
            acc = jnp.dot(x_ref[rows, :], wb_ref[...], preferred_element_type=F32)
            epilogue(acc, rows)

    def head_cols(acc, rows, lo, hi, fn, scale):
        for j in range(lo, hi):
            sl = slice(j * HEAD_DIM, (j + 1) * HEAD_DIM)
            r = fn(acc[:, sl])
            if scale != 1.0:
                r = r * scale
            o_ref[rows, sl] = r.astype(o_ref.dtype)

    def plain_cols(acc, rows, lo, scale=1.0):
        t = acc[:, lo:]
        if scale != 1.0:
            t = t * scale
        o_ref[rows, lo:] = t.astype(o_ref.dtype)

    def mixer_a(acc, rows):
        cos, sin = cosa_ref[rows, :], sina_ref[rows, :]
        rope = lambda t: _rope_a(t, cos, sin)
        head_cols(acc, rows, 0, A_Q_HEADS, rope, a_scale)
        head_cols(acc, rows, A_Q_HEADS, A_Q_HEADS + A_KV_HEADS, rope, 1.0)
        plain_cols(acc, rows, (A_Q_HEADS + A_KV_HEADS) * HEAD_DIM)

    def mixer_b(acc, rows):
        cos, lo_, hi_ = cosb_ref[rows, :], sinb_lo_ref[rows, :], sinb_hi_ref[rows, :]
        rope = lambda t: _rope_b(t, cos, lo_, hi_)
        head_cols(acc, rows, 0, B_HEADS, rope, b_scale)
        head_cols(acc, rows, B_HEADS, 2 * B_HEADS, rope, 1.0)
        plain_cols(acc, rows, 2 * B_HEADS * HEAD_DIM)

    def mixer_c(acc, rows):
        o_ref[rows, :C_HEADS * HEAD_DIM] = (acc[:, :C_HEADS * HEAD_DIM] * a_scale).astype(o_ref.dtype)
        plain_cols(acc, rows, C_HEADS * HEAD_DIM)

    pl.when(n == 0)(lambda: run(mixer_a))
    pl.when(n == 1)(lambda: run(mixer_b))
    pl.when(n == 2)(lambda: run(mixer_c))


def _in_proj(xb, w_in, layer, tabs):
    tm, tn = PROJ_TM, PROJ_TN
    rope_spec = pl.BlockSpec((tm, HEAD_DIM), lambda n, m: (m, 0))
    return pl.pallas_call(
        _in_proj_kernel,
        grid=(IN_WIDTH // tn, SEQ // tm),
        in_specs=[
            pl.BlockSpec((tm, D_MODEL), lambda n, m: (m, 0)),
            pl.BlockSpec((None, D_MODEL, tn), lambda n, m: (layer, 0, n)),
            rope_spec, rope_spec, rope_spec, rope_spec, rope_spec,
        ],
        out_specs=pl.BlockSpec((tm, tn), lambda n, m: (m, n)),
        out_shape=jax.ShapeDtypeStruct((SEQ, IN_WIDTH), BF16),
        scratch_shapes=[pltpu.VMEM((D_MODEL, tn), BF16)],
        compiler_params=_params("arbitrary", "arbitrary"),
        name="in_proj",
    )(xb, w_in, *tabs)


def _attn_a_stages(sink_ref, q_ref, kp_ref, kc_ref, vp_ref, vc_ref, mask_ref, o_ref, layer):
    i = pl.program_id(0)
    has_prev = i > 0
    dn = (((1,), (1,)), ((), ()))
    hcols = lambda hd: slice(hd * HEAD_DIM, (hd + 1) * HEAD_DIM)
    scores = []
    for g in range(A_KV_HEADS):
        q = jnp.concatenate([q_ref[:, hcols(A_GROUP * g + j)] for j in range(A_GROUP)], axis=0)
        scores.append((lax.dot_general(q, kp_ref[:, hcols(g)], dn, preferred_element_type=F32),
                       lax.dot_general(q, kc_ref[:, hcols(g)], dn, preferred_element_type=F32)))

    def stage(g):
        s_prev, s_cur = scores[g]
        s_prev = jnp.where(has_prev, s_prev + mask_ref[:, :A_WINDOW], NEG_INF)
        s_cur = s_cur + mask_ref[:, A_WINDOW:]
        sink = jnp.concatenate(
            [jnp.full((A_TQ, HEAD_DIM), sink_ref[layer, A_GROUP * g + j], F32) for j in range(A_GROUP)], axis=0)
        folded = functools.reduce(
            jnp.maximum, [s_prev] + [s_cur[:, c:c + HEAD_DIM] for c in range(0, A_TQ, HEAD_DIM)])
        mx = jnp.maximum(jnp.broadcast_to(folded.max(-1, keepdims=True), sink.shape), sink)
        p_prev = jnp.exp(s_prev - mx)
        p_cur = jnp.exp(s_cur - jnp.concatenate([mx] * (A_TQ // HEAD_DIM), axis=1))
        ones = lambda rows: jnp.ones((rows, HEAD_DIM), BF16)
        v_prev = jnp.concatenate([vp_ref[:, hcols(g)], ones(A_WINDOW)], axis=1)
        v_cur = jnp.concatenate([vc_ref[:, hcols(g)], ones(A_TQ)], axis=1)
        o_ext = (jnp.dot(p_prev.astype(BF16), v_prev, preferred_element_type=F32)
                 + jnp.dot(p_cur.astype(BF16), v_cur, preferred_element_type=F32))
        denom = o_ext[:, HEAD_DIM:] + jnp.exp(sink - mx)
        o = o_ext[:, :HEAD_DIM] * (1.0 / denom)
        for j in range(A_GROUP):
            o_ref[:, hcols(A_GROUP * g + j)] = o[j * A_TQ:(j + 1) * A_TQ].astype(o_ref.dtype)

    return [functools.partial(stage, g) for g in range(A_KV_HEADS)]


def _attn_a_specs():
    tq, win = A_TQ, A_WINDOW
    kv_width = A_KV_HEADS * HEAD_DIM
    kblk, vblk = AK_OFF // kv_width, AV_OFF // kv_width
    prev = lambda i: jnp.maximum(i * (tq // win) - 1, 0)
    in_specs = [
        pl.BlockSpec(memory_space=pltpu.SMEM),
        pl.BlockSpec((tq, A_Q_HEADS * HEAD_DIM), lambda i: (i, 0)),
        pl.BlockSpec((win, kv_width), lambda i: (prev(i), kblk)),
        pl.BlockSpec((tq, kv_width), lambda i: (i, kblk)),
        pl.BlockSpec((win, kv_width), lambda i: (prev(i), vblk)),
        pl.BlockSpec((tq, kv_width), lambda i: (i, vblk)),
        pl.BlockSpec((A_GROUP * tq, win + tq), lambda i: (0, 0)),
    ]
    out_spec = pl.BlockSpec((tq, A_Q_HEADS * HEAD_DIM), lambda i: (i, 0))
    return in_specs, out_spec, jax.ShapeDtypeStruct((SEQ, A_Q_HEADS * HEAD_DIM), BF16)


def _attn_b_kernel(lam_ref, g_ref, q_ref, k_ref, v_ref, o_ref, vt_s, qst_s, acc_s, s_s, m_s, *, lam_init):
    i = pl.program_id(0)
    tq, tk = B_TQ, B_TK
    hcols = lambda hd: slice(hd * HEAD_DIM, (hd + 1) * HEAD_DIM)

    @pl.when(i == 0)
    def _():
        ones = jnp.ones((B_ONES_ROWS, tk), BF16)
        for hd in range(B_HEADS):
            def transpose_block(c, carry, hd=hd):
                start = pl.multiple_of(c * tk, tk)
                blk = v_ref[pl.ds(start, tk), hcols(hd)].astype(F32)
                vt_s[hd, c, :HEAD_DIM, :] = blk.T.astype(BF16)
                vt_s[hd, c, HEAD_DIM:, :] = ones
                return carry
            lax.fori_loop(0, SEQ // tk, transpose_block, 0)

    feat = lax.broadcasted_iota(jnp.int32, (HEAD_DIM, tq), 0)
    for hd in range(B_HEADS):
        qt = q_ref[:, hcols(hd)].astype(F32).T
        qst_s[hd, :, :tq] = jnp.where(feat < B_QK_DIM, qt, 0.0).astype(BF16)
        qst_s[hd, :, tq:] = jnp.where(feat >= B_QK_DIM, qt, 0.0).astype(BF16)
    acc_s[...] = jnp.zeros_like(acc_s)

    def scores(j, hd):
        kj = k_ref[pl.ds(pl.multiple_of(j * tk, tk), tk), hcols(hd)]
        return jnp.dot(kj, qst_s[hd], preferred_element_type=F32)

    def block(j, src, dst, diagonal=False):
        for hd in range(B_HEADS):
            if dst is not None:
                s_s[dst, hd] = scores(j + 1, hd)
            s = s_s[src, hd]
            if diagonal:
                kc = lax.broadcasted_iota(jnp.int32, s.shape, 0) // CHUNK
                qc = (lax.broadcasted_iota(jnp.int32, s.shape, 1) % tq) // CHUNK
                s = jnp.where(kc <= qc, s, NEG_INF)
            m_old = m_s[hd]
            m_new = jnp.maximum(m_old, s.max(axis=0, keepdims=True))
            m_s[hd] = m_new
            a = jnp.exp2(m_old - m_new)
            p = jnp.exp2(s - m_new).astype(BF16)
            acc_s[hd] = a * acc_s[hd] + jnp.dot(vt_s[hd, j], p, preferred_element_type=F32)

    assert tq == tk
    for hd in range(B_HEADS):
        s_s[0, hd] = scores(0, hd)
    m_s[...] = jnp.full(m_s.shape, NEG_INF, F32)

    def pair(t, carry):
        block(2 * t, 0, 1)
        block(2 * t + 1, 1, 0)
        return carry

    lax.fori_loop(0, i // 2, pair, 0)

    @pl.when(i % 2 == 0)
    def _():
        block(i, 0, None, diagonal=True)

    @pl.when(i % 2 == 1)
    def _():
        block(i - 1, 0, 1)
        block(i, 1, None, diagonal=True)

    lam = (jnp.exp(jnp.sum(lam_ref[0:1, :] * lam_ref[1:2, :], axis=-1, keepdims=True))
           - jnp.exp(jnp.sum(lam_ref[2:3, :] * lam_ref[3:4, :], axis=-1, keepdims=True))
           + lam_init)
    for hd in range(B_HEADS):
        acc = acc_s[hd]
        o_all = acc[:HEAD_DIM] * (1.0 / acc[HEAD_DIM:HEAD_DIM + 1])
        o = o_all[:, :tq] - lam * o_all[:, tq:]
        o = o * lax.rsqrt(jnp.mean(jnp.square(o), axis=0, keepdims=True) + LN_EPS)
        o = o * g_ref[...] * (1.0 - lam_init)
        o_ref[:, hcols(hd)] = o.T.astype(o_ref.dtype)


def _attn_b(h, lam_vecs, sub_g, layer, lam_init):
    tq, tk = B_TQ, B_TK
    width = B_HEADS * HEAD_DIM
    resident = lambda col: pl.BlockSpec((SEQ, width), lambda i: (0, col), pipeline_mode=pl.Buffered(1))
    return pl.pallas_call(
        functools.partial(_attn_b_kernel, lam_init=lam_init),
        grid=(SEQ // tq,),
        in_specs=[
            pl.BlockSpec((None, 4, B_QK_DIM), lambda i: (layer, 0, 0)),
            pl.BlockSpec((None, HEAD_DIM, 1), lambda i: (layer, 0, 0)),
            pl.BlockSpec((tq, width), lambda i: (i, BQ_OFF // width)),
            resident(BK_OFF // width),
            resident(BV_OFF // width),
        ],
        out_specs=pl.BlockSpec((tq, width), lambda i: (i, 0)),
        out_shape=jax.ShapeDtypeStruct((SEQ, width), BF16),
        scratch_shapes=[
            pltpu.VMEM((B_HEADS, SEQ // tk, HEAD_DIM + B_ONES_ROWS, tk), BF16),
            pltpu.VMEM((B_HEADS, HEAD_DIM, 2 * tq), BF16),
            pltpu.VMEM((B_HEADS, HEAD_DIM + B_ONES_ROWS, 2 * tq), F32),
            pltpu.VMEM((2, B_HEADS, tk, 2 * tq), F32),
            pltpu.VMEM((B_HEADS, 1, 2 * tq), F32),
        ],
        compiler_params=_params("arbitrary"),
        name="attn_b",
    )(lam_vecs, sub_g, h, h, h)


def _attn_c_build_bias(diag_ref, bias_ref):
    nk = C_KBLOCKS * C_TQ
    qc = lax.broadcasted_iota(jnp.int32, (C_TQ, nk), 0) // CHUNK
    kc = lax.broadcasted_iota(jnp.int32, (C_TQ, nk), 1) // CHUNK
    band = (kc >= qc) & (kc <= qc + C_PREV_CHUNKS)
    for hd in range(C_HEADS):
        rows = jnp.broadcast_to(diag_ref[hd:hd + 1, :], (C_TQ, C_TQ + nk))
        skew = pltpu.roll(rows, 0, 1, stride=1, stride_axis=0)
        bias_ref[hd] = jnp.where(band, skew[:, :nk], NEG_INF)


def _attn_c_stages(q_ref, k_refs, v_refs, o_ref, bias_ref):
    i = pl.program_id(0)
    dn = (((1,), (1,)), ((), ()))
    hcols = lambda hd: slice(hd * HEAD_DIM, (hd + 1) * HEAD_DIM)
    scores = [[lax.dot_general(q_ref[:, hcols(hd)], k_refs[b][:, hcols(hd)], dn, preferred_element_type=F32)
               for b in range(C_KBLOCKS)] for hd in range(C_HEADS)]

    def stage(hd):
        hsl = hcols(hd)
        s = []
        for b in range(C_KBLOCKS):
            sb = scores[hd][b] + bias_ref[hd, :, b * C_TQ:(b + 1) * C_TQ]
            s.append(jnp.where(i - (C_KBLOCKS - 1) + b >= 0, sb, NEG_INF))
        mx = functools.reduce(jnp.maximum, [sb.max(-1, keepdims=True) for sb in s])
        p = [jnp.exp(sb - mx) for sb in s]
        ones = jnp.ones((C_TQ, HEAD_DIM), BF16)
        o_ext = functools.reduce(
            lambda a, b_: a + b_,
            [jnp.dot(p[b].astype(BF16), jnp.concatenate([v_refs[b][:, hsl], ones], axis=1),
                     preferred_element_type=F32) for b in range(C_KBLOCKS)])
        o_ref[:, hsl] = (o_ext[:, :HEAD_DIM] * (1.0 / o_ext[:, HEAD_DIM:])).astype(o_ref.dtype)

    return [functools.partial(stage, hd) for hd in range(C_HEADS)]


def _attn_ac_kernel(*refs, layer):
    n_a = 7
    a_in, c_in = refs[:n_a], refs[n_a:n_a + 2 + 2 * C_KBLOCKS]
    oa_ref, oc_ref, bias_ref = refs[n_a + 2 + 2 * C_KBLOCKS:]
    pl.when(pl.program_id(0) == 0)(lambda: _attn_c_build_bias(c_in[-1], bias_ref))
    a_stages = _attn_a_stages(*a_in, oa_ref, layer)
    c_stages = _attn_c_stages(c_in[0], c_in[1:1 + C_KBLOCKS], c_in[1 + C_KBLOCKS:1 + 2 * C_KBLOCKS],
                              oc_ref, bias_ref)
    per_a = len(c_stages) // len(a_stages)
    for g, a_stage in enumerate(a_stages):
        a_stage()
        for c_stage in c_stages[g * per_a:(g + 1) * per_a]:
            c_stage()


def _attn_ac(h, sinks, mask_a, bias_diag, layer):
    assert A_TQ == C_TQ
    tq = C_TQ
    c_width = C_HEADS * HEAD_DIM
    qb, kb, vb = CQ_OFF // c_width, CK_OFF // c_width, CV_OFF // c_width

    def kv_spec(b, col):
        return pl.BlockSpec((tq, c_width), lambda i: (jnp.maximum(i - (C_KBLOCKS - 1) + b, 0), col))

    a_in_specs, a_out_spec, a_out_shape = _attn_a_specs()
    c_in_specs = ([pl.BlockSpec((tq, c_width), lambda i: (i, qb))]
                  + [kv_spec(b, kb) for b in range(C_KBLOCKS)]
                  + [kv_spec(b, vb) for b in range(C_KBLOCKS)]
                  + [pl.BlockSpec((None, C_HEADS, (C_KBLOCKS + 1) * tq), lambda i: (layer, 0, 0))])
    return pl.pallas_call(
        functools.partial(_attn_ac_kernel, layer=layer),
        grid=(SEQ // tq,),
        in_specs=a_in_specs + c_in_specs,
        out_specs=[a_out_spec, pl.BlockSpec((tq, c_width), lambda i: (i, 0))],
        out_shape=[a_out_shape, jax.ShapeDtypeStruct((SEQ, c_width), BF16)],
        scratch_shapes=[pltpu.VMEM((C_HEADS, tq, C_KBLOCKS * tq), F32)],
        compiler_params=_params("arbitrary"),
        name="attn_ac",
    )(sinks, h, h, h, h, h, mask_a, h, *([h] * (2 * C_KBLOCKS)), bias_diag)


def _mix_kernel(x_ref, ya_ref, yb_ref, yc_ref, wga_ref, wgb_ref, wgc_ref, ba_ref, bb_ref, bc_ref,
                wa_ref, wb_ref, wc_ref, o_ref, wg_s, wa_s, wb_s, wc_s):
    m = pl.program_id(1)

    @pl.when(m == 0)
    def _():
        wg_s[0] = wga_ref[...].astype(BF16)
        wg_s[1] = wgb_ref[...].astype(BF16)
        wg_s[2] = wgc_ref[...].astype(BF16)
        wa_s[...] = wa_ref[...].astype(BF16)
        wb_s[...] = wb_ref[...].astype(BF16)
        wc_s[...] = wc_ref[...].astype(BF16)

    for r in range(MIX_TM // ROW_CHUNK):
        rows = slice(r * ROW_CHUNK, (r + 1) * ROW_CHUNK)
        x = x_ref[rows, :]

        def branch(idx, b_ref, y_ref, w_s):
            gate = jax.nn.sigmoid(jnp.dot(x, wg_s[idx], preferred_element_type=F32) + b_ref[...])
            return gate * jnp.dot(y_ref[rows, :], w_s[...], preferred_element_type=F32)

        mix = branch(0, ba_ref, ya_ref, wa_s) + branch(1, bb_ref, yb_ref, wb_s) + branch(2, bc_ref, yc_ref, wc_s)
        o_ref[rows, :] = mix.astype(o_ref.dtype)


def _mix(xb, ya, yb, yc, w_gate, b_gate, w_br_a, w_br_b, w_br_c, layer):
    tm, tn = MIX_TM, MIX_TN
    nb = D_MODEL // tn
    row = lambda width: pl.BlockSpec((tm, width), lambda n, m: (m, 0))
    gate_w = lambda k: pl.BlockSpec((None, D_MODEL, tn), lambda n, m: (layer, 0, k * nb + n))
    gate_b = lambda k: pl.BlockSpec((None, 1, tn), lambda n, m: (layer, 0, k * nb + n))
    br_w = lambda width: pl.BlockSpec((None, width, tn), lambda n, m: (layer, 0, n))
    return pl.pallas_call(
        _mix_kernel,
        grid=(nb, SEQ // tm),
        in_specs=[row(D_MODEL), row(1024), row(512), row(512),
                  gate_w(0), gate_w(1), gate_w(2), gate_b(0), gate_b(1), gate_b(2),
                  br_w(1024), br_w(512), br_w(512)],
        out_specs=pl.BlockSpec((tm, tn), lambda n, m: (m, n)),
        out_shape=jax.ShapeDtypeStruct((SEQ, D_MODEL), BF16),
        scratch_shapes=[pltpu.VMEM((3, D_MODEL, tn), BF16), pltpu.VMEM((1024, tn), BF16),
                        pltpu.VMEM((512, tn), BF16), pltpu.VMEM((512, tn), BF16)],
        compiler_params=_params("arbitrary", "arbitrary"),
        name="mix",
    )(xb, ya, yb, yc, w_gate, w_gate, w_gate, b_gate, b_gate, b_gate, w_br_a, w_br_b, w_br_c)


def _matmul_ln_kernel(y_ref, w_ref, x_ref, g_ref, b_ref, of_ref, ob_ref, *wb_scratch, chunk):
    if wb_scratch:
        wb_ref, = wb_scratch

        @pl.when(pl.program_id(0) == 0)
        def _():
            wb_ref[...] = w_ref[...].astype(BF16)
    else:
        wb_ref = w_ref
    for r in range(y_ref.shape[0] // chunk):
        rows = slice(r * chunk, (r + 1) * chunk)
        z = DEEPNORM_ALPHA * x_ref[rows, :] + jnp.dot(y_ref[rows, :], wb_ref[...], preferred_element_type=F32)
        mu = jnp.mean(z, axis=-1, keepdims=True)
        zc = z - mu
        var = jnp.mean(jnp.square(zc), axis=-1, keepdims=True)
        out = zc * lax.rsqrt(var + LN_EPS) * g_ref[...] + b_ref[...]
        of_ref[rows, :] = out
        ob_ref[rows, :] = out.astype(ob_ref.dtype)


def _matmul_ln(y, w, x, g, b, layer, tm, name):
    k = y.shape[1]
    vec = pl.BlockSpec((None, 1, D_MODEL), lambda m: (layer, 0, 0))
    if w.ndim == 3:
        w_spec = pl.BlockSpec((None, k, D_MODEL), lambda m: (layer, 0, 0), pipeline_mode=pl.Buffered(1))
        scratch = [pltpu.VMEM((k, D_MODEL), BF16)]
    else:
        w_spec = pl.BlockSpec((k, D_MODEL), lambda m: (0, 0), pipeline_mode=pl.Buffered(1))
        scratch = []
    return pl.pallas_call(
        functools.partial(_matmul_ln_kernel, chunk=LN_ROW_CHUNK),
        grid=(SEQ // tm,),
        in_specs=[
            pl.BlockSpec((tm, k), lambda m: (m, 0)),
            w_spec,
            pl.BlockSpec((tm, D_MODEL), lambda m: (m, 0)),
            vec, vec,
        ],
        out_specs=[pl.BlockSpec((tm, D_MODEL), lambda m: (m, 0)),
                   pl.BlockSpec((tm, D_MODEL), lambda m: (m, 0))],
        out_shape=[jax.ShapeDtypeStruct((SEQ, D_MODEL), F32),
                   jax.ShapeDtypeStruct((SEQ, D_MODEL), BF16)],
        scratch_shapes=scratch,
        compiler_params=_params("arbitrary"),
        name=name,
    )(y, w, x, g, b)


def _ffn_in_kernel(x_ref, wg_ref, wu_ref, wo_ref, o_ref, wo_b_ref, wg_s, wu_s):
    m = pl.program_id(1)

    @pl.when(m == 0)
    def _():
        wg_s[...] = wg_ref[...].astype(BF16)
        wu_s[...] = wu_ref[...].astype(BF16)

    @pl.when((m == FFN_WO_PHASE) | ((pl.program_id(0) == 0) & (m == 0)))
    def _():
        wo_b_ref[...] = wo_ref[...].astype(BF16)

    for r in range(FFN_TM // ROW_CHUNK):
        rows = slice(r * ROW_CHUNK, (r + 1) * ROW_CHUNK)
        x = x_ref[rows, :]
        gate = jnp.dot(x, wg_s[...], preferred_element_type=F32)
        up = jnp.dot(x, wu_s[...], preferred_element_type=F32)
        o_ref[rows, :] = (jax.nn.silu(gate) * up).astype(o_ref.dtype)


def _ffn_in(xb, w_ffn_in, w_ffn_out, layer):
    tm, tn = FFN_TM, FFN_TN
    nb = FFN_HIDDEN // tn

    def _wo_block(n, m):
        return jnp.minimum(n + (m >= FFN_WO_PHASE).astype(jnp.int32), nb - 1)

    return pl.pallas_call(
        _ffn_in_kernel,
        grid=(nb, SEQ // tm),
        in_specs=[
            pl.BlockSpec((tm, D_MODEL), lambda n, m: (m, 0)),
            pl.BlockSpec((None, D_MODEL, tn), lambda n, m: (layer, 0, n)),
            pl.BlockSpec((None, D_MODEL, tn), lambda n, m: (layer, 0, nb + n)),
            pl.BlockSpec((None, tn, D_MODEL), lambda n, m: (layer, _wo_block(n, m), 0)),
        ],
        out_specs=[pl.BlockSpec((tm, tn), lambda n, m: (m, n)),
                   pl.BlockSpec((tn, D_MODEL), lambda n, m: (_wo_block(n, m), 0))],
        out_shape=[jax.ShapeDtypeStruct((SEQ, FFN_HIDDEN), BF16),
                   jax.ShapeDtypeStruct((FFN_HIDDEN, D_MODEL), BF16)],
        scratch_shapes=[pltpu.VMEM((D_MODEL, tn), BF16), pltpu.VMEM((D_MODEL, tn), BF16)],
        compiler_params=_params("arbitrary", "arbitrary"),
        name="ffn_in",
    )(xb, w_ffn_in, w_ffn_in, w_ffn_out)


def _rope_tables():
    pos = jnp.arange(SEQ, dtype=F32)

    def cs(dim):
        inv = 1.0 / (ROPE_THETA ** (jnp.arange(0, dim, 2, dtype=F32) / dim))
        ang = pos[:, None] * inv[None, :]
        ang = jnp.concatenate([ang, ang], axis=-1)
        return jnp.cos(ang), jnp.sin(ang)

    cos_a, sin_a = cs(HEAD_DIM)
    half = HEAD_DIM // 2
    sin_a = jnp.concatenate([-sin_a[:, :half], sin_a[:, half:]], axis=-1)
    cos_b, sin_b = cs(B_QK_DIM)
    cos_b2 = jnp.concatenate([cos_b, cos_b], axis=-1)
    sin_b2 = jnp.concatenate([sin_b, sin_b], axis=-1)
    first_half = (np.arange(HEAD_DIM) % B_QK_DIM) < (B_QK_DIM // 2)
    sin_lo = jnp.where(first_half[None, :], -sin_b2, 0.0)
    sin_hi = jnp.where(first_half[None, :], 0.0, sin_b2)
    return cos_a, sin_a, cos_b2, sin_lo, sin_hi


def _mask_a():
    qc = (np.arange(A_TQ) // CHUNK)[:, None]
    kc = (np.arange(A_WINDOW + A_TQ) // CHUNK)[None, :]
    ok = (kc >= qc) & (kc <= qc + A_WINDOW // CHUNK)
    m = np.where(ok, 0.0, NEG_INF).astype(np.float32)
    return jnp.asarray(np.tile(m, (A_GROUP, 1)))


def _bias_diagonals(rel_bias):
    nq, nk = C_TQ, C_KBLOCKS * C_TQ
    rb = rel_bias.astype(F32)
    n_clip = (nk - nq) - REL_CLIP
    far = jnp.broadcast_to(rb[..., -1:], rb.shape[:-1] + (n_clip,))
    near = rb[..., REL_CLIP - (nq - 1):][..., ::-1]
    d_nonneg = jnp.concatenate([far, near], axis=-1)
    assert d_nonneg.shape[-1] == nk
    d_neg = jnp.broadcast_to(rb[..., -1:], rb.shape[:-1] + (nq,))
    return jnp.concatenate([d_nonneg, d_neg], axis=-1)


def kernel(x, w_in, sinks, lambda_q1, lambda_k1, lambda_q2, lambda_k2, diff_norm_g, rel_bias,
           w_br_a, w_br_b, w_br_c, w_gate, b_gate, w_out, ln1_g, ln1_b,
           w_ffn_in, w_ffn_out, ln2_g, ln2_b):
    assert x.shape == (1, SEQ, D_MODEL)
    tabs = _rope_tables()
    mask_a = _mask_a()
    xf = x.reshape(SEQ, D_MODEL)
    xb = xf.astype(BF16)
    lam_vecs = jnp.stack([lambda_q1, lambda_k1, lambda_q2, lambda_k2], axis=1).astype(F32)
    sub_g = diff_norm_g.astype(F32).reshape(DEPTH, HEAD_DIM, 1)
    bias_diag = _bias_diagonals(rel_bias)
    b_gate3 = b_gate.reshape(DEPTH, 1, -1)
    vec3 = lambda v: v.reshape(DEPTH, 1, D_MODEL)
    ln1_g, ln1_b, ln2_g, ln2_b = vec3(ln1_g), vec3(ln1_b), vec3(ln2_g), vec3(ln2_b)
    for l in range(DEPTH):
        lam_init = 0.8 - 0.6 * math.exp(-0.3 * l)
        h = _in_proj(xb, w_in, l, tabs)
        ya, yc = _attn_ac(h, sinks, mask_a, bias_diag, l)
        yb = _attn_b(h, lam_vecs, sub_g, l, lam_init)
        mix = _mix(xb, ya, yb, yc, w_gate, b_gate3, w_br_a, w_br_b, w_br_c, l)
        xf, xb = _matmul_ln(mix, w_out, xf, ln1_g, ln1_b, l, OUT_TM, "out_ln")
        f, w_ffn_out_b = _ffn_in(xb, w_ffn_in, w_ffn_out, l)
        xf, xb = _matmul_ln(f, w_ffn_out_b, xf, ln2_g, ln2_b, l, FFN_OUT_TM, "ffn_out")
    return xf.reshape(1, SEQ, D_MODEL)
```

```python
import functools
import math

import jax
import jax.numpy as jnp
import numpy as np
from jax import lax
from jax.experimental import pallas as pl
from jax.experimental.pallas import tpu as pltpu

D_MODEL = 2048
SEQ = 8192
DEPTH = 4
CHUNK = 64
HEAD_DIM = 128
A_Q_HEADS = 8
A_KV_HEADS = 2
A_GROUP = A_Q_HEADS // A_KV_HEADS
B_HEADS = 4
B_QK_DIM = 64
C_HEADS = 4
C_PREV_CHUNKS = 8
REL_CLIP = 256
FFN_HIDDEN = 5632
IN_WIDTH = 4608
ROPE_THETA = 10000.0
LN_EPS = 1e-5
DEEPNORM_ALPHA = (2 * DEPTH) ** 0.25
NEG_INF = -1e30

BF16 = jnp.bfloat16
F32 = jnp.float32

VMEM_LIMIT_BYTES = 56 * 1024 * 1024

AQ_OFF, AK_OFF, AV_OFF = 0, 1024, 1280
BQ_OFF, BK_OFF, BV_OFF = 1536, 2048, 2560
CQ_OFF, CK_OFF, CV_OFF = 3072, 3584, 4096

LN_ROW_CHUNK = 128
ROW_CHUNK = 256
PROJ_TM, PROJ_TN = 1024, 1536
MIX_TM, MIX_TN = 512, 512
OUT_TM = 512
FFN_TM, FFN_TN = 1024, 512
FFN_WO_PHASE = 4
FFN_OUT_TM = 256
A_TQ = 256
A_WINDOW = 128
B_TQ = 512
B_TK = 512
B_ONES_ROWS = 16
C_TQ = 256
C_KBLOCKS = (C_PREV_CHUNKS * CHUNK) // C_TQ + 1


def _params(*sem):
    return pltpu.CompilerParams(dimension_semantics=sem, vmem_limit_bytes=VMEM_LIMIT_BYTES)


def _rope_a(t, cos, sin_signed):
    return t * cos + pltpu.roll(t, HEAD_DIM // 2, 1) * sin_signed


def _rope_b(t, cos2, sin_lo, sin_hi):
    return t * cos2 + pltpu.roll(t, 96, 1) * sin_lo + pltpu.roll(t, 32, 1) * sin_hi


def _in_proj_kernel(x_ref, w_ref, cosa_ref, sina_ref, cosb_ref, sinb_lo_ref, sinb_hi_ref,
                    o_ref, wb_ref):
    n = pl.program_id(0)
    m = pl.program_id(1)

    @pl.when(m == 0)
    def _():
        wb_ref[...] = w_ref[...].astype(BF16)

    a_scale = HEAD_DIM ** -0.5
    b_scale = B_QK_DIM ** -0.5 * math.log2(math.e)

    def run(epilogue):
        for r in range(PROJ_TM // ROW_CHUNK):
            rows = slice(r * ROW_CHUNK, (r + 1) * ROW_CHUNK)
            acc = jnp.dot(x_ref[rows, :], wb_ref[...], preferred_element_type=F32)
            epilogue(acc, rows)

    def head_cols(acc, rows, lo, hi, fn, scale):
        for j in range(lo, hi):
            sl = slice(j * HEAD_DIM, (j + 1) * HEAD_DIM)
            r = fn(acc[:, sl])
            if scale != 1.0:
                r = r * scale
            o_ref[rows, sl] = r.astype(o_ref.dtype)

    def plain_cols(acc, rows, lo, scale=1.0):
        t = acc[:, lo:]
        if scale != 1.0:
            t = t * scale
        o_ref[rows, lo:] = t.astype(o_ref.dtype)

    def mixer_a(acc, rows):
        cos, sin = cosa_ref[rows, :], sina_ref[rows, :]
        rope = lambda t: _rope_a(t, cos, sin)
        head_cols(acc, rows, 0, A_Q_HEADS, rope, a_scale)
        head_cols(acc, rows, A_Q_HEADS, A_Q_HEADS + A_KV_HEADS, rope, 1.0)
        plain_cols(acc, rows, (A_Q_HEADS + A_KV_HEADS) * HEAD_DIM)

    def mixer_b(acc, rows):
        cos, lo_, hi_ = cosb_ref[rows, :], sinb_lo_ref[rows, :], sinb_hi_ref[rows, :]
        rope = lambda t: _rope_b(t, cos, lo_, hi_)
        head_cols(acc, rows, 0, B_HEADS, rope, b_scale)
        head_cols(acc, rows, B_HEADS, 2 * B_HEADS, rope, 1.0)
        plain_cols(acc, rows, 2 * B_HEADS * HEAD_DIM)

    def mixer_c(acc, rows):
        o_ref[rows, :C_HEADS * HEAD_DIM] = (acc[:, :C_HEADS * HEAD_DIM] * a_scale).astype(o_ref.dtype)
        plain_cols(acc, rows, C_HEADS * HEAD_DIM)

    pl.when(n == 0)(lambda: run(mixer_a))
    pl.when(n == 1)(lambda: run(mixer_b))
    pl.when(n == 2)(lambda: run(mixer_c))


def _in_proj(xb, w_in, layer, tabs):
    tm, tn = PROJ_TM, PROJ_TN
    rope_spec = pl.BlockSpec((tm, HEAD_DIM), lambda n, m: (m, 0))
    return pl.pallas_call(
        _in_proj_kernel,
        grid=(IN_WIDTH // tn, SEQ // tm),
        in_specs=[
            pl.BlockSpec((tm, D_MODEL), lambda n, m: (m, 0)),
            pl.BlockSpec((None, D_MODEL, tn), lambda n, m: (layer, 0, n)),
            rope_spec, rope_spec, rope_spec, rope_spec, rope_spec,
        ],
        out_specs=pl.BlockSpec((tm, tn), lambda n, m: (m, n)),
        out_shape=jax.ShapeDtypeStruct((SEQ, IN_WIDTH), BF16),
        scratch_shapes=[pltpu.VMEM((D_MODEL, tn), BF16)],
        compiler_params=_params("arbitrary", "arbitrary"),
        name="in_proj",
    )(xb, w_in, *tabs)


def _attn_a_stages(sink_ref, q_ref, kp_ref, kc_ref, vp_ref, vc_ref, mask_ref, o_ref, layer):
    i = pl.program_id(0)
    has_prev = i > 0
    dn = (((1,), (1,)), ((), ()))
    hcols = lambda hd: slice(hd * HEAD_DIM, (hd + 1) * HEAD_DIM)
    scores = []
    for g in range(A_KV_HEADS):
        q = jnp.concatenate([q_ref[:, hcols(A_GROUP * g + j)] for j in range(A_GROUP)], axis=0)
        scores.append((lax.dot_general(q, kp_ref[:, hcols(g)], dn, preferred_element_type=F32),
                       lax.dot_general(q, kc_ref[:, hcols(g)], dn, preferred_element_type=F32)))

    def stage(g):
        s_prev, s_cur = scores[g]
        s_prev = jnp.where(has_prev, s_prev + mask_ref[:, :A_WINDOW], NEG_INF)
        s_cur = s_cur + mask_ref[:, A_WINDOW:]
        sink = jnp.concatenate(
            [jnp.full((A_TQ, HEAD_DIM), sink_ref[layer, A_GROUP * g + j], F32) for j in range(A_GROUP)], axis=0)
        folded = functools.reduce(
            jnp.maximum, [s_prev] + [s_cur[:, c:c + HEAD_DIM] for c in range(0, A_TQ, HEAD_DIM)])
        mx = jnp.maximum(jnp.broadcast_to(folded.max(-1, keepdims=True), sink.shape), sink)
        p_prev = jnp.exp(s_prev - mx)
        p_cur = jnp.exp(s_cur - jnp.concatenate([mx] * (A_TQ // HEAD_DIM), axis=1))
        ones = lambda rows: jnp.ones((rows, HEAD_DIM), BF16)
        v_prev = jnp.concatenate([vp_ref[:, hcols(g)], ones(A_WINDOW)], axis=1)
        v_cur = jnp.concatenate([vc_ref[:, hcols(g)], ones(A_TQ)], axis=1)
        o_ext = (jnp.dot(p_prev.astype(BF16), v_prev, preferred_element_type=F32)
                 + jnp.dot(p_cur.astype(BF16), v_cur, preferred_element_type=F32))
        denom = o_ext[:, HEAD_DIM:] + jnp.exp(sink - mx)
        o = o_ext[:, :HEAD_DIM] * (1.0 / denom)
        for j in range(A_GROUP):
            o_ref[:, hcols(A_GROUP * g + j)] = o[j * A_TQ:(j + 1) * A_TQ].astype(o_ref.dtype)

    return [functools.partial(stage, g) for g in range(A_KV_HEADS)]


def _attn_a_specs():
    tq, win = A_TQ, A_WINDOW
    kv_width = A_KV_HEADS * HEAD_DIM
    kblk, vblk = AK_OFF // kv_width, AV_OFF // kv_width
    prev = lambda i: jnp.maximum(i * (tq // win) - 1, 0)
    in_specs = [
        pl.BlockSpec(memory_space=pltpu.SMEM),
        pl.BlockSpec((tq, A_Q_HEADS * HEAD_DIM), lambda i: (i, 0)),
        pl.BlockSpec((win, kv_width), lambda i: (prev(i), kblk)),
        pl.BlockSpec((tq, kv_width), lambda i: (i, kblk)),
        pl.BlockSpec((win, kv_width), lambda i: (prev(i), vblk)),
        pl.BlockSpec((tq, kv_width), lambda i: (i, vblk)),
        pl.BlockSpec((A_GROUP * tq, win + tq), lambda i: (0, 0)),
    ]
    out_spec = pl.BlockSpec((tq, A_Q_HEADS * HEAD_DIM), lambda i: (i, 0))
    return in_specs, out_spec, jax.ShapeDtypeStruct((SEQ, A_Q_HEADS * HEAD_DIM), BF16)


def _attn_b_kernel(lam_ref, g_ref, q_ref, k_ref, v_ref, o_ref, vt_s, qst_s, acc_s, s_s, m_s, *, lam_init):
    i = pl.program_id(0)
    tq, tk = B_TQ, B_TK
    hcols = lambda hd: slice(hd * HEAD_DIM, (hd + 1) * HEAD_DIM)

    @pl.when(i == 0)
    def _():
        ones = jnp.ones((B_ONES_ROWS, tk), BF16)
        for hd in range(B_HEADS):
            def transpose_block(c, carry, hd=hd):
                start = pl.multiple_of(c * tk, tk)
                blk = v_ref[pl.ds(start, tk), hcols(hd)].astype(F32)
                vt_s[hd, c, :HEAD_DIM, :] = blk.T.astype(BF16)
                vt_s[hd, c, HEAD_DIM:, :] = ones
                return carry
            lax.fori_loop(0, SEQ // tk, transpose_block, 0)

    feat = lax.broadcasted_iota(jnp.int32, (HEAD_DIM, tq), 0)
    for hd in range(B_HEADS):
        qt = q_ref[:, hcols(hd)].astype(F32).T
        qst_s[hd, :, :tq] = jnp.where(feat < B_QK_DIM, qt, 0.0).astype(BF16)
        qst_s[hd, :, tq:] = jnp.where(feat >= B_QK_DIM, qt, 0.0).astype(BF16)
    acc_s[...] = jnp.zeros_like(acc_s)

    def scores(j, hd):
        kj = k_ref[pl.ds(pl.multiple_of(j * tk, tk), tk), hcols(hd)]
        return jnp.dot(kj, qst_s[hd], preferred_element_type=F32)

    def block(j, src, dst, diagonal=False):
        for hd in range(B_HEADS):
            if dst is not None:
                s_s[dst, hd] = scores(j + 1, hd)
            s = s_s[src, hd]
            if diagonal:
                kc = lax.broadcasted_iota(jnp.int32, s.shape, 0) // CHUNK
                qc = (lax.broadcasted_iota(jnp.int32, s.shape, 1) % tq) // CHUNK
                s = jnp.where(kc <= qc, s, NEG_INF)
            m_old = m_s[hd]
            m_new = jnp.maximum(m_old, s.max(axis=0, keepdims=True))
            m_s[hd] = m_new
            a = jnp.exp2(m_old - m_new)
            p = jnp.exp2(s - m_new).astype(BF16)
            acc_s[hd] = a * acc_s[hd] + jnp.dot(vt_s[hd, j], p, preferred_element_type=F32)

    assert tq == tk
    for hd in range(B_HEADS):
        s_s[0, hd] = scores(0, hd)
    m_s[...] = jnp.full(m_s.shape, NEG_INF, F32)

    def pair(t, carry):
        block(2 * t, 0, 1)
        block(2 * t + 1, 1, 0)
        return carry

    lax.fori_loop(0, i // 2, pair, 0)

    @pl.when(i % 2 == 0)
    def _():
        block(i, 0, None, diagonal=True)

    @pl.when(i % 2 == 1)
    def _():
        block(i - 1, 0, 1)
        block(i, 1, None, diagonal=True)

    lam = (jnp.exp(jnp.sum(lam_ref[0:1, :] * lam_ref[1:2, :], axis=-1, keepdims=True))
           - jnp.exp(jnp.sum(lam_ref[2:3, :] * lam_ref[3:4, :], axis=-1, keepdims=True))
           + lam_init)
    for hd in range(B_HEADS):
        acc = acc_s[hd]
        o_all = acc[:HEAD_DIM] * (1.0 / acc[HEAD_DIM:HEAD_DIM + 1])
        o = o_all[:, :tq] - lam * o_all[:, tq:]
        o = o * lax.rsqrt(jnp.mean(jnp.square(o), axis=0, keepdims=True) + LN_EPS)
        o = o * g_ref[...] * (1.0 - lam_init)
        o_ref[:, hcols(hd)] = o.T.astype(o_ref.dtype)


def _attn_b(h, lam_vecs, sub_g, layer, lam_init):
    tq, tk = B_TQ, B_TK
    width = B_HEADS * HEAD_DIM
    resident = lambda col: pl.BlockSpec((SEQ, width), lambda i: (0, col), pipeline_mode=pl.Buffered(1))
    return pl.pallas_call(
        functools.partial(_attn_b_kernel, lam_init=lam_init),
        grid=(SEQ // tq,),
        in_specs=[
            pl.BlockSpec((None, 4, B_QK_DIM), lambda i: (layer, 0, 0)),
            pl.BlockSpec((None, HEAD_DIM, 1), lambda i: (layer, 0, 0)),
            pl.BlockSpec((tq, width), lambda i: (i, BQ_OFF // width)),
            resident(BK_OFF // width),
            resident(BV_OFF // width),
        ],
        out_specs=pl.BlockSpec((tq, width), lambda i: (i, 0)),
        out_shape=jax.ShapeDtypeStruct((SEQ, width), BF16),
        scratch_shapes=[
            pltpu.VMEM((B_HEADS, SEQ // tk, HEAD_DIM + B_ONES_ROWS, tk), BF16),
            pltpu.VMEM((B_HEADS, HEAD_DIM, 2 * tq), BF16),
            pltpu.VMEM((B_HEADS, HEAD_DIM + B_ONES_ROWS, 2 * tq), F32),
            pltpu.VMEM((2, B_HEADS, tk, 2 * tq), F32),
            pltpu.VMEM((B_HEADS, 1, 2 * tq), F32),
        ],
        compiler_params=_params("arbitrary"),
        name="attn_b",
    )(lam_vecs, sub_g, h, h, h)


def _attn_c_build_bias(diag_ref, bias_ref):
    nk = C_KBLOCKS * C_TQ
    qc = lax.broadcasted_iota(jnp.int32, (C_TQ, nk), 0) // CHUNK
    kc = lax.broadcasted_iota(jnp.int32, (C_TQ, nk), 1) // CHUNK
    band = (kc >= qc) & (kc <= qc + C_PREV_CHUNKS)
    for hd in range(C_HEADS):
        rows = jnp.broadcast_to(diag_ref[hd:hd + 1, :], (C_TQ, C_TQ + nk))
        skew = pltpu.roll(rows, 0, 1, stride=1, stride_axis=0)
        bias_ref[hd] = jnp.where(band, skew[:, :nk], NEG_INF)


def _attn_c_stages(q_ref, k_refs, v_refs, o_ref, bias_ref):
    i = pl.program_id(0)
    dn = (((1,), (1,)), ((), ()))
    hcols = lambda hd: slice(hd * HEAD_DIM, (hd + 1) * HEAD_DIM)
    scores = [[lax.dot_general(q_ref[:, hcols(hd)], k_refs[b][:, hcols(hd)], dn, preferred_element_type=F32)
               for b in range(C_KBLOCKS)] for hd in range(C_HEADS)]

    def stage(hd):
        hsl = hcols(hd)
        s = []
        for b in range(C_KBLOCKS):
            sb = scores[hd][b] + bias_ref[hd, :, b * C_TQ:(b + 1) * C_TQ]
            s.append(jnp.where(i - (C_KBLOCKS - 1) + b >= 0, sb, NEG_INF))
        mx = functools.reduce(jnp.maximum, [sb.max(-1, keepdims=True) for sb in s])
        p = [jnp.exp(sb - mx) for sb in s]
        ones = jnp.ones((C_TQ, HEAD_DIM), BF16)
        o_ext = functools.reduce(
            lambda a, b_: a + b_,
            [jnp.dot(p[b].astype(BF16), jnp.concatenate([v_refs[b][:, hsl], ones], axis=1),
                     preferred_element_type=F32) for b in range(C_KBLOCKS)])
        o_ref[:, hsl] = (o_ext[:, :HEAD_DIM] * (1.0 / o_ext[:, HEAD_DIM:])).astype(o_ref.dtype)

    return [functools.partial(stage, hd) for hd in range(C_HEADS)]


def _attn_ac_kernel(*refs, layer):
    n_a = 7
    a_in, c_in = refs[:n_a], refs[n_a:n_a + 2 + 2 * C_KBLOCKS]
    oa_ref, oc_ref, bias_ref = refs[n_a + 2 + 2 * C_KBLOCKS:]
    pl.when(pl.program_id(0) == 0)(lambda: _attn_c_build_bias(c_in[-1], bias_ref))
    c_stages = _attn_c_stages(c_in[0], c_in[1:1 + C_KBLOCKS], c_in[1 + C_KBLOCKS:1 + 2 * C_KBLOCKS],
                              oc_ref, bias_ref)
    a_stages = _attn_a_stages(*a_in, oa_ref, layer)
    per_a = len(c_stages) // len(a_stages)
    for g, a_stage in enumerate(a_stages):
        for c_stage in c_stages[g * per_a:(g + 1) * per_a]:
            c_stage()
        a_stage()


def _attn_ac(h, sinks, mask_a, bias_diag, layer):
    assert A_TQ == C_TQ
    tq = C_TQ
    c_width = C_HEADS * HEAD_DIM
    qb, kb, vb = CQ_OFF // c_width, CK_OFF // c_width, CV_OFF // c_width

    def kv_spec(b, col):
        return pl.BlockSpec((tq, c_width), lambda i: (jnp.maximum(i - (C_KBLOCKS - 1) + b, 0), col))

    a_in_specs, a_out_spec, a_out_shape = _attn_a_specs()
    c_in_specs = ([pl.BlockSpec((tq, c_width), lambda i: (i, qb))]
                  + [kv_spec(b, kb) for b in range(C_KBLOCKS)]
                  + [kv_spec(b, vb) for b in range(C_KBLOCKS)]
                  + [pl.BlockSpec((None, C_HEADS, (C_KBLOCKS + 1) * tq), lambda i: (layer, 0, 0))])
    return pl.pallas_call(
        functools.partial(_attn_ac_kernel, layer=layer),
        grid=(SEQ // tq,),
        in_specs=a_in_specs + c_in_specs,
        out_specs=[a_out_spec, pl.BlockSpec((tq, c_width), lambda i: (i, 0))],
        out_shape=[a_out_shape, jax.ShapeDtypeStruct((SEQ, c_width), BF16)],
        scratch_shapes=[pltpu.VMEM((C_HEADS, tq, C_KBLOCKS * tq), F32)],
        compiler_params=_params("arbitrary"),
        name="attn_ac",
    )(sinks, h, h, h, h, h, mask_a, h, *([h] * (2 * C_KBLOCKS)), bias_diag)


def _mix_kernel(x_ref, ya_ref, yb_ref, yc_ref, wga_ref, wgb_ref, wgc_ref, ba_ref, bb_ref, bc_ref,
                wa_ref, wb_ref, wc_ref, o_ref, wg_s, wa_s, wb_s, wc_s):
    m = pl.program_id(1)

    @pl.when(m == 0)
    def _():
        wg_s[0] = wga_ref[...].astype(BF16)
        wg_s[1] = wgb_ref[...].astype(BF16)
        wg_s[2] = wgc_ref[...].astype(BF16)
        wa_s[...] = wa_ref[...].astype(BF16)
        wb_s[...] = wb_ref[...].astype(BF16)
        wc_s[...] = wc_ref[...].astype(BF16)

    for r in range(MIX_TM // ROW_CHUNK):
        rows = slice(r * ROW_CHUNK, (r + 1) * ROW_CHUNK)
        x = x_ref[rows, :]

        def branch(idx, b_ref, y_ref, w_s):
            gate = jax.nn.sigmoid(jnp.dot(x, wg_s[idx], preferred_element_type=F32) + b_ref[...])
            return gate * jnp.dot(y_ref[rows, :], w_s[...], preferred_element_type=F32)

        mix = branch(0, ba_ref, ya_ref, wa_s) + branch(1, bb_ref, yb_ref, wb_s) + branch(2, bc_ref, yc_ref, wc_s)
        o_ref[rows, :] = mix.astype(o_ref.dtype)


def _mix(xb, ya, yb, yc, w_gate, b_gate, w_br_a, w_br_b, w_br_c, layer):
    tm, tn = MIX_TM, MIX_TN
    nb = D_MODEL // tn
    row = lambda width: pl.BlockSpec((tm, width), lambda n, m: (m, 0))
    gate_w = lambda k: pl.BlockSpec((None, D_MODEL, tn), lambda n, m: (layer, 0, k * nb + n))
    gate_b = lambda k: pl.BlockSpec((None, 1, tn), lambda n, m: (layer, 0, k * nb + n))
    br_w = lambda width: pl.BlockSpec((None, width, tn), lambda n, m: (layer, 0, n))
    return pl.pallas_call(
        _mix_kernel,
        grid=(nb, SEQ // tm),
        in_specs=[row(D_MODEL), row(1024), row(512), row(512),
                  gate_w(0), gate_w(1), gate_w(2), gate_b(0), gate_b(1), gate_b(2),
                  br_w(1024), br_w(512), br_w(512)],
        out_specs=pl.BlockSpec((tm, tn), lambda n, m: (m, n)),
        out_shape=jax.ShapeDtypeStruct((SEQ, D_MODEL), BF16),
        scratch_shapes=[pltpu.VMEM((3, D_MODEL, tn), BF16), pltpu.VMEM((1024, tn), BF16),
                        pltpu.VMEM((512, tn), BF16), pltpu.VMEM((512, tn), BF16)],
        compiler_params=_params("arbitrary", "arbitrary"),
        name="mix",
    )(xb, ya, yb, yc, w_gate, w_gate, w_gate, b_gate, b_gate, b_gate, w_br_a, w_br_b, w_br_c)


def _matmul_ln_kernel(y_ref, w_ref, x_ref, g_ref, b_ref, of_ref, ob_ref, *wb_scratch, chunk):
    if wb_scratch:
        wb_ref, = wb_scratch

        @pl.when(pl.program_id(0) == 0)
        def _():
            wb_ref[...] = w_ref[...].astype(BF16)
    else:
        wb_ref = w_ref
    for r in range(y_ref.shape[0] // chunk):
        rows = slice(r * chunk, (r + 1) * chunk)
        z = DEEPNORM_ALPHA * x_ref[rows, :] + jnp.dot(y_ref[rows, :], wb_ref[...], preferred_element_type=F32)
        mu = jnp.mean(z, axis=-1, keepdims=True)
        zc = z - mu
        var = jnp.mean(jnp.square(zc), axis=-1, keepdims=True)
        out = zc * lax.rsqrt(var + LN_EPS) * g_ref[...] + b_ref[...]
        of_ref[rows, :] = out
        ob_ref[rows, :] = out.astype(ob_ref.dtype)


def _matmul_ln(y, w, x, g, b, layer, tm, name):
    k = y.shape[1]
    vec = pl.BlockSpec((None, 1, D_MODEL), lambda m: (layer, 0, 0))
    if w.ndim == 3:
        w_spec = pl.BlockSpec((None, k, D_MODEL), lambda m: (layer, 0, 0), pipeline_mode=pl.Buffered(1))
        scratch = [pltpu.VMEM((k, D_MODEL), BF16)]
    else:
        w_spec = pl.BlockSpec((k, D_MODEL), lambda m: (0, 0), pipeline_mode=pl.Buffered(1))
        scratch = []
    return pl.pallas_call(
        functools.partial(_matmul_ln_kernel, chunk=LN_ROW_CHUNK),
        grid=(SEQ // tm,),
        in_specs=[
            pl.BlockSpec((tm, k), lambda m: (m, 0)),
            w_spec,
            pl.BlockSpec((tm, D_MODEL), lambda m: (m, 0)),
            vec, vec,
        ],
        out_specs=[pl.BlockSpec((tm, D_MODEL), lambda m: (m, 0)),
                   pl.BlockSpec((tm, D_MODEL), lambda m: (m, 0))],
        out_shape=[jax.ShapeDtypeStruct((SEQ, D_MODEL), F32),
                   jax.ShapeDtypeStruct((SEQ, D_MODEL), BF16)],
        scratch_shapes=scratch,
        compiler_params=_params("arbitrary"),
        name=name,
    )(y, w, x, g, b)


def _ffn_in_kernel(x_ref, wg_ref, wu_ref, wo_ref, o_ref, wo_b_ref, wg_s, wu_s):
    m = pl.program_id(1)

    @pl.when(m == 0)
    def _():
        wg_s[...] = wg_ref[...].astype(BF16)
        wu_s[...] = wu_ref[...].astype(BF16)

    @pl.when((m == FFN_WO_PHASE) | ((pl.program_id(0) == 0) & (m == 0)))
    def _():
        wo_b_ref[...] = wo_ref[...].astype(BF16)

    for r in range(FFN_TM // ROW_CHUNK):
        rows = slice(r * ROW_CHUNK, (r + 1) * ROW_CHUNK)
        x = x_ref[rows, :]
        gate = jnp.dot(x, wg_s[...], preferred_element_type=F32)
        up = jnp.dot(x, wu_s[...], preferred_element_type=F32)
        o_ref[rows, :] = (jax.nn.silu(gate) * up).astype(o_ref.dtype)


def _ffn_in(xb, w_ffn_in, w_ffn_out, layer):
    tm, tn = FFN_TM, FFN_TN
    nb = FFN_HIDDEN // tn

    def _wo_block(n, m):
        return jnp.minimum(n + (m >= FFN_WO_PHASE).astype(jnp.int32), nb - 1)

    return pl.pallas_call(
        _ffn_in_kernel,
        grid=(nb, SEQ // tm),
        in_specs=[
            pl.BlockSpec((tm, D_MODEL), lambda n, m: (m, 0)),
            pl.BlockSpec((None, D_MODEL, tn), lambda n, m: (layer, 0, n)),
            pl.BlockSpec((None, D_MODEL, tn), lambda n, m: (layer, 0, nb + n)),
            pl.BlockSpec((None, tn, D_MODEL), lambda n, m: (layer, _wo_block(n, m), 0)),
        ],
        out_specs=[pl.BlockSpec((tm, tn), lambda n, m: (m, n)),
                   pl.BlockSpec((tn, D_MODEL), lambda n, m: (_wo_block(n, m), 0))],
        out_shape=[jax.ShapeDtypeStruct((SEQ, FFN_HIDDEN), BF16),
                   jax.ShapeDtypeStruct((FFN_HIDDEN, D_MODEL), BF16)],
        scratch_shapes=[pltpu.VMEM((D_MODEL, tn), BF16), pltpu.VMEM((D_MODEL, tn), BF16)],
        compiler_params=_params("arbitrary", "arbitrary"),
        name="ffn_in",
    )(xb, w_ffn_in, w_ffn_in, w_ffn_out)


def _rope_tables():
    pos = jnp.arange(SEQ, dtype=F32)

    def cs(dim):
        inv = 1.0 / (ROPE_THETA ** (jnp.arange(0, dim, 2, dtype=F32) / dim))
        ang = pos[:, None] * inv[None, :]
        ang = jnp.concatenate([ang, ang], axis=-1)
        return jnp.cos(ang), jnp.sin(ang)

    cos_a, sin_a = cs(HEAD_DIM)
    half = HEAD_DIM // 2
    sin_a = jnp.concatenate([-sin_a[:, :half], sin_a[:, half:]], axis=-1)
    cos_b, sin_b = cs(B_QK_DIM)
    cos_b2 = jnp.concatenate([cos_b, cos_b], axis=-1)
    sin_b2 = jnp.concatenate([sin_b, sin_b], axis=-1)
    first_half = (np.arange(HEAD_DIM) % B_QK_DIM) < (B_QK_DIM // 2)
    sin_lo = jnp.where(first_half[None, :], -sin_b2, 0.0)
    sin_hi = jnp.where(first_half[None, :], 0.0, sin_b2)
    return cos_a, sin_a, cos_b2, sin_lo, sin_hi


def _mask_a():
    qc = (np.arange(A_TQ) // CHUNK)[:, None]
    kc = (np.arange(A_WINDOW + A_TQ) // CHUNK)[None, :]
    ok = (kc >= qc) & (kc <= qc + A_WINDOW // CHUNK)
    m = np.where(ok, 0.0, NEG_INF).astype(np.float32)
    return jnp.asarray(np.tile(m, (A_GROUP, 1)))


def _bias_diagonals(rel_bias):
    nq, nk = C_TQ, C_KBLOCKS * C_TQ
    rb = rel_bias.astype(F32)
    n_clip = (nk - nq) - REL_CLIP
    far = jnp.broadcast_to(rb[..., -1:], rb.shape[:-1] + (n_clip,))
    near = rb[..., REL_CLIP - (nq - 1):][..., ::-1]
    d_nonneg = jnp.concatenate([far, near], axis=-1)
    assert d_nonneg.shape[-1] == nk
    d_neg = jnp.broadcast_to(rb[..., -1:], rb.shape[:-1] + (nq,))
    return jnp.concatenate([d_nonneg, d_neg], axis=-1)


def kernel(x, w_in, sinks, lambda_q1, lambda_k1, lambda_q2, lambda_k2, diff_norm_g, rel_bias,
           w_br_a, w_br_b, w_br_c, w_gate, b_gate, w_out, ln1_g, ln1_b,
           w_ffn_in, w_ffn_out, ln2_g, ln2_b):
    assert x.shape == (1, SEQ, D_MODEL)
    tabs = _rope_tables()
    mask_a = _mask_a()
    xf = x.reshape(SEQ, D_MODEL)
    xb = xf.astype(BF16)
    lam_vecs = jnp.stack([lambda_q1, lambda_k1, lambda_q2, lambda_k2], axis=1).astype(F32)
    sub_g = diff_norm_g.astype(F32).reshape(DEPTH, HEAD_DIM, 1)
    bias_diag = _bias_diagonals(rel_bias)
    b_gate3 = b_gate.reshape(DEPTH, 1, -1)
    vec3 = lambda v: v.reshape(DEPTH, 1, D_MODEL)
    ln1_g, ln1_b, ln2_g, ln2_b = vec3(ln1_g), vec3(ln1_b), vec3(ln2_g), vec3(ln2_b)
    for l in range(DEPTH):
        lam_init = 0.8 - 0.6 * math.exp(-0.3 * l)
        h = _in_proj(xb, w_in, l, tabs)
        ya, yc = _attn_ac(h, sinks, mask_a, bias_diag, l)
        yb = _attn_b(h, lam_vecs, sub_g, l, lam_init)
        mix = _mix(xb, ya, yb, yc, w_gate, b_gate3, w_br_a, w_br_b, w_br_c, l)
        xf, xb = _matmul_ln(mix, w_out, xf, ln1_g, ln1_b, l, OUT_TM, "out_ln")
        f, w_ffn_out_b = _ffn_in(xb, w_ffn_in, w_ffn_out, l)
        xf, xb = _matmul_ln(f, w_ffn_out_b, xf, ln2_g, ln2_b, l, FFN_OUT_TM, "ffn_out")
    return xf.reshape(1, SEQ, D_MODEL)
```

```python
import functools
import math

import jax
import jax.numpy as jnp
import numpy as np
from jax import lax
from jax.experimental import pallas as pl
from jax.experimental.pallas import tpu as pltpu

D_MODEL = 2048
SEQ = 8192
DEPTH = 4
CHUNK = 64
HEAD_DIM = 128
A_Q_HEADS = 8
A_KV_HEADS = 2
A_GROUP = A_Q_HEADS // A_KV_HEADS
B_HEADS = 4
B_QK_DIM = 64
C_HEADS = 4
C_PREV_CHUNKS = 8
REL_CLIP = 256
FFN_HIDDEN = 5632
IN_WIDTH = 4608
ROPE_THETA = 10000.0
LN_EPS = 1e-5
DEEPNORM_ALPHA = (2 * DEPTH) ** 0.25
NEG_INF = -1e30

BF16 = jnp.bfloat16
F32 = jnp.float32

VMEM_LIMIT_BYTES = 56 * 1024 * 1024

AQ_OFF, AK_OFF, AV_OFF = 0, 1024, 1280
BQ_OFF, BK_OFF, BV_OFF = 1536, 2048, 2560
CQ_OFF, CK_OFF, CV_OFF = 3072, 3584, 4096

LN_ROW_CHUNK = 128
ROW_CHUNK = 256
PROJ_TM, PROJ_TN = 1024, 1536
MIX_TM, MIX_TN = 512, 512
OUT_TM = 512
FFN_TM, FFN_TN = 1024, 512
FFN_WO_PHASE = 4
FFN_OUT_TM = 256
A_TQ = 256
A_WINDOW = 128
B_TQ = 512
B_TK = 512
B_ONES_ROWS = 16
C_TQ = 256
C_KBLOCKS = (C_PREV_CHUNKS * CHUNK) // C_TQ + 1


def _params(*sem):
    return pltpu.CompilerParams(dimension_semantics=sem, vmem_limit_bytes=VMEM_LIMIT_BYTES)


def _rope_a(t, cos, sin_signed):
    return t * cos + pltpu.roll(t, HEAD_DIM // 2, 1) * sin_signed


def _rope_b(t, cos2, sin_lo, sin_hi):
    return t * cos2 + pltpu.roll(t, 96, 1) * sin_lo + pltpu.roll(t, 32, 1) * sin_hi


def _in_proj_kernel(x_ref, w_ref, cosa_ref, sina_ref, cosb_ref, sinb_lo_ref, sinb_hi_ref,
                    o_ref, wb_ref):
    n = pl.program_id(0)
    m = pl.program_id(1)

    @pl.when(m == 0)
    def _():
        wb_ref[...] = w_ref[...].astype(BF16)

    a_scale = HEAD_DIM ** -0.5
    b_scale = B_QK_DIM ** -0.5 * math.log2(math.e)

    def run(epilogue):
        for r in range(PROJ_TM // ROW_CHUNK):
            rows = slice(r * ROW_CHUNK, (r + 1) * ROW_CHUNK)
            acc = jnp.dot(x_ref[rows, :], wb_ref[...], preferred_element_type=F32)
            epilogue(acc, rows)

    def head_cols(acc, rows, lo, hi, fn, scale):
        for j in range(lo, hi):
            sl = slice(j * HEAD_DIM, (j + 1) * HEAD_DIM)
            r = fn(acc[:, sl])
            if scale != 1.0:
                r = r * scale
            o_ref[rows, sl] = r.astype(o_ref.dtype)

    def plain_cols(acc, rows, lo, scale=1.0):
        t = acc[:, lo:]
        if scale != 1.0:
            t = t * scale
        o_ref[rows, lo:] = t.astype(o_ref.dtype)

    def mixer_a(acc, rows):
        cos, sin = cosa_ref[rows, :], sina_ref[rows, :]
        rope = lambda t: _rope_a(t, cos, sin)
        head_cols(acc, rows, 0, A_Q_HEADS, rope, a_scale)
        head_cols(acc, rows, A_Q_HEADS, A_Q_HEADS + A_KV_HEADS, rope, 1.0)
        plain_cols(acc, rows, (A_Q_HEADS + A_KV_HEADS) * HEAD_DIM)

    def mixer_b(acc, rows):
        cos, lo_, hi_ = cosb_ref[rows, :], sinb_lo_ref[rows, :], sinb_hi_ref[rows, :]
        rope = lambda t: _rope_b(t, cos, lo_, hi_)
        head_cols(acc, rows, 0, B_HEADS, rope, b_scale)
        head_cols(acc, rows, B_HEADS, 2 * B_HEADS, rope, 1.0)
        plain_cols(acc, rows, 2 * B_HEADS * HEAD_DIM)

    def mixer_c(acc, rows):
        o_ref[rows, :C_HEADS * HEAD_DIM] = (acc[:, :C_HEADS * HEAD_DIM] * a_scale).astype(o_ref.dtype)
        plain_cols(acc, rows, C_HEADS * HEAD_DIM)

    pl.when(n == 0)(lambda: run(mixer_a))
    pl.when(n == 1)(lambda: run(mixer_b))
    pl.when(n == 2)(lambda: run(mixer_c))


def _in_proj(xb, w_in, layer, tabs):
    tm, tn = PROJ_TM, PROJ_TN
    rope_spec = pl.BlockSpec((tm, HEAD_DIM), lambda n, m: (m, 0))
    return pl.pallas_call(
        _in_proj_kernel,
        grid=(IN_WIDTH // tn, SEQ // tm),
        in_specs=[
            pl.BlockSpec((tm, D_MODEL), lambda n, m: (m, 0)),
            pl.BlockSpec((None, D_MODEL, tn), lambda n, m: (layer, 0, n)),
            rope_spec, rope_spec, rope_spec, rope_spec, rope_spec,
        ],
        out_specs=pl.BlockSpec((tm, tn), lambda n, m: (m, n)),
        out_shape=jax.ShapeDtypeStruct((SEQ, IN_WIDTH), BF16),
        scratch_shapes=[pltpu.VMEM((D_MODEL, tn), BF16)],
        compiler_params=_params("arbitrary", "arbitrary"),
        name="in_proj",
    )(xb, w_in, *tabs)


def _attn_a_stages(sink_ref, q_ref, kp_ref, kc_ref, vp_ref, vc_ref, mask_ref, o_ref, layer):
    i = pl.program_id(0)
    has_prev = i > 0
    dn = (((1,), (1,)), ((), ()))
    hcols = lambda hd: slice(hd * HEAD_DIM, (hd + 1) * HEAD_DIM)
    scores = []
    for g in range(A_KV_HEADS):
        q = jnp.concatenate([q_ref[:, hcols(A_GROUP * g + j)] for j in range(A_GROUP)], axis=0)
        scores.append((lax.dot_general(q, kp_ref[:, hcols(g)], dn, preferred_element_type=F32),
                       lax.dot_general(q, kc_ref[:, hcols(g)], dn, preferred_element_type=F32)))

    def stage(g):
        s_prev, s_cur = scores[g]
        s_prev = jnp.where(has_prev, s_prev + mask_ref[:, :A_WINDOW], NEG_INF)
        s_cur = s_cur + mask_ref[:, A_WINDOW:]
        sink = jnp.concatenate(
            [jnp.full((A_TQ, HEAD_DIM), sink_ref[layer, A_GROUP * g + j], F32) for j in range(A_GROUP)], axis=0)
        folded = functools.reduce(
            jnp.maximum, [s_prev] + [s_cur[:, c:c + HEAD_DIM] for c in range(0, A_TQ, HEAD_DIM)])
        mx = jnp.maximum(jnp.broadcast_to(folded.max(-1, keepdims=True), sink.shape), sink)
        p_prev = jnp.exp(s_prev - mx)
        p_cur = jnp.exp(s_cur - jnp.concatenate([mx] * (A_TQ // HEAD_DIM), axis=1))
        ones = lambda rows: jnp.ones((rows, HEAD_DIM), BF16)
        v_prev = jnp.concatenate([vp_ref[:, hcols(g)], ones(A_WINDOW)], axis=1)
        v_cur = jnp.concatenate([vc_ref[:, hcols(g)], ones(A_TQ)], axis=1)
        o_ext = (jnp.dot(p_prev.astype(BF16), v_prev, preferred_element_type=F32)
                 + jnp.dot(p_cur.astype(BF16), v_cur, preferred_element_type=F32))
        denom = o_ext[:, HEAD_DIM:] + jnp.exp(sink - mx)
        o = o_ext[:, :HEAD_DIM] * (1.0 / denom)
        for j in range(A_GROUP):
            o_ref[:, hcols(A_GROUP * g + j)] = o[j * A_TQ:(j + 1) * A_TQ].astype(o_ref.dtype)

    return [functools.partial(stage, g) for g in range(A_KV_HEADS)]


def _attn_a_specs():
    tq, win = A_TQ, A_WINDOW
    kv_width = A_KV_HEADS * HEAD_DIM
    kblk, vblk = AK_OFF // kv_width, AV_OFF // kv_width
    prev = lambda i: jnp.maximum(i * (tq // win) - 1, 0)
    in_specs = [
        pl.BlockSpec(memory_space=pltpu.SMEM),
        pl.BlockSpec((tq, A_Q_HEADS * HEAD_DIM), lambda i: (i, 0)),
        pl.BlockSpec((win, kv_width), lambda i: (prev(i), kblk)),
        pl.BlockSpec((tq, kv_width), lambda i: (i, kblk)),
        pl.BlockSpec((win, kv_width), lambda i: (prev(i), vblk)),
        pl.BlockSpec((tq, kv_width), lambda i: (i, vblk)),
        pl.BlockSpec((A_GROUP * tq, win + tq), lambda i: (0, 0)),
    ]
    out_spec = pl.BlockSpec((tq, A_Q_HEADS * HEAD_DIM), lambda i: (i, 0))
    return in_specs, out_spec, jax.ShapeDtypeStruct((SEQ, A_Q_HEADS * HEAD_DIM), BF16)


def _attn_b_kernel(lam_ref, g_ref, q_ref, qn_ref, k_ref, v_ref, o_ref,
                   vt_s, qst_s, acc_s, s_s, sn_s, m_s, *, lam_init):
    i = pl.program_id(0)
    tq, tk = B_TQ, B_TK
    assert tq == tk
    hcols = lambda hd: slice(hd * HEAD_DIM, (hd + 1) * HEAD_DIM)
    cur = i % 2

    def stage_queries(src_ref, half):
        feat = lax.broadcasted_iota(jnp.int32, (HEAD_DIM, tq), 0)
        for hd in range(B_HEADS):
            qt = src_ref[:, hcols(hd)].astype(F32).T
            qst_s[half, hd, :, :tq] = jnp.where(feat < B_QK_DIM, qt, 0.0).astype(BF16)
            qst_s[half, hd, :, tq:] = jnp.where(feat >= B_QK_DIM, qt, 0.0).astype(BF16)

    def scores(j, hd, half):
        kj = k_ref[pl.ds(pl.multiple_of(j * tk, tk), tk), hcols(hd)]
        return jnp.dot(kj, qst_s[half, hd], preferred_element_type=F32)

    for hd in range(B_HEADS):
        vt_s[hd, i, :HEAD_DIM, :] = v_ref[:, hcols(hd)].astype(F32).T.astype(BF16)
        vt_s[hd, i, HEAD_DIM:, :] = jnp.ones((B_ONES_ROWS, tk), BF16)

    @pl.when(i == 0)
    def _():
        stage_queries(q_ref, 0)
        for hd in range(B_HEADS):
            s_s[0, hd] = scores(0, hd, 0)

    stage_queries(qn_ref, 1 - cur)
    acc_s[...] = jnp.zeros_like(acc_s)
    m_s[...] = jnp.full(m_s.shape, NEG_INF, F32)

    def block(j, src, dst, diagonal=False):
        for hd in range(B_HEADS):
            if diagonal:
                dst[hd] = scores(0, hd, 1 - cur)
            else:
                dst[hd] = scores(j + 1, hd, cur)
            s = src[hd]
            if diagonal:
                kc = lax.broadcasted_iota(jnp.int32, s.shape, 0) // CHUNK
                qc = (lax.broadcasted_iota(jnp.int32, s.shape, 1) % tq) // CHUNK
                s = jnp.where(kc <= qc, s, NEG_INF)
            m_old = m_s[hd]
            m_new = jnp.maximum(m_old, s.max(axis=0, keepdims=True))
            m_s[hd] = m_new
            a = jnp.exp2(m_old - m_new)
            p = jnp.exp2(s - m_new).astype(BF16)
            acc_s[hd] = a * acc_s[hd] + jnp.dot(vt_s[hd, j], p, preferred_element_type=F32)

    half0, half1 = s_s.at[0], s_s.at[1]

    @pl.when(i == 0)
    def _():
        block(0, half0, sn_s, diagonal=True)

    @pl.when(i > 0)
    def _():
        block(0, sn_s, half0)

    def pair(t, carry):
        block(2 * t + 1, half0, half1)
        block(2 * t + 2, half1, half0)
        return carry

    n_mid = jnp.maximum(i - 1, 0)
    lax.fori_loop(0, n_mid // 2, pair, 0)

    @pl.when((i > 0) & (n_mid % 2 == 0))
    def _():
        block(i, half0, sn_s, diagonal=True)

    @pl.when((i > 0) & (n_mid % 2 == 1))
    def _():
        block(i - 1, half0, half1)
        block(i, half1, sn_s, diagonal=True)

    lam = (jnp.exp(jnp.sum(lam_ref[0:1, :] * lam_ref[1:2, :], axis=-1, keepdims=True))
           - jnp.exp(jnp.sum(lam_ref[2:3, :] * lam_ref[3:4, :], axis=-1, keepdims=True))
           + lam_init)
    for hd in range(B_HEADS):
        acc = acc_s[hd]
        o_all = acc[:HEAD_DIM] * (1.0 / acc[HEAD_DIM:HEAD_DIM + 1])
        o = o_all[:, :tq] - lam * o_all[:, tq:]
        o = o * lax.rsqrt(jnp.mean(jnp.square(o), axis=0, keepdims=True) + LN_EPS)
        o = o * g_ref[...] * (1.0 - lam_init)
        o_ref[:, hcols(hd)] = o.T.astype(o_ref.dtype)


def _attn_b(h, lam_vecs, sub_g, layer, lam_init):
    tq, tk = B_TQ, B_TK
    width = B_HEADS * HEAD_DIM
    n_steps = SEQ // tq
    return pl.pallas_call(
        functools.partial(_attn_b_kernel, lam_init=lam_init),
        grid=(n_steps,),
        in_specs=[
            pl.BlockSpec((None, 4, B_QK_DIM), lambda i: (layer, 0, 0)),
            pl.BlockSpec((None, HEAD_DIM, 1), lambda i: (layer, 0, 0)),
            pl.BlockSpec((tq, width), lambda i: (i, BQ_OFF // width)),
            pl.BlockSpec((tq, width), lambda i: (jnp.minimum(i + 1, n_steps - 1), BQ_OFF // width)),
            pl.BlockSpec((SEQ, width), lambda i: (0, BK_OFF // width), pipeline_mode=pl.Buffered(1)),
            pl.BlockSpec((tk, width), lambda i: (i, BV_OFF // width)),
        ],
        out_specs=pl.BlockSpec((tq, width), lambda i: (i, 0)),
        out_shape=jax.ShapeDtypeStruct((SEQ, width), BF16),
        scratch_shapes=[
            pltpu.VMEM((B_HEADS, SEQ // tk, HEAD_DIM + B_ONES_ROWS, tk), BF16),
            pltpu.VMEM((2, B_HEADS, HEAD_DIM, 2 * tq), BF16),
            pltpu.VMEM((B_HEADS, HEAD_DIM + B_ONES_ROWS, 2 * tq), F32),
            pltpu.VMEM((2, B_HEADS, tk, 2 * tq), F32),
            pltpu.VMEM((B_HEADS, tk, 2 * tq), F32),
            pltpu.VMEM((B_HEADS, 1, 2 * tq), F32),
        ],
        compiler_params=_params("arbitrary"),
        name="attn_b",
    )(lam_vecs, sub_g, h, h, h, h)


def _attn_c_build_bias(diag_ref, bias_ref):
    nk = C_KBLOCKS * C_TQ
    qc = lax.broadcasted_iota(jnp.int32, (C_TQ, nk), 0) // CHUNK
    kc = lax.broadcasted_iota(jnp.int32, (C_TQ, nk), 1) // CHUNK
    band = (kc >= qc) & (kc <= qc + C_PREV_CHUNKS)
    for hd in range(C_HEADS):
        rows = jnp.broadcast_to(diag_ref[hd:hd + 1, :], (C_TQ, C_TQ + nk))
        skew = pltpu.roll(rows, 0, 1, stride=1, stride_axis=0)
        bias_ref[hd] = jnp.where(band, skew[:, :nk], NEG_INF)


def _attn_c_stages(q_ref, k_refs, v_refs, o_ref, bias_ref):
    i = pl.program_id(0)
    dn = (((1,), (1,)), ((), ()))
    hcols = lambda hd: slice(hd * HEAD_DIM, (hd + 1) * HEAD_DIM)
    scores = [[lax.dot_general(q_ref[:, hcols(hd)], k_refs[b][:, hcols(hd)], dn, preferred_element_type=F32)
               for b in range(C_KBLOCKS)] for hd in range(C_HEADS)]

    def stage(hd):
        hsl = hcols(hd)
        s = []
        for b in range(C_KBLOCKS):
            sb = scores[hd][b] + bias_ref[hd, :, b * C_TQ:(b + 1) * C_TQ]
            s.append(jnp.where(i - (C_KBLOCKS - 1) + b >= 0, sb, NEG_INF))
        mx = functools.reduce(jnp.maximum, [sb.max(-1, keepdims=True) for sb in s])
        p = [jnp.exp(sb - mx) for sb in s]
        ones = jnp.ones((C_TQ, HEAD_DIM), BF16)
        o_ext = functools.reduce(
            lambda a, b_: a + b_,
            [jnp.dot(p[b].astype(BF16), jnp.concatenate([v_refs[b][:, hsl], ones], axis=1),
                     preferred_element_type=F32) for b in range(C_KBLOCKS)])
        o_ref[:, hsl] = (o_ext[:, :HEAD_DIM] * (1.0 / o_ext[:, HEAD_DIM:])).astype(o_ref.dtype)

    return [functools.partial(stage, hd) for hd in range(C_HEADS)]


def _attn_ac_kernel(*refs, layer):
    n_a = 7
    a_in, c_in = refs[:n_a], refs[n_a:n_a + 2 + 2 * C_KBLOCKS]
    oa_ref, oc_ref, bias_ref = refs[n_a + 2 + 2 * C_KBLOCKS:]
    pl.when(pl.program_id(0) == 0)(lambda: _attn_c_build_bias(c_in[-1], bias_ref))
    c_stages = _attn_c_stages(c_in[0], c_in[1:1 + C_KBLOCKS], c_in[1 + C_KBLOCKS:1 + 2 * C_KBLOCKS],
                              oc_ref, bias_ref)
    a_stages = _attn_a_stages(*a_in, oa_ref, layer)
    per_a = len(c_stages) // len(a_stages)
    for g, a_stage in enumerate(a_stages):
        for c_stage in c_stages[g * per_a:(g + 1) * per_a]:
            c_stage()
        a_stage()


def _attn_ac(h, sinks, mask_a, bias_diag, layer):
    assert A_TQ == C_TQ
    tq = C_TQ
    c_width = C_HEADS * HEAD_DIM
    qb, kb, vb = CQ_OFF // c_width, CK_OFF // c_width, CV_OFF // c_width

    def kv_spec(b, col):
        return pl.BlockSpec((tq, c_width), lambda i: (jnp.maximum(i - (C_KBLOCKS - 1) + b, 0), col))

    a_in_specs, a_out_spec, a_out_shape = _attn_a_specs()
    c_in_specs = ([pl.BlockSpec((tq, c_width), lambda i: (i, qb))]
                  + [kv_spec(b, kb) for b in range(C_KBLOCKS)]
                  + [kv_spec(b, vb) for b in range(C_KBLOCKS)]
                  + [pl.BlockSpec((None, C_HEADS, (C_KBLOCKS + 1) * tq), lambda i: (layer, 0, 0))])
    return pl.pallas_call(
        functools.partial(_attn_ac_kernel, layer=layer),
        grid=(SEQ // tq,),
        in_specs=a_in_specs + c_in_specs,
        out_specs=[a_out_spec, pl.BlockSpec((tq, c_width), lambda i: (i, 0))],
        out_shape=[a_out_shape, jax.ShapeDtypeStruct((SEQ, c_width), BF16)],
        scratch_shapes=[pltpu.VMEM((C_HEADS, tq, C_KBLOCKS * tq), F32)],
        compiler_params=_params("arbitrary"),
        name="attn_ac",
    )(sinks, h, h, h, h, h, mask_a, h, *([h] * (2 * C_KBLOCKS)), bias_diag)


def _mix_kernel(x_ref, ya_ref, yb_ref, yc_ref, wga_ref, wgb_ref, wgc_ref, ba_ref, bb_ref, bc_ref,
                wa_ref, wb_ref, wc_ref, o_ref, wg_s, wa_s, wb_s, wc_s):
    m = pl.program_id(1)

    @pl.when(m == 0)
    def _():
        wg_s[0] = wga_ref[...].astype(BF16)
        wg_s[1] = wgb_ref[...].astype(BF16)
        wg_s[2] = wgc_ref[...].astype(BF16)
        wa_s[...] = wa_ref[...].astype(BF16)
        wb_s[...] = wb_ref[...].astype(BF16)
        wc_s[...] = wc_ref[...].astype(BF16)

    for r in range(MIX_TM // ROW_CHUNK):
        rows = slice(r * ROW_CHUNK, (r + 1) * ROW_CHUNK)
        x = x_ref[rows, :]

        def branch(idx, b_ref, y_ref, w_s):
            gate = jax.nn.sigmoid(jnp.dot(x, wg_s[idx], preferred_element_type=F32) + b_ref[...])
            return gate * jnp.dot(y_ref[rows, :], w_s[...], preferred_element_type=F32)

        mix = branch(0, ba_ref, ya_ref, wa_s) + branch(1, bb_ref, yb_ref, wb_s) + branch(2, bc_ref, yc_ref, wc_s)
        o_ref[rows, :] = mix.astype(o_ref.dtype)


def _mix(xb, ya, yb, yc, w_gate, b_gate, w_br_a, w_br_b, w_br_c, layer):
    tm, tn = MIX_TM, MIX_TN
    nb = D_MODEL // tn
    row = lambda width: pl.BlockSpec((tm, width), lambda n, m: (m, 0))
    gate_w = lambda k: pl.BlockSpec((None, D_MODEL, tn), lambda n, m: (layer, 0, k * nb + n))
    gate_b = lambda k: pl.BlockSpec((None, 1, tn), lambda n, m: (layer, 0, k * nb + n))
    br_w = lambda width: pl.BlockSpec((None, width, tn), lambda n, m: (layer, 0, n))
    return pl.pallas_call(
        _mix_kernel,
        grid=(nb, SEQ // tm),
        in_specs=[row(D_MODEL), row(1024), row(512), row(512),
                  gate_w(0), gate_w(1), gate_w(2), gate_b(0), gate_b(1), gate_b(2),
                  br_w(1024), br_w(512), br_w(512)],
        out_specs=pl.BlockSpec((tm, tn), lambda n, m: (m, n)),
        out_shape=jax.ShapeDtypeStruct((SEQ, D_MODEL), BF16),
        scratch_shapes=[pltpu.VMEM((3, D_MODEL, tn), BF16), pltpu.VMEM((1024, tn), BF16),
                        pltpu.VMEM((512, tn), BF16), pltpu.VMEM((512, tn), BF16)],
        compiler_params=_params("arbitrary", "arbitrary"),
        name="mix",
    )(xb, ya, yb, yc, w_gate, w_gate, w_gate, b_gate, b_gate, b_gate, w_br_a, w_br_b, w_br_c)


def _matmul_ln_kernel(y_ref, w_ref, x_ref, g_ref, b_ref, of_ref, ob_ref, *wb_scratch, chunk):
    if wb_scratch:
        wb_ref, = wb_scratch

        @pl.when(pl.program_id(0) == 0)
        def _():
            wb_ref[...] = w_ref[...].astype(BF16)
    else:
        wb_ref = w_ref
    for r in range(y_ref.shape[0] // chunk):
        rows = slice(r * chunk, (r + 1) * chunk)
        z = DEEPNORM_ALPHA * x_ref[rows, :] + jnp.dot(y_ref[rows, :], wb_ref[...], preferred_element_type=F32)
        mu = jnp.mean(z, axis=-1, keepdims=True)
        zc = z - mu
        var = jnp.mean(jnp.square(zc), axis=-1, keepdims=True)
        out = zc * lax.rsqrt(var + LN_EPS) * g_ref[...] + b_ref[...]
        of_ref[rows, :] = out
        ob_ref[rows, :] = out.astype(ob_ref.dtype)


def _matmul_ln(y, w, x, g, b, layer, tm, name):
    k = y.shape[1]
    vec = pl.BlockSpec((None, 1, D_MODEL), lambda m: (layer, 0, 0))
    if w.ndim == 3:
        w_spec = pl.BlockSpec((None, k, D_MODEL), lambda m: (layer, 0, 0), pipeline_mode=pl.Buffered(1))
        scratch = [pltpu.VMEM((k, D_MODEL), BF16)]
    else:
        w_spec = pl.BlockSpec((k, D_MODEL), lambda m: (0, 0), pipeline_mode=pl.Buffered(1))
        scratch = []
    return pl.pallas_call(
        functools.partial(_matmul_ln_kernel, chunk=LN_ROW_CHUNK),
        grid=(SEQ // tm,),
        in_specs=[
            pl.BlockSpec((tm, k), lambda m: (m, 0)),
            w_spec,
            pl.BlockSpec((tm, D_MODEL), lambda m: (m, 0)),
            vec, vec,
        ],
        out_specs=[pl.BlockSpec((tm, D_MODEL), lambda m: (m, 0)),
                   pl.BlockSpec((tm, D_MODEL), lambda m: (m, 0))],
        out_shape=[jax.ShapeDtypeStruct((SEQ, D_MODEL), F32),
                   jax.ShapeDtypeStruct((SEQ, D_MODEL), BF16)],
        scratch_shapes=scratch,
        compiler_params=_params("arbitrary"),
        name=name,
    )(y, w, x, g, b)


def _ffn_in_kernel(x_ref, wg_ref, wu_ref, wo_ref, o_ref, wo_b_ref, wg_s, wu_s):
    m = pl.program_id(1)

    @pl.when(m == 0)
    def _():
        wg_s[...] = wg_ref[...].astype(BF16)
        wu_s[...] = wu_ref[...].astype(BF16)

    @pl.when((m == FFN_WO_PHASE) | ((pl.program_id(0) == 0) & (m == 0)))
    def _():
        wo_b_ref[...] = wo_ref[...].astype(BF16)

    for r in range(FFN_TM // ROW_CHUNK):
        rows = slice(r * ROW_CHUNK, (r + 1) * ROW_CHUNK)
        x = x_ref[rows, :]
        gate = jnp.dot(x, wg_s[...], preferred_element_type=F32)
        up = jnp.dot(x, wu_s[...], preferred_element_type=F32)
        o_ref[rows, :] = (jax.nn.silu(gate) * up).astype(o_ref.dtype)


def _ffn_in(xb, w_ffn_in, w_ffn_out, layer):
    tm, tn = FFN_TM, FFN_TN
    nb = FFN_HIDDEN // tn

    def _wo_block(n, m):
        return jnp.minimum(n + (m >= FFN_WO_PHASE).astype(jnp.int32), nb - 1)

    return pl.pallas_call(
        _ffn_in_kernel,
        grid=(nb, SEQ // tm),
        in_specs=[
            pl.BlockSpec((tm, D_MODEL), lambda n, m: (m, 0)),
            pl.BlockSpec((None, D_MODEL, tn), lambda n, m: (layer, 0, n)),
            pl.BlockSpec((None, D_MODEL, tn), lambda n, m: (layer, 0, nb + n)),
            pl.BlockSpec((None, tn, D_MODEL), lambda n, m: (layer, _wo_block(n, m), 0)),
        ],
        out_specs=[pl.BlockSpec((tm, tn), lambda n, m: (m, n)),
                   pl.BlockSpec((tn, D_MODEL), lambda n, m: (_wo_block(n, m), 0))],
        out_shape=[jax.ShapeDtypeStruct((SEQ, FFN_HIDDEN), BF16),
                   jax.ShapeDtypeStruct((FFN_HIDDEN, D_MODEL), BF16)],
        scratch_shapes=[pltpu.VMEM((D_MODEL, tn), BF16), pltpu.VMEM((D_MODEL, tn), BF16)],
        compiler_params=_params("arbitrary", "arbitrary"),
        name="ffn_in",
    )(xb, w_ffn_in, w_ffn_in, w_ffn_out)


def _rope_tables():
    pos = jnp.arange(SEQ, dtype=F32)

    def cs(dim):
        inv = 1.0 / (ROPE_THETA ** (jnp.arange(0, dim, 2, dtype=F32) / dim))
        ang = pos[:, None] * inv[None, :]
        ang = jnp.concatenate([ang, ang], axis=-1)
        return jnp.cos(ang), jnp.sin(ang)

    cos_a, sin_a = cs(HEAD_DIM)
    half = HEAD_DIM // 2
    sin_a = jnp.concatenate([-sin_a[:, :half], sin_a[:, half:]], axis=-1)
    cos_b, sin_b = cs(B_QK_DIM)
    cos_b2 = jnp.concatenate([cos_b, cos_b], axis=-1)
    sin_b2 = jnp.concatenate([sin_b, sin_b], axis=-1)
    first_half = (np.arange(HEAD_DIM) % B_QK_DIM) < (B_QK_DIM // 2)
    sin_lo = jnp.where(first_half[None, :], -sin_b2, 0.0)
    sin_hi = jnp.where(first_half[None, :], 0.0, sin_b2)
    return cos_a, sin_a, cos_b2, sin_lo, sin_hi


def _mask_a():
    qc = (np.arange(A_TQ) // CHUNK)[:, None]
    kc = (np.arange(A_WINDOW + A_TQ) // CHUNK)[None, :]
    ok = (kc >= qc) & (kc <= qc + A_WINDOW // CHUNK)
    m = np.where(ok, 0.0, NEG_INF).astype(np.float32)
    return jnp.asarray(np.tile(m, (A_GROUP, 1)))


def _bias_diagonals(rel_bias):
    nq, nk = C_TQ, C_KBLOCKS * C_TQ
    rb = rel_bias.astype(F32)
    n_clip = (nk - nq) - REL_CLIP
    far = jnp.broadcast_to(rb[..., -1:], rb.shape[:-1] + (n_clip,))
    near = rb[..., REL_CLIP - (nq - 1):][..., ::-1]
    d_nonneg = jnp.concatenate([far, near], axis=-1)
    assert d_nonneg.shape[-1] == nk
    d_neg = jnp.broadcast_to(rb[..., -1:], rb.shape[:-1] + (nq,))
    return jnp.concatenate([d_nonneg, d_neg], axis=-1)


def kernel(x, w_in, sinks, lambda_q1, lambda_k1, lambda_q2, lambda_k2, diff_norm_g, rel_bias,
           w_br_a, w_br_b, w_br_c, w_gate, b_gate, w_out, ln1_g, ln1_b,
           w_ffn_in, w_ffn_out, ln2_g, ln2_b):
    assert x.shape == (1, SEQ, D_MODEL)
    tabs = _rope_tables()
    mask_a = _mask_a()
    xf = x.reshape(SEQ, D_MODEL)
    xb = xf.astype(BF16)
    lam_vecs = jnp.stack([lambda_q1, lambda_k1, lambda_q2, lambda_k2], axis=1).astype(F32)
    sub_g = diff_norm_g.astype(F32).reshape(DEPTH, HEAD_DIM, 1)
    bias_diag = _bias_diagonals(rel_bias)
    b_gate3 = b_gate.reshape(DEPTH, 1, -1)
    vec3 = lambda v: v.reshape(DEPTH, 1, D_MODEL)
    ln1_g, ln1_b, ln2_g, ln2_b = vec3(ln1_g), vec3(ln1_b), vec3(ln2_g), vec3(ln2_b)
    for l in range(DEPTH):
        lam_init = 0.8 - 0.6 * math.exp(-0.3 * l)
        h = _in_proj(xb, w_in, l, tabs)
        ya, yc = _attn_ac(h, sinks, mask_a, bias_diag, l)
        yb = _attn_b(h, lam_vecs, sub_g, l, lam_init)
        mix = _mix(xb, ya, yb, yc, w_gate, b_gate3, w_br_a, w_br_b, w_br_c, l)
        xf, xb = _matmul_ln(mix, w_out, xf, ln1_g, ln1_b, l, OUT_TM, "out_ln")
        f, w_ffn_out_b = _ffn_in(xb, w_ffn_in, w_ffn_out, l)
        xf, xb = _matmul_ln(f, w_ffn_out_b, xf, ln2_g, ln2_b, l, FFN_OUT_TM, "ffn_out")
    return xf.reshape(1, SEQ, D_MODEL)
```

```python
import functools
import math

import jax
import jax.numpy as jnp
import numpy as np
from jax import lax
from jax.experimental import pallas as pl
from jax.experimental.pallas import tpu as pltpu

D_MODEL = 2048
SEQ = 8192
DEPTH = 4
CHUNK = 64
HEAD_DIM = 128
A_Q_HEADS = 8
A_KV_HEADS = 2
A_GROUP = A_Q_HEADS // A_KV_HEADS
B_HEADS = 4
B_QK_DIM = 64
C_HEADS = 4
C_PREV_CHUNKS = 8
REL_CLIP = 256
FFN_HIDDEN = 5632
IN_WIDTH = 4608
ROPE_THETA = 10000.0
LN_EPS = 1e-5
DEEPNORM_ALPHA = (2 * DEPTH) ** 0.25
NEG_INF = -1e30
LOG2_E = math.log2(math.e)

BF16 = jnp.bfloat16
F32 = jnp.float32

VMEM_LIMIT_BYTES = 56 * 1024 * 1024

AQ_OFF, AK_OFF, AV_OFF = 0, 1024, 1280
BQ_OFF, BK_OFF, BV_OFF = 1536, 2048, 2560
CQ_OFF, CK_OFF, CV_OFF = 3072, 3584, 4096

LN_ROW_CHUNK = 128
ROW_CHUNK = 256
PROJ_TM, PROJ_TN = 1024, 1536
MIX_TM, MIX_TN = 512, 512
OUT_TM = 512
FFN_TM, FFN_TN = 1024, 512
FFN_WO_PHASE = 4
FFN_OUT_TM = 256
A_TQ = 256
A_WINDOW = 128
B_TQ = 512
B_TK = 512
B_ONES_ROWS = 16
C_TQ = 256
C_KBLOCKS = (C_PREV_CHUNKS * CHUNK) // C_TQ + 1


def _params(*sem):
    return pltpu.CompilerParams(dimension_semantics=sem, vmem_limit_bytes=VMEM_LIMIT_BYTES)


def _rope_a(t, cos, sin_signed):
    return t * cos + pltpu.roll(t, HEAD_DIM // 2, 1) * sin_signed


def _rope_b(t, cos2, sin_lo, sin_hi):
    half = B_QK_DIM // 2
    return t * cos2 + pltpu.roll(t, HEAD_DIM - half, 1) * sin_lo + pltpu.roll(t, half, 1) * sin_hi


def _in_proj_kernel(x_ref, w_ref, cosa_ref, sina_ref, cosb_ref, sinb_lo_ref, sinb_hi_ref,
                    o_ref, wb_ref):
    n = pl.program_id(0)
    m = pl.program_id(1)

    @pl.when(m == 0)
    def _():
        wb_ref[...] = w_ref[...].astype(BF16)

    a_scale = HEAD_DIM ** -0.5 * LOG2_E
    b_scale = B_QK_DIM ** -0.5 * LOG2_E

    def run(epilogue):
        for r in range(PROJ_TM // ROW_CHUNK):
            rows = slice(r * ROW_CHUNK, (r + 1) * ROW_CHUNK)
            acc = jnp.dot(x_ref[rows, :], wb_ref[...], preferred_element_type=F32)
            epilogue(acc, rows)

    def head_cols(acc, rows, lo, hi, fn, scale):
        for j in range(lo, hi):
            sl = slice(j * HEAD_DIM, (j + 1) * HEAD_DIM)
            r = fn(acc[:, sl])
            if scale != 1.0:
                r = r * scale
            o_ref[rows, sl] = r.astype(o_ref.dtype)

    def plain_cols(acc, rows, lo, scale=1.0):
        t = acc[:, lo:]
        if scale != 1.0:
            t = t * scale
        o_ref[rows, lo:] = t.astype(o_ref.dtype)

    def mixer_a(acc, rows):
        cos, sin = cosa_ref[rows, :], sina_ref[rows, :]
        rope = lambda t: _rope_a(t, cos, sin)
        head_cols(acc, rows, 0, A_Q_HEADS, rope, a_scale)
        head_cols(acc, rows, A_Q_HEADS, A_Q_HEADS + A_KV_HEADS, rope, 1.0)
        plain_cols(acc, rows, (A_Q_HEADS + A_KV_HEADS) * HEAD_DIM)

    def mixer_b(acc, rows):
        cos, lo_, hi_ = cosb_ref[rows, :], sinb_lo_ref[rows, :], sinb_hi_ref[rows, :]
        rope = lambda t: _rope_b(t, cos, lo_, hi_)
        head_cols(acc, rows, 0, B_HEADS, rope, b_scale)
        head_cols(acc, rows, B_HEADS, 2 * B_HEADS, rope, 1.0)
        plain_cols(acc, rows, 2 * B_HEADS * HEAD_DIM)

    def mixer_c(acc, rows):
        o_ref[rows, :C_HEADS * HEAD_DIM] = (acc[:, :C_HEADS * HEAD_DIM] * a_scale).astype(o_ref.dtype)
        plain_cols(acc, rows, C_HEADS * HEAD_DIM)

    pl.when(n == 0)(lambda: run(mixer_a))
    pl.when(n == 1)(lambda: run(mixer_b))
    pl.when(n == 2)(lambda: run(mixer_c))


def _in_proj(xb, w_in, layer, tabs):
    tm, tn = PROJ_TM, PROJ_TN
    rope_spec = pl.BlockSpec((tm, HEAD_DIM), lambda n, m: (m, 0))
    return pl.pallas_call(
        _in_proj_kernel,
        grid=(IN_WIDTH // tn, SEQ // tm),
        in_specs=[
            pl.BlockSpec((tm, D_MODEL), lambda n, m: (m, 0)),
            pl.BlockSpec((None, D_MODEL, tn), lambda n, m: (layer, 0, n)),
            rope_spec, rope_spec, rope_spec, rope_spec, rope_spec,
        ],
        out_specs=pl.BlockSpec((tm, tn), lambda n, m: (m, n)),
        out_shape=jax.ShapeDtypeStruct((SEQ, IN_WIDTH), BF16),
        scratch_shapes=[pltpu.VMEM((D_MODEL, tn), BF16)],
        compiler_params=_params("arbitrary", "arbitrary"),
        name="in_proj",
    )(xb, w_in, *tabs)


def _attn_a_stages(sink_ref, q_ref, kp_ref, kc_ref, vp_ref, vc_ref, mask_ref, o_ref, layer):
    i = pl.program_id(0)
    has_prev = i > 0
    dn = (((1,), (1,)), ((), ()))
    hcols = lambda hd: slice(hd * HEAD_DIM, (hd + 1) * HEAD_DIM)
    scores = []
    for g in range(A_KV_HEADS):
        q = jnp.concatenate([q_ref[:, hcols(A_GROUP * g + j)] for j in range(A_GROUP)], axis=0)
        scores.append((lax.dot_general(q, kp_ref[:, hcols(g)], dn, preferred_element_type=F32),
                       lax.dot_general(q, kc_ref[:, hcols(g)], dn, preferred_element_type=F32)))

    def stage(g):
        s_prev, s_cur = scores[g]
        s_prev = jnp.where(has_prev, s_prev + mask_ref[:, :A_WINDOW], NEG_INF)
        s_cur = s_cur + mask_ref[:, A_WINDOW:]
        sink = jnp.concatenate(
            [jnp.full((A_TQ, HEAD_DIM), sink_ref[layer, A_GROUP * g + j] * LOG2_E, F32)
             for j in range(A_GROUP)], axis=0)
        folded = functools.reduce(
            jnp.maximum, [s_prev] + [s_cur[:, c:c + HEAD_DIM] for c in range(0, A_TQ, HEAD_DIM)])
        mx = jnp.maximum(jnp.broadcast_to(folded.max(-1, keepdims=True), sink.shape), sink)
        p_prev = jnp.exp2(s_prev - mx)
        p_cur = jnp.exp2(s_cur - jnp.concatenate([mx] * (A_TQ // HEAD_DIM), axis=1))
        ones = lambda rows: jnp.ones((rows, HEAD_DIM), BF16)
        v_prev = jnp.concatenate([vp_ref[:, hcols(g)], ones(A_WINDOW)], axis=1)
        v_cur = jnp.concatenate([vc_ref[:, hcols(g)], ones(A_TQ)], axis=1)
        o_ext = (jnp.dot(p_prev.astype(BF16), v_prev, preferred_element_type=F32)
                 + jnp.dot(p_cur.astype(BF16), v_cur, preferred_element_type=F32))
        denom = o_ext[:, HEAD_DIM:] + jnp.exp2(sink - mx)
        o = o_ext[:, :HEAD_DIM] * (1.0 / denom)
        for j in range(A_GROUP):
            o_ref[:, hcols(A_GROUP * g + j)] = o[j * A_TQ:(j + 1) * A_TQ].astype(o_ref.dtype)

    return [functools.partial(stage, g) for g in range(A_KV_HEADS)]


def _attn_a_specs():
    tq, win = A_TQ, A_WINDOW
    kv_width = A_KV_HEADS * HEAD_DIM
    kblk, vblk = AK_OFF // kv_width, AV_OFF // kv_width
    prev = lambda i: jnp.maximum(i * (tq // win) - 1, 0)
    in_specs = [
        pl.BlockSpec(memory_space=pltpu.SMEM),
        pl.BlockSpec((tq, A_Q_HEADS * HEAD_DIM), lambda i: (i, 0)),
        pl.BlockSpec((win, kv_width), lambda i: (prev(i), kblk)),
        pl.BlockSpec((tq, kv_width), lambda i: (i, kblk)),
        pl.BlockSpec((win, kv_width), lambda i: (prev(i), vblk)),
        pl.BlockSpec((tq, kv_width), lambda i: (i, vblk)),
        pl.BlockSpec((A_GROUP * tq, win + tq), lambda i: (0, 0)),
    ]
    out_spec = pl.BlockSpec((tq, A_Q_HEADS * HEAD_DIM), lambda i: (i, 0))
    return in_specs, out_spec, jax.ShapeDtypeStruct((SEQ, A_Q_HEADS * HEAD_DIM), BF16)


def _attn_b_kernel(lam_ref, g_ref, q_ref, qn_ref, k_ref, v_ref, o_ref,
                   vt_s, qst_s, acc_s, s_s, sn_s, m_s, *, lam_init):
    i = pl.program_id(0)
    tq, tk = B_TQ, B_TK
    assert tq == tk
    hcols = lambda hd: slice(hd * HEAD_DIM, (hd + 1) * HEAD_DIM)
    cur = i % 2

    def stage_queries(src_ref, half):
        feat = lax.broadcasted_iota(jnp.int32, (HEAD_DIM, tq), 0)
        for hd in range(B_HEADS):
            qt = src_ref[:, hcols(hd)].astype(F32).T
            qst_s[half, hd, :, :tq] = jnp.where(feat < B_QK_DIM, qt, 0.0).astype(BF16)
            qst_s[half, hd, :, tq:] = jnp.where(feat >= B_QK_DIM, qt, 0.0).astype(BF16)

    def scores(j, hd, half):
        kj = k_ref[pl.ds(pl.multiple_of(j * tk, tk), tk), hcols(hd)]
        return jnp.dot(kj, qst_s[half, hd], preferred_element_type=F32)

    for hd in range(B_HEADS):
        vt_s[hd, i, :HEAD_DIM, :] = v_ref[:, hcols(hd)].astype(F32).T.astype(BF16)
        vt_s[hd, i, HEAD_DIM:, :] = jnp.ones((B_ONES_ROWS, tk), BF16)

    @pl.when(i == 0)
    def _():
        stage_queries(q_ref, 0)
        for hd in range(B_HEADS):
            s_s[0, hd] = scores(0, hd, 0)

    stage_queries(qn_ref, 1 - cur)
    acc_s[...] = jnp.zeros_like(acc_s)
    m_s[...] = jnp.full(m_s.shape, NEG_INF, F32)

    def block(j, src, dst, diagonal=False):
        for hd in range(B_HEADS):
            if diagonal:
                dst[hd] = scores(0, hd, 1 - cur)
            else:
                dst[hd] = scores(j + 1, hd, cur)
            s = src[hd]
            if diagonal:
                kc = lax.broadcasted_iota(jnp.int32, s.shape, 0) // CHUNK
                qc = (lax.broadcasted_iota(jnp.int32, s.shape, 1) % tq) // CHUNK
                s = jnp.where(kc <= qc, s, NEG_INF)
            m_old = m_s[hd]
            m_new = jnp.maximum(m_old, s.max(axis=0, keepdims=True))
            m_s[hd] = m_new
            a = jnp.exp2(m_old - m_new)
            p = jnp.exp2(s - m_new).astype(BF16)
            acc_s[hd] = a * acc_s[hd] + jnp.dot(vt_s[hd, j], p, preferred_element_type=F32)

    half0, half1 = s_s.at[0], s_s.at[1]

    @pl.when(i == 0)
    def _():
        block(0, half0, sn_s, diagonal=True)

    @pl.when(i > 0)
    def _():
        block(0, sn_s, half0)

    def pair(t, carry):
        block(2 * t + 1, half0, half1)
        block(2 * t + 2, half1, half0)
        return carry

    n_mid = jnp.maximum(i - 1, 0)
    lax.fori_loop(0, n_mid // 2, pair, 0)

    @pl.when((i > 0) & (n_mid % 2 == 0))
    def _():
        block(i, half0, sn_s, diagonal=True)

    @pl.when((i > 0) & (n_mid % 2 == 1))
    def _():
        block(i - 1, half0, half1)
        block(i, half1, sn_s, diagonal=True)

    lam = (jnp.exp(jnp.sum(lam_ref[0:1, :] * lam_ref[1:2, :], axis=-1, keepdims=True))
           - jnp.exp(jnp.sum(lam_ref[2:3, :] * lam_ref[3:4, :], axis=-1, keepdims=True))
           + lam_init)
    for hd in range(B_HEADS):
        acc = acc_s[hd]
        o_all = acc[:HEAD_DIM] * (1.0 / acc[HEAD_DIM:HEAD_DIM + 1])
        o = o_all[:, :tq] - lam * o_all[:, tq:]
        o = o * lax.rsqrt(jnp.mean(jnp.square(o), axis=0, keepdims=True) + LN_EPS)
        o = o * g_ref[...] * (1.0 - lam_init)
        o_ref[:, hcols(hd)] = o.T.astype(o_ref.dtype)


def _attn_b(h, lam_vecs, sub_g, layer, lam_init):
    tq, tk = B_TQ, B_TK
    width = B_HEADS * HEAD_DIM
    n_steps = SEQ // tq
    return pl.pallas_call(
        functools.partial(_attn_b_kernel, lam_init=lam_init),
        grid=(n_steps,),
        in_specs=[
            pl.BlockSpec((None, 4, B_QK_DIM), lambda i: (layer, 0, 0)),
            pl.BlockSpec((None, HEAD_DIM, 1), lambda i: (layer, 0, 0)),
            pl.BlockSpec((tq, width), lambda i: (i, BQ_OFF // width)),
            pl.BlockSpec((tq, width), lambda i: (jnp.minimum(i + 1, n_steps - 1), BQ_OFF // width)),
            pl.BlockSpec((SEQ, width), lambda i: (0, BK_OFF // width), pipeline_mode=pl.Buffered(1)),
            pl.BlockSpec((tk, width), lambda i: (i, BV_OFF // width)),
        ],
        out_specs=pl.BlockSpec((tq, width), lambda i: (i, 0)),
        out_shape=jax.ShapeDtypeStruct((SEQ, width), BF16),
        scratch_shapes=[
            pltpu.VMEM((B_HEADS, SEQ // tk, HEAD_DIM + B_ONES_ROWS, tk), BF16),
            pltpu.VMEM((2, B_HEADS, HEAD_DIM, 2 * tq), BF16),
            pltpu.VMEM((B_HEADS, HEAD_DIM + B_ONES_ROWS, 2 * tq), F32),
            pltpu.VMEM((2, B_HEADS, tk, 2 * tq), F32),
            pltpu.VMEM((B_HEADS, tk, 2 * tq), F32),
            pltpu.VMEM((B_HEADS, 1, 2 * tq), F32),
        ],
        compiler_params=_params("arbitrary"),
        name="attn_b",
    )(lam_vecs, sub_g, h, h, h, h)


def _attn_c_build_bias(diag_ref, bias_ref):
    nk = C_KBLOCKS * C_TQ
    qc = lax.broadcasted_iota(jnp.int32, (C_TQ, nk), 0) // CHUNK
    kc = lax.broadcasted_iota(jnp.int32, (C_TQ, nk), 1) // CHUNK
    band = (kc >= qc) & (kc <= qc + C_PREV_CHUNKS)
    for hd in range(C_HEADS):
        rows = jnp.broadcast_to(diag_ref[hd:hd + 1, :] * LOG2_E, (C_TQ, C_TQ + nk))
        skew = pltpu.roll(rows, 0, 1, stride=1, stride_axis=0)
        bias_ref[hd] = jnp.where(band, skew[:, :nk], NEG_INF)


def _attn_c_stages(q_ref, k_refs, v_refs, o_ref, bias_ref):
    i = pl.program_id(0)
    dn = (((1,), (1,)), ((), ()))
    hcols = lambda hd: slice(hd * HEAD_DIM, (hd + 1) * HEAD_DIM)
    scores = [[lax.dot_general(q_ref[:, hcols(hd)], k_refs[b][:, hcols(hd)], dn, preferred_element_type=F32)
               for b in range(C_KBLOCKS)] for hd in range(C_HEADS)]

    def stage(hd):
        hsl = hcols(hd)
        s = []
        for b in range(C_KBLOCKS):
            sb = scores[hd][b] + bias_ref[hd, :, b * C_TQ:(b + 1) * C_TQ]
            s.append(jnp.where(i - (C_KBLOCKS - 1) + b >= 0, sb, NEG_INF))
        mx = functools.reduce(jnp.maximum, [sb.max(-1, keepdims=True) for sb in s])
        p = [jnp.exp2(sb - mx) for sb in s]
        ones = jnp.ones((C_TQ, HEAD_DIM), BF16)
        o_ext = functools.reduce(
            lambda a, b_: a + b_,
            [jnp.dot(p[b].astype(BF16), jnp.concatenate([v_refs[b][:, hsl], ones], axis=1),
                     preferred_element_type=F32) for b in range(C_KBLOCKS)])
        o_ref[:, hsl] = (o_ext[:, :HEAD_DIM] * (1.0 / o_ext[:, HEAD_DIM:])).astype(o_ref.dtype)

    return [functools.partial(stage, hd) for hd in range(C_HEADS)]


def _attn_ac_kernel(*refs, layer):
    n_a = 7
    a_in, c_in = refs[:n_a], refs[n_a:n_a + 2 + 2 * C_KBLOCKS]
    oa_ref, oc_ref, bias_ref = refs[n_a + 2 + 2 * C_KBLOCKS:]
    pl.when(pl.program_id(0) == 0)(lambda: _attn_c_build_bias(c_in[-1], bias_ref))
    c_stages = _attn_c_stages(c_in[0], c_in[1:1 + C_KBLOCKS], c_in[1 + C_KBLOCKS:1 + 2 * C_KBLOCKS],
                              oc_ref, bias_ref)
    a_stages = _attn_a_stages(*a_in, oa_ref, layer)
    per_a = len(c_stages) // len(a_stages)
    for g, a_stage in enumerate(a_stages):
        for c_stage in c_stages[g * per_a:(g + 1) * per_a]:
            c_stage()
        a_stage()


def _attn_ac(h, sinks, mask_a, bias_diag, layer):
    assert A_TQ == C_TQ
    tq = C_TQ
    c_width = C_HEADS * HEAD_DIM
    qb, kb, vb = CQ_OFF // c_width, CK_OFF // c_width, CV_OFF // c_width

    def kv_spec(b, col):
        return pl.BlockSpec((tq, c_width), lambda i: (jnp.maximum(i - (C_KBLOCKS - 1) + b, 0), col))

    a_in_specs, a_out_spec, a_out_shape = _attn_a_specs()
    c_in_specs = ([pl.BlockSpec((tq, c_width), lambda i: (i, qb))]
                  + [kv_spec(b, kb) for b in range(C_KBLOCKS)]
                  + [kv_spec(b, vb) for b in range(C_KBLOCKS)]
                  + [pl.BlockSpec((None, C_HEADS, (C_KBLOCKS + 1) * tq), lambda i: (layer, 0, 0))])
    return pl.pallas_call(
        functools.partial(_attn_ac_kernel, layer=layer),
        grid=(SEQ // tq,),
        in_specs=a_in_specs + c_in_specs,
        out_specs=[a_out_spec, pl.BlockSpec((tq, c_width), lambda i: (i, 0))],
        out_shape=[a_out_shape, jax.ShapeDtypeStruct((SEQ, c_width), BF16)],
        scratch_shapes=[pltpu.VMEM((C_HEADS, tq, C_KBLOCKS * tq), F32)],
        compiler_params=_params("arbitrary"),
        name="attn_ac",
    )(sinks, h, h, h, h, h, mask_a, h, *([h] * (2 * C_KBLOCKS)), bias_diag)


def _mix_kernel(x_ref, ya_ref, yb_ref, yc_ref, wga_ref, wgb_ref, wgc_ref, ba_ref, bb_ref, bc_ref,
                wa_ref, wb_ref, wc_ref, o_ref, wg_s, wa_s, wb_s, wc_s):
    m = pl.program_id(1)

    @pl.when(m == 0)
    def _():
        wg_s[0] = wga_ref[...].astype(BF16)
        wg_s[1] = wgb_ref[...].astype(BF16)
        wg_s[2] = wgc_ref[...].astype(BF16)
        wa_s[...] = wa_ref[...].astype(BF16)
        wb_s[...] = wb_ref[...].astype(BF16)
        wc_s[...] = wc_ref[...].astype(BF16)

    for r in range(MIX_TM // ROW_CHUNK):
        rows = slice(r * ROW_CHUNK, (r + 1) * ROW_CHUNK)
        x = x_ref[rows, :]

        def branch(idx, b_ref, y_ref, w_s):
            gate = jax.nn.sigmoid(jnp.dot(x, wg_s[idx], preferred_element_type=F32) + b_ref[...])
            return gate * jnp.dot(y_ref[rows, :], w_s[...], preferred_element_type=F32)

        mix = branch(0, ba_ref, ya_ref, wa_s) + branch(1, bb_ref, yb_ref, wb_s) + branch(2, bc_ref, yc_ref, wc_s)
        o_ref[rows, :] = mix.astype(o_ref.dtype)


def _mix(xb, ya, yb, yc, w_gate, b_gate, w_br_a, w_br_b, w_br_c, layer):
    tm, tn = MIX_TM, MIX_TN
    nb = D_MODEL // tn
    row = lambda width: pl.BlockSpec((tm, width), lambda n, m: (m, 0))
    gate_w = lambda k: pl.BlockSpec((None, D_MODEL, tn), lambda n, m: (layer, 0, k * nb + n))
    gate_b = lambda k: pl.BlockSpec((None, 1, tn), lambda n, m: (layer, 0, k * nb + n))
    br_w = lambda width: pl.BlockSpec((None, width, tn), lambda n, m: (layer, 0, n))
    return pl.pallas_call(
        _mix_kernel,
        grid=(nb, SEQ // tm),
        in_specs=[row(D_MODEL), row(1024), row(512), row(512),
                  gate_w(0), gate_w(1), gate_w(2), gate_b(0), gate_b(1), gate_b(2),
                  br_w(1024), br_w(512), br_w(512)],
        out_specs=pl.BlockSpec((tm, tn), lambda n, m: (m, n)),
        out_shape=jax.ShapeDtypeStruct((SEQ, D_MODEL), BF16),
        scratch_shapes=[pltpu.VMEM((3, D_MODEL, tn), BF16), pltpu.VMEM((1024, tn), BF16),
                        pltpu.VMEM((512, tn), BF16), pltpu.VMEM((512, tn), BF16)],
        compiler_params=_params("arbitrary", "arbitrary"),
        name="mix",
    )(xb, ya, yb, yc, w_gate, w_gate, w_gate, b_gate, b_gate, b_gate, w_br_a, w_br_b, w_br_c)


def _matmul_ln_kernel(y_ref, w_ref, x_ref, g_ref, b_ref, of_ref, ob_ref, *wb_scratch, chunk):
    if wb_scratch:
        wb_ref, = wb_scratch

        @pl.when(pl.program_id(0) == 0)
        def _():
            wb_ref[...] = w_ref[...].astype(BF16)
    else:
        wb_ref = w_ref
    for r in range(y_ref.shape[0] // chunk):
        rows = slice(r * chunk, (r + 1) * chunk)
        z = DEEPNORM_ALPHA * x_ref[rows, :] + jnp.dot(y_ref[rows, :], wb_ref[...], preferred_element_type=F32)
        mu = jnp.mean(z, axis=-1, keepdims=True)
        zc = z - mu
        var = jnp.mean(jnp.square(zc), axis=-1, keepdims=True)
        out = zc * lax.rsqrt(var + LN_EPS) * g_ref[...] + b_ref[...]
        of_ref[rows, :] = out
        ob_ref[rows, :] = out.astype(ob_ref.dtype)


def _matmul_ln(y, w, x, g, b, layer, tm, name):
    k = y.shape[1]
    vec = pl.BlockSpec((None, 1, D_MODEL), lambda m: (layer, 0, 0))
    if w.ndim == 3:
        w_spec = pl.BlockSpec((None, k, D_MODEL), lambda m: (layer, 0, 0), pipeline_mode=pl.Buffered(1))
        scratch = [pltpu.VMEM((k, D_MODEL), BF16)]
    else:
        w_spec = pl.BlockSpec((k, D_MODEL), lambda m: (0, 0), pipeline_mode=pl.Buffered(1))
        scratch = []
    return pl.pallas_call(
        functools.partial(_matmul_ln_kernel, chunk=LN_ROW_CHUNK),
        grid=(SEQ // tm,),
        in_specs=[
            pl.BlockSpec((tm, k), lambda m: (m, 0)),
            w_spec,
            pl.BlockSpec((tm, D_MODEL), lambda m: (m, 0)),
            vec, vec,
        ],
        out_specs=[pl.BlockSpec((tm, D_MODEL), lambda m: (m, 0)),
                   pl.BlockSpec((tm, D_MODEL), lambda m: (m, 0))],
        out_shape=[jax.ShapeDtypeStruct((SEQ, D_MODEL), F32),
                   jax.ShapeDtypeStruct((SEQ, D_MODEL), BF16)],
        scratch_shapes=scratch,
        compiler_params=_params("arbitrary"),
        name=name,
    )(y, w, x, g, b)


def _ffn_in_kernel(x_ref, wg_ref, wu_ref, wo_ref, o_ref, wo_b_ref, wg_s, wu_s):
    m = pl.program_id(1)

    @pl.when(m == 0)
    def _():
        wg_s[...] = wg_ref[...].astype(BF16)
        wu_s[...] = wu_ref[...].astype(BF16)

    @pl.when((m == FFN_WO_PHASE) | ((pl.program_id(0) == 0) & (m == 0)))
    def _():
        wo_b_ref[...] = wo_ref[...].astype(BF16)

    for r in range(FFN_TM // ROW_CHUNK):
        rows = slice(r * ROW_CHUNK, (r + 1) * ROW_CHUNK)
        x = x_ref[rows, :]
        gate = jnp.dot(x, wg_s[...], preferred_element_type=F32)
        up = jnp.dot(x, wu_s[...], preferred_element_type=F32)
        o_ref[rows, :] = (jax.nn.silu(gate) * up).astype(o_ref.dtype)


def _ffn_in(xb, w_ffn_in, w_ffn_out, layer):
    tm, tn = FFN_TM, FFN_TN
    nb = FFN_HIDDEN // tn

    def _wo_block(n, m):
        return jnp.minimum(n + (m >= FFN_WO_PHASE).astype(jnp.int32), nb - 1)

    return pl.pallas_call(
        _ffn_in_kernel,
        grid=(nb, SEQ // tm),
        in_specs=[
            pl.BlockSpec((tm, D_MODEL), lambda n, m: (m, 0)),
            pl.BlockSpec((None, D_MODEL, tn), lambda n, m: (layer, 0, n)),
            pl.BlockSpec((None, D_MODEL, tn), lambda n, m: (layer, 0, nb + n)),
            pl.BlockSpec((None, tn, D_MODEL), lambda n, m: (layer, _wo_block(n, m), 0)),
        ],
        out_specs=[pl.BlockSpec((tm, tn), lambda n, m: (m, n)),
                   pl.BlockSpec((tn, D_MODEL), lambda n, m: (_wo_block(n, m), 0))],
        out_shape=[jax.ShapeDtypeStruct((SEQ, FFN_HIDDEN), BF16),
                   jax.ShapeDtypeStruct((FFN_HIDDEN, D_MODEL), BF16)],
        scratch_shapes=[pltpu.VMEM((D_MODEL, tn), BF16), pltpu.VMEM((D_MODEL, tn), BF16)],
        compiler_params=_params("arbitrary", "arbitrary"),
        name="ffn_in",
    )(xb, w_ffn_in, w_ffn_in, w_ffn_out)


def _rope_tables():
    pos = jnp.arange(SEQ, dtype=F32)

    def cs(dim):
        inv = 1.0 / (ROPE_THETA ** (jnp.arange(0, dim, 2, dtype=F32) / dim))
        ang = pos[:, None] * inv[None, :]
        ang = jnp.concatenate([ang, ang], axis=-1)
        return jnp.cos(ang), jnp.sin(ang)

    cos_a, sin_a = cs(HEAD_DIM)
    half = HEAD_DIM // 2
    sin_a = jnp.concatenate([-sin_a[:, :half], sin_a[:, half:]], axis=-1)
    cos_b, sin_b = cs(B_QK_DIM)
    cos_b2 = jnp.concatenate([cos_b, cos_b], axis=-1)
    sin_b2 = jnp.concatenate([sin_b, sin_b], axis=-1)
    first_half = (np.arange(HEAD_DIM) % B_QK_DIM) < (B_QK_DIM // 2)
    sin_lo = jnp.where(first_half[None, :], -sin_b2, 0.0)
    sin_hi = jnp.where(first_half[None, :], 0.0, sin_b2)
    return cos_a, sin_a, cos_b2, sin_lo, sin_hi


def _mask_a():
    qc = (np.arange(A_TQ) // CHUNK)[:, None]
    kc = (np.arange(A_WINDOW + A_TQ) // CHUNK)[None, :]
    ok = (kc >= qc) & (kc <= qc + A_WINDOW // CHUNK)
    m = np.where(ok, 0.0, NEG_INF).astype(np.float32)
    return jnp.asarray(np.tile(m, (A_GROUP, 1)))


def _bias_diagonals(rel_bias):
    nq, nk = C_TQ, C_KBLOCKS * C_TQ
    rb = rel_bias.astype(F32)
    n_clip = (nk - nq) - REL_CLIP
    far = jnp.broadcast_to(rb[..., -1:], rb.shape[:-1] + (n_clip,))
    near = rb[..., REL_CLIP - (nq - 1):][..., ::-1]
    d_nonneg = jnp.concatenate([far, near], axis=-1)
    assert d_nonneg.shape[-1] == nk
    d_neg = jnp.broadcast_to(rb[..., -1:], rb.shape[:-1] + (nq,))
    return jnp.concatenate([d_nonneg, d_neg], axis=-1)


def kernel(x, w_in, sinks, lambda_q1, lambda_k1, lambda_q2, lambda_k2, diff_norm_g, rel_bias,
           w_br_a, w_br_b, w_br_c, w_gate, b_gate, w_out, ln1_g, ln1_b,
           w_ffn_in, w_ffn_out, ln2_g, ln2_b):
    assert x.shape == (1, SEQ, D_MODEL)
    tabs = _rope_tables()
    mask_a = _mask_a()
    xf = x.reshape(SEQ, D_MODEL)
    xb = xf.astype(BF16)
    lam_vecs = jnp.stack([lambda_q1, lambda_k1, lambda_q2, lambda_k2], axis=1).astype(F32)
    sub_g = diff_norm_g.astype(F32).reshape(DEPTH, HEAD_DIM, 1)
    bias_diag = _bias_diagonals(rel_bias)
    b_gate3 = b_gate.reshape(DEPTH, 1, -1)
    vec3 = lambda v: v.reshape(DEPTH, 1, D_MODEL)
    ln1_g, ln1_b, ln2_g, ln2_b = vec3(ln1_g), vec3(ln1_b), vec3(ln2_g), vec3(ln2_b)
    for l in range(DEPTH):
        lam_init = 0.8 - 0.6 * math.exp(-0.3 * l)
        h = _in_proj(xb, w_in, l, tabs)
        ya, yc = _attn_ac(h, sinks, mask_a, bias_diag, l)
        yb = _attn_b(h, lam_vecs, sub_g, l, lam_init)
        mix = _mix(xb, ya, yb, yc, w_gate, b_gate3, w_br_a, w_br_b, w_br_c, l)
        xf, xb = _matmul_ln(mix, w_out, xf, ln1_g, ln1_b, l, OUT_TM, "out_ln")
        f, w_ffn_out_b = _ffn_in(xb, w_ffn_in, w_ffn_out, l)
        xf, xb = _matmul_ln(f, w_ffn_out_b, xf, ln2_g, ln2_b, l, FFN_OUT_TM, "ffn_out")
    return xf.reshape(1, SEQ, D_MODEL)
```

```python
import functools
import math

import jax
import jax.numpy as jnp
import numpy as np
from jax import lax
from jax.experimental import pallas as pl
from jax.experimental.pallas import tpu as pltpu

D_MODEL = 2048
SEQ = 8192
DEPTH = 4
CHUNK = 64
HEAD_DIM = 128
A_Q_HEADS = 8
A_KV_HEADS = 2
A_GROUP = A_Q_HEADS // A_KV_HEADS
B_HEADS = 4
B_QK_DIM = 64
C_HEADS = 4
C_PREV_CHUNKS = 8
REL_CLIP = 256
FFN_HIDDEN = 5632
IN_WIDTH = 4608
ROPE_THETA = 10000.0
LN_EPS = 1e-5
DEEPNORM_ALPHA = (2 * DEPTH) ** 0.25
NEG_INF = -1e30
LOG2_E = math.log2(math.e)

BF16 = jnp.bfloat16
F32 = jnp.float32

VMEM_LIMIT_BYTES = 58 * 1024 * 1024

AQ_OFF, AK_OFF, AV_OFF = 0, 1024, 1280
BQ_OFF, BK_OFF, BV_OFF = 1536, 2048, 2560
CQ_OFF, CK_OFF, CV_OFF = 3072, 3584, 4096

LN_ROW_CHUNK = 128
ROW_CHUNK = 256
PROJ_TM, PROJ_TN = 1024, 1536
MIX_TM, MIX_TN = 512, 512
OUT_TM = 512
FFN_TM, FFN_TN = 1024, 512
FFN_WO_PHASE = 4
FFN_OUT_TM = 512
A_TQ = 256
A_WINDOW = 128
B_TQ = 512
B_TK = 512
B_ONES_ROWS = 16
C_TQ = 256
C_KBLOCKS = (C_PREV_CHUNKS * CHUNK) // C_TQ + 1


def _params(*sem):
    return pltpu.CompilerParams(dimension_semantics=sem, vmem_limit_bytes=VMEM_LIMIT_BYTES)


def _rope_a(t, cos, sin_signed):
    return t * cos + pltpu.roll(t, HEAD_DIM // 2, 1) * sin_signed


def _rope_b(t, cos2, sin_lo, sin_hi):
    half = B_QK_DIM // 2
    return t * cos2 + pltpu.roll(t, HEAD_DIM - half, 1) * sin_lo + pltpu.roll(t, half, 1) * sin_hi


def _in_proj_kernel(x_ref, w_ref, cosa_ref, sina_ref, cosb_ref, sinb_lo_ref, sinb_hi_ref,
                    o_ref, wb_ref):
    n = pl.program_id(0)
    m = pl.program_id(1)

    @pl.when(m == 0)
    def _():
        wb_ref[...] = w_ref[...].astype(BF16)

    a_scale = HEAD_DIM ** -0.5 * LOG2_E
    b_scale = B_QK_DIM ** -0.5 * LOG2_E

    def run(epilogue):
        for r in range(PROJ_TM // ROW_CHUNK):
            rows = slice(r * ROW_CHUNK, (r + 1) * ROW_CHUNK)
            acc = jnp.dot(x_ref[rows, :], wb_ref[...], preferred_element_type=F32)
            epilogue(acc, rows)

    def head_cols(acc, rows, lo, hi, fn, scale):
        for j in range(lo, hi):
            sl = slice(j * HEAD_DIM, (j + 1) * HEAD_DIM)
            r = fn(acc[:, sl])
            if scale != 1.0:
                r = r * scale
            o_ref[rows, sl] = r.astype(o_ref.dtype)

    def plain_cols(acc, rows, lo, scale=1.0):
        t = acc[:, lo:]
        if scale != 1.0:
            t = t * scale
        o_ref[rows, lo:] = t.astype(o_ref.dtype)

    def mixer_a(acc, rows):
        cos, sin = cosa_ref[rows, :], sina_ref[rows, :]
        rope = lambda t: _rope_a(t, cos, sin)
        head_cols(acc, rows, 0, A_Q_HEADS, rope, a_scale)
        head_cols(acc, rows, A_Q_HEADS, A_Q_HEADS + A_KV_HEADS, rope, 1.0)
        plain_cols(acc, rows, (A_Q_HEADS + A_KV_HEADS) * HEAD_DIM)

    def mixer_b(acc, rows):
        cos, lo_, hi_ = cosb_ref[rows, :], sinb_lo_ref[rows, :], sinb_hi_ref[rows, :]
        rope = lambda t: _rope_b(t, cos, lo_, hi_)
        head_cols(acc, rows, 0, B_HEADS, rope, b_scale)
        head_cols(acc, rows, B_HEADS, 2 * B_HEADS, rope, 1.0)
        plain_cols(acc, rows, 2 * B_HEADS * HEAD_DIM)

    def mixer_c(acc, rows):
        o_ref[rows, :C_HEADS * HEAD_DIM] = (acc[:, :C_HEADS * HEAD_DIM] * a_scale).astype(o_ref.dtype)
        plain_cols(acc, rows, C_HEADS * HEAD_DIM)

    pl.when(n == 0)(lambda: run(mixer_a))
    pl.when(n == 1)(lambda: run(mixer_b))
    pl.when(n == 2)(lambda: run(mixer_c))


def _in_proj(xb, w_in, layer, tabs):
    tm, tn = PROJ_TM, PROJ_TN
    rope_spec = pl.BlockSpec((tm, HEAD_DIM), lambda n, m: (m, 0))
    return pl.pallas_call(
        _in_proj_kernel,
        grid=(IN_WIDTH // tn, SEQ // tm),
        in_specs=[
            pl.BlockSpec((tm, D_MODEL), lambda n, m: (m, 0)),
            pl.BlockSpec((None, D_MODEL, tn), lambda n, m: (layer, 0, n)),
            rope_spec, rope_spec, rope_spec, rope_spec, rope_spec,
        ],
        out_specs=pl.BlockSpec((tm, tn), lambda n, m: (m, n)),
        out_shape=jax.ShapeDtypeStruct((SEQ, IN_WIDTH), BF16),
        scratch_shapes=[pltpu.VMEM((D_MODEL, tn), BF16)],
        compiler_params=_params("arbitrary", "arbitrary"),
        name="in_proj",
    )(xb, w_in, *tabs)


def _attn_a_stages(sink_ref, q_ref, kp_ref, kc_ref, vp_ref, vc_ref, mask_ref, o_ref, layer):
    i = pl.program_id(0)
    has_prev = i > 0
    dn = (((1,), (1,)), ((), ()))
    hcols = lambda hd: slice(hd * HEAD_DIM, (hd + 1) * HEAD_DIM)
    scores = []
    for g in range(A_KV_HEADS):
        q = jnp.concatenate([q_ref[:, hcols(A_GROUP * g + j)] for j in range(A_GROUP)], axis=0)
        scores.append((lax.dot_general(q, kp_ref[:, hcols(g)], dn, preferred_element_type=F32),
                       lax.dot_general(q, kc_ref[:, hcols(g)], dn, preferred_element_type=F32)))

    def stage(g):
        s_prev, s_cur = scores[g]
        s_prev = jnp.where(has_prev, s_prev + mask_ref[:, :A_WINDOW], NEG_INF)
        s_cur = s_cur + mask_ref[:, A_WINDOW:]
        sink = jnp.concatenate(
            [jnp.full((A_TQ, HEAD_DIM), sink_ref[layer, A_GROUP * g + j] * LOG2_E, F32)
             for j in range(A_GROUP)], axis=0)
        folded = functools.reduce(
            jnp.maximum, [s_prev] + [s_cur[:, c:c + HEAD_DIM] for c in range(0, A_TQ, HEAD_DIM)])
        mx = jnp.maximum(jnp.broadcast_to(folded.max(-1, keepdims=True), sink.shape), sink)
        p_prev = jnp.exp2(s_prev - mx)
        p_cur = jnp.exp2(s_cur - jnp.concatenate([mx] * (A_TQ // HEAD_DIM), axis=1))
        ones = lambda rows: jnp.ones((rows, HEAD_DIM), BF16)
        v_prev = jnp.concatenate([vp_ref[:, hcols(g)], ones(A_WINDOW)], axis=1)
        v_cur = jnp.concatenate([vc_ref[:, hcols(g)], ones(A_TQ)], axis=1)
        o_ext = (jnp.dot(p_prev.astype(BF16), v_prev, preferred_element_type=F32)
                 + jnp.dot(p_cur.astype(BF16), v_cur, preferred_element_type=F32))
        denom = o_ext[:, HEAD_DIM:] + jnp.exp2(sink - mx)
        o = o_ext[:, :HEAD_DIM] * (1.0 / denom)
        for j in range(A_GROUP):
            o_ref[:, hcols(A_GROUP * g + j)] = o[j * A_TQ:(j + 1) * A_TQ].astype(o_ref.dtype)

    return [functools.partial(stage, g) for g in range(A_KV_HEADS)]


def _attn_a_specs():
    tq, win = A_TQ, A_WINDOW
    kv_width = A_KV_HEADS * HEAD_DIM
    kblk, vblk = AK_OFF // kv_width, AV_OFF // kv_width
    prev = lambda i: jnp.maximum(i * (tq // win) - 1, 0)
    in_specs = [
        pl.BlockSpec(memory_space=pltpu.SMEM),
        pl.BlockSpec((tq, A_Q_HEADS * HEAD_DIM), lambda i: (i, 0)),
        pl.BlockSpec((win, kv_width), lambda i: (prev(i), kblk)),
        pl.BlockSpec((tq, kv_width), lambda i: (i, kblk)),
        pl.BlockSpec((win, kv_width), lambda i: (prev(i), vblk)),
        pl.BlockSpec((tq, kv_width), lambda i: (i, vblk)),
        pl.BlockSpec((A_GROUP * tq, win + tq), lambda i: (0, 0)),
    ]
    out_spec = pl.BlockSpec((tq, A_Q_HEADS * HEAD_DIM), lambda i: (i, 0))
    return in_specs, out_spec, jax.ShapeDtypeStruct((SEQ, A_Q_HEADS * HEAD_DIM), BF16)


def _attn_b_kernel(lam_ref, g_ref, q_ref, qn_ref, k_ref, v_ref, o_ref,
                   vt_s, qst_s, acc_s, s_s, sn_s, m_s, *, lam_init):
    i = pl.program_id(0)
    tq, tk = B_TQ, B_TK
    assert tq == tk
    hcols = lambda hd: slice(hd * HEAD_DIM, (hd + 1) * HEAD_DIM)
    cur = i % 2

    def stage_queries(src_ref, half):
        feat = lax.broadcasted_iota(jnp.int32, (HEAD_DIM, tq), 0)
        for hd in range(B_HEADS):
            qt = src_ref[:, hcols(hd)].astype(F32).T
            qst_s[half, hd, :, :tq] = jnp.where(feat < B_QK_DIM, qt, 0.0).astype(BF16)
            qst_s[half, hd, :, tq:] = jnp.where(feat >= B_QK_DIM, qt, 0.0).astype(BF16)

    def scores(j, hd, half):
        kj = k_ref[pl.ds(pl.multiple_of(j * tk, tk), tk), hcols(hd)]
        return jnp.dot(kj, qst_s[half, hd], preferred_element_type=F32)

    for hd in range(B_HEADS):
        vt_s[hd, i, :HEAD_DIM, :] = v_ref[:, hcols(hd)].astype(F32).T.astype(BF16)
        vt_s[hd, i, HEAD_DIM:, :] = jnp.ones((B_ONES_ROWS, tk), BF16)

    @pl.when(i == 0)
    def _():
        stage_queries(q_ref, 0)
        for hd in range(B_HEADS):
            s_s[0, hd] = scores(0, hd, 0)

    stage_queries(qn_ref, 1 - cur)
    acc_s[...] = jnp.zeros_like(acc_s)
    m_s[...] = jnp.full(m_s.shape, NEG_INF, F32)

    def block(j, src, dst, diagonal=False):
        for hd in range(B_HEADS):
            if diagonal:
                dst[hd] = scores(0, hd, 1 - cur)
            else:
                dst[hd] = scores(j + 1, hd, cur)
            s = src[hd]
            if diagonal:
                kc = lax.broadcasted_iota(jnp.int32, s.shape, 0) // CHUNK
                qc = (lax.broadcasted_iota(jnp.int32, s.shape, 1) % tq) // CHUNK
                s = jnp.where(kc <= qc, s, NEG_INF)
            m_old = m_s[hd]
            m_new = jnp.maximum(m_old, s.max(axis=0, keepdims=True))
            m_s[hd] = m_new
            a = jnp.exp2(m_old - m_new)
            p = jnp.exp2(s - m_new).astype(BF16)
            acc_s[hd] = a * acc_s[hd] + jnp.dot(vt_s[hd, j], p, preferred_element_type=F32)

    half0, half1 = s_s.at[0], s_s.at[1]

    @pl.when(i == 0)
    def _():
        block(0, half0, sn_s, diagonal=True)

    @pl.when(i > 0)
    def _():
        block(0, sn_s, half0)

    def pair(t, carry):
        block(2 * t + 1, half0, half1)
        block(2 * t + 2, half1, half0)
        return carry

    n_mid = jnp.maximum(i - 1, 0)
    lax.fori_loop(0, n_mid // 2, pair, 0)

    @pl.when((i > 0) & (n_mid % 2 == 0))
    def _():
        block(i, half0, sn_s, diagonal=True)

    @pl.when((i > 0) & (n_mid % 2 == 1))
    def _():
        block(i - 1, half0, half1)
        block(i, half1, sn_s, diagonal=True)

    lam = (jnp.exp(jnp.sum(lam_ref[0:1, :] * lam_ref[1:2, :], axis=-1, keepdims=True))
           - jnp.exp(jnp.sum(lam_ref[2:3, :] * lam_ref[3:4, :], axis=-1, keepdims=True))
           + lam_init)
    for hd in range(B_HEADS):
        acc = acc_s[hd]
        o_all = acc[:HEAD_DIM] * (1.0 / acc[HEAD_DIM:HEAD_DIM + 1])
        o = o_all[:, :tq] - lam * o_all[:, tq:]
        o = o * lax.rsqrt(jnp.mean(jnp.square(o), axis=0, keepdims=True) + LN_EPS)
        o = o * g_ref[...] * (1.0 - lam_init)
        o_ref[:, hcols(hd)] = o.T.astype(o_ref.dtype)


def _attn_b(h, lam_vecs, sub_g, layer, lam_init):
    tq, tk = B_TQ, B_TK
    width = B_HEADS * HEAD_DIM
    n_steps = SEQ // tq
    return pl.pallas_call(
        functools.partial(_attn_b_kernel, lam_init=lam_init),
        grid=(n_steps,),
        in_specs=[
            pl.BlockSpec((None, 4, B_QK_DIM), lambda i: (layer, 0, 0)),
            pl.BlockSpec((None, HEAD_DIM, 1), lambda i: (layer, 0, 0)),
            pl.BlockSpec((tq, width), lambda i: (i, BQ_OFF // width)),
            pl.BlockSpec((tq, width), lambda i: (jnp.minimum(i + 1, n_steps - 1), BQ_OFF // width)),
            pl.BlockSpec((SEQ, width), lambda i: (0, BK_OFF // width), pipeline_mode=pl.Buffered(1)),
            pl.BlockSpec((tk, width), lambda i: (i, BV_OFF // width)),
        ],
        out_specs=pl.BlockSpec((tq, width), lambda i: (i, 0)),
        out_shape=jax.ShapeDtypeStruct((SEQ, width), BF16),
        scratch_shapes=[
            pltpu.VMEM((B_HEADS, SEQ // tk, HEAD_DIM + B_ONES_ROWS, tk), BF16),
            pltpu.VMEM((2, B_HEADS, HEAD_DIM, 2 * tq), BF16),
            pltpu.VMEM((B_HEADS, HEAD_DIM + B_ONES_ROWS, 2 * tq), F32),
            pltpu.VMEM((2, B_HEADS, tk, 2 * tq), F32),
            pltpu.VMEM((B_HEADS, tk, 2 * tq), F32),
            pltpu.VMEM((B_HEADS, 1, 2 * tq), F32),
        ],
        compiler_params=_params("arbitrary"),
        name="attn_b",
    )(lam_vecs, sub_g, h, h, h, h)


def _attn_c_build_bias(diag_ref, bias_ref):
    nk = C_KBLOCKS * C_TQ
    qc = lax.broadcasted_iota(jnp.int32, (C_TQ, nk), 0) // CHUNK
    kc = lax.broadcasted_iota(jnp.int32, (C_TQ, nk), 1) // CHUNK
    band = (kc >= qc) & (kc <= qc + C_PREV_CHUNKS)
    for hd in range(C_HEADS):
        rows = jnp.broadcast_to(diag_ref[hd:hd + 1, :] * LOG2_E, (C_TQ, C_TQ + nk))
        skew = pltpu.roll(rows, 0, 1, stride=1, stride_axis=0)
        bias_ref[hd] = jnp.where(band, skew[:, :nk], NEG_INF)


def _attn_c_stages(q_ref, k_refs, v_refs, o_ref, bias_ref):
    i = pl.program_id(0)
    dn = (((1,), (1,)), ((), ()))
    hcols = lambda hd: slice(hd * HEAD_DIM, (hd + 1) * HEAD_DIM)
    scores = [[lax.dot_general(q_ref[:, hcols(hd)], k_refs[b][:, hcols(hd)], dn, preferred_element_type=F32)
               for b in range(C_KBLOCKS)] for hd in range(C_HEADS)]

    def stage(hd):
        hsl = hcols(hd)
        s = []
        for b in range(C_KBLOCKS):
            sb = scores[hd][b] + bias_ref[hd, :, b * C_TQ:(b + 1) * C_TQ]
            s.append(jnp.where(i - (C_KBLOCKS - 1) + b >= 0, sb, NEG_INF))
        mx = functools.reduce(jnp.maximum, [sb.max(-1, keepdims=True) for sb in s])
        p = [jnp.exp2(sb - mx) for sb in s]
        ones = jnp.ones((C_TQ, HEAD_DIM), BF16)
        o_ext = functools.reduce(
            lambda a, b_: a + b_,
            [jnp.dot(p[b].astype(BF16), jnp.concatenate([v_refs[b][:, hsl], ones], axis=1),
                     preferred_element_type=F32) for b in range(C_KBLOCKS)])
        o_ref[:, hsl] = (o_ext[:, :HEAD_DIM] * (1.0 / o_ext[:, HEAD_DIM:])).astype(o_ref.dtype)

    return [functools.partial(stage, hd) for hd in range(C_HEADS)]


def _attn_ac_kernel(*refs, layer):
    n_a = 7
    a_in, c_in = refs[:n_a], refs[n_a:n_a + 2 + 2 * C_KBLOCKS]
    oa_ref, oc_ref, bias_ref = refs[n_a + 2 + 2 * C_KBLOCKS:]
    pl.when(pl.program_id(0) == 0)(lambda: _attn_c_build_bias(c_in[-1], bias_ref))
    c_stages = _attn_c_stages(c_in[0], c_in[1:1 + C_KBLOCKS], c_in[1 + C_KBLOCKS:1 + 2 * C_KBLOCKS],
                              oc_ref, bias_ref)
    a_stages = _attn_a_stages(*a_in, oa_ref, layer)
    per_a = len(c_stages) // len(a_stages)
    for g, a_stage in enumerate(a_stages):
        for c_stage in c_stages[g * per_a:(g + 1) * per_a]:
            c_stage()
        a_stage()


def _attn_ac(h, sinks, mask_a, bias_diag, layer):
    assert A_TQ == C_TQ
    tq = C_TQ
    c_width = C_HEADS * HEAD_DIM
    qb, kb, vb = CQ_OFF // c_width, CK_OFF // c_width, CV_OFF // c_width

    def kv_spec(b, col):
        return pl.BlockSpec((tq, c_width), lambda i: (jnp.maximum(i - (C_KBLOCKS - 1) + b, 0), col))

    a_in_specs, a_out_spec, a_out_shape = _attn_a_specs()
    c_in_specs = ([pl.BlockSpec((tq, c_width), lambda i: (i, qb))]
                  + [kv_spec(b, kb) for b in range(C_KBLOCKS)]
                  + [kv_spec(b, vb) for b in range(C_KBLOCKS)]
                  + [pl.BlockSpec((None, C_HEADS, (C_KBLOCKS + 1) * tq), lambda i: (layer, 0, 0))])
    return pl.pallas_call(
        functools.partial(_attn_ac_kernel, layer=layer),
        grid=(SEQ // tq,),
        in_specs=a_in_specs + c_in_specs,
        out_specs=[a_out_spec, pl.BlockSpec((tq, c_width), lambda i: (i, 0))],
        out_shape=[a_out_shape, jax.ShapeDtypeStruct((SEQ, c_width), BF16)],
        scratch_shapes=[pltpu.VMEM((C_HEADS, tq, C_KBLOCKS * tq), F32)],
        compiler_params=_params("arbitrary"),
        name="attn_ac",
    )(sinks, h, h, h, h, h, mask_a, h, *([h] * (2 * C_KBLOCKS)), bias_diag)


def _mix_kernel(x_ref, ya_ref, yb_ref, yc_ref, wga_ref, wgb_ref, wgc_ref, ba_ref, bb_ref, bc_ref,
                wa_ref, wb_ref, wc_ref, o_ref, wg_s, wa_s, wb_s, wc_s):
    m = pl.program_id(1)

    @pl.when(m == 0)
    def _():
        wg_s[0] = wga_ref[...].astype(BF16)
        wg_s[1] = wgb_ref[...].astype(BF16)
        wg_s[2] = wgc_ref[...].astype(BF16)
        wa_s[...] = wa_ref[...].astype(BF16)
        wb_s[...] = wb_ref[...].astype(BF16)
        wc_s[...] = wc_ref[...].astype(BF16)

    for r in range(MIX_TM // ROW_CHUNK):
        rows = slice(r * ROW_CHUNK, (r + 1) * ROW_CHUNK)
        x = x_ref[rows, :]

        def branch(idx, b_ref, y_ref, w_s):
            gate = jax.nn.sigmoid(jnp.dot(x, wg_s[idx], preferred_element_type=F32) + b_ref[...])
            return gate * jnp.dot(y_ref[rows, :], w_s[...], preferred_element_type=F32)

        mix = branch(0, ba_ref, ya_ref, wa_s) + branch(1, bb_ref, yb_ref, wb_s) + branch(2, bc_ref, yc_ref, wc_s)
        o_ref[rows, :] = mix.astype(o_ref.dtype)


def _mix(xb, ya, yb, yc, w_gate, b_gate, w_br_a, w_br_b, w_br_c, layer):
    tm, tn = MIX_TM, MIX_TN
    nb = D_MODEL // tn
    row = lambda width: pl.BlockSpec((tm, width), lambda n, m: (m, 0))
    gate_w = lambda k: pl.BlockSpec((None, D_MODEL, tn), lambda n, m: (layer, 0, k * nb + n))
    gate_b = lambda k: pl.BlockSpec((None, 1, tn), lambda n, m: (layer, 0, k * nb + n))
    br_w = lambda width: pl.BlockSpec((None, width, tn), lambda n, m: (layer, 0, n))
    return pl.pallas_call(
        _mix_kernel,
        grid=(nb, SEQ // tm),
        in_specs=[row(D_MODEL), row(1024), row(512), row(512),
                  gate_w(0), gate_w(1), gate_w(2), gate_b(0), gate_b(1), gate_b(2),
                  br_w(1024), br_w(512), br_w(512)],
        out_specs=pl.BlockSpec((tm, tn), lambda n, m: (m, n)),
        out_shape=jax.ShapeDtypeStruct((SEQ, D_MODEL), BF16),
        scratch_shapes=[pltpu.VMEM((3, D_MODEL, tn), BF16), pltpu.VMEM((1024, tn), BF16),
                        pltpu.VMEM((512, tn), BF16), pltpu.VMEM((512, tn), BF16)],
        compiler_params=_params("arbitrary", "arbitrary"),
        name="mix",
    )(xb, ya, yb, yc, w_gate, w_gate, w_gate, b_gate, b_gate, b_gate, w_br_a, w_br_b, w_br_c)


def _matmul_ln_kernel(y_ref, w_ref, x_ref, g_ref, b_ref, of_ref, ob_ref, *wb_scratch, chunk):
    if wb_scratch:
        wb_ref, = wb_scratch

        @pl.when(pl.program_id(0) == 0)
        def _():
            wb_ref[...] = w_ref[...].astype(BF16)
    else:
        wb_ref = w_ref
    for r in range(y_ref.shape[0] // chunk):
        rows = slice(r * chunk, (r + 1) * chunk)
        z = DEEPNORM_ALPHA * x_ref[rows, :] + jnp.dot(y_ref[rows, :], wb_ref[...], preferred_element_type=F32)
        mu = jnp.mean(z, axis=-1, keepdims=True)
        zc = z - mu
        var = jnp.mean(jnp.square(zc), axis=-1, keepdims=True)
        out = zc * lax.rsqrt(var + LN_EPS) * g_ref[...] + b_ref[...]
        of_ref[rows, :] = out
        ob_ref[rows, :] = out.astype(ob_ref.dtype)


def _matmul_ln(y, w, x, g, b, layer, tm, name):
    k = y.shape[1]
    vec = pl.BlockSpec((None, 1, D_MODEL), lambda m: (layer, 0, 0))
    if w.ndim == 3:
        w_spec = pl.BlockSpec((None, k, D_MODEL), lambda m: (layer, 0, 0), pipeline_mode=pl.Buffered(1))
        scratch = [pltpu.VMEM((k, D_MODEL), BF16)]
    else:
        w_spec = pl.BlockSpec((k, D_MODEL), lambda m: (0, 0), pipeline_mode=pl.Buffered(1))
        scratch = []
    return pl.pallas_call(
        functools.partial(_matmul_ln_kernel, chunk=LN_ROW_CHUNK),
        grid=(SEQ // tm,),
        in_specs=[
            pl.BlockSpec((tm, k), lambda m: (m, 0)),
            w_spec,
            pl.BlockSpec((tm, D_MODEL), lambda m: (m, 0)),
            vec, vec,
        ],
        out_specs=[pl.BlockSpec((tm, D_MODEL), lambda m: (m, 0)),
                   pl.BlockSpec((tm, D_MODEL), lambda m: (m, 0))],
        out_shape=[jax.ShapeDtypeStruct((SEQ, D_MODEL), F32),
                   jax.ShapeDtypeStruct((SEQ, D_MODEL), BF16)],
        scratch_shapes=scratch,
        compiler_params=_params("arbitrary"),
        name=name,
    )(y, w, x, g, b)


def _ffn_in_kernel(x_ref, wg_ref, wu_ref, wo_ref, o_ref, wo_b_ref, wg_s, wu_s):
    m = pl.program_id(1)

    @pl.when(m == 0)
    def _():
        wg_s[...] = wg_ref[...].astype(BF16)
        wu_s[...] = wu_ref[...].astype(BF16)

    @pl.when((m == FFN_WO_PHASE) | ((pl.program_id(0) == 0) & (m == 0)))
    def _():
        wo_b_ref[...] = wo_ref[...].astype(BF16)

    for r in range(FFN_TM // ROW_CHUNK):
        rows = slice(r * ROW_CHUNK, (r + 1) * ROW_CHUNK)
        x = x_ref[rows, :]
        gate = jnp.dot(x, wg_s[...], preferred_element_type=F32)
        up = jnp.dot(x, wu_s[...], preferred_element_type=F32)
        o_ref[rows, :] = (jax.nn.silu(gate) * up).astype(o_ref.dtype)


def _ffn_in(xb, w_ffn_in, w_ffn_out, layer):
    tm, tn = FFN_TM, FFN_TN
    nb = FFN_HIDDEN // tn

    def _wo_block(n, m):
        return jnp.minimum(n + (m >= FFN_WO_PHASE).astype(jnp.int32), nb - 1)

    return pl.pallas_call(
        _ffn_in_kernel,
        grid=(nb, SEQ // tm),
        in_specs=[
            pl.BlockSpec((tm, D_MODEL), lambda n, m: (m, 0)),
            pl.BlockSpec((None, D_MODEL, tn), lambda n, m: (layer, 0, n)),
            pl.BlockSpec((None, D_MODEL, tn), lambda n, m: (layer, 0, nb + n)),
            pl.BlockSpec((None, tn, D_MODEL), lambda n, m: (layer, _wo_block(n, m), 0)),
        ],
        out_specs=[pl.BlockSpec((tm, tn), lambda n, m: (m, n)),
                   pl.BlockSpec((tn, D_MODEL), lambda n, m: (_wo_block(n, m), 0))],
        out_shape=[jax.ShapeDtypeStruct((SEQ, FFN_HIDDEN), BF16),
                   jax.ShapeDtypeStruct((FFN_HIDDEN, D_MODEL), BF16)],
        scratch_shapes=[pltpu.VMEM((D_MODEL, tn), BF16), pltpu.VMEM((D_MODEL, tn), BF16)],
        compiler_params=_params("arbitrary", "arbitrary"),
        name="ffn_in",
    )(xb, w_ffn_in, w_ffn_in, w_ffn_out)


def _rope_tables():
    pos = jnp.arange(SEQ, dtype=F32)

    def cs(dim):
        inv = 1.0 / (ROPE_THETA ** (jnp.arange(0, dim, 2, dtype=F32) / dim))
        ang = pos[:, None] * inv[None, :]
        cos, sin = jnp.cos(ang), jnp.sin(ang)
        return jnp.concatenate([cos, cos], axis=-1), jnp.concatenate([sin, sin], axis=-1)

    cos_a, sin_a = cs(HEAD_DIM)
    half = HEAD_DIM // 2
    sin_a = jnp.concatenate([-sin_a[:, :half], sin_a[:, half:]], axis=-1)
    cos_b, sin_b = cs(B_QK_DIM)
    cos_b2 = jnp.concatenate([cos_b, cos_b], axis=-1)
    sin_b2 = jnp.concatenate([sin_b, sin_b], axis=-1)
    first_half = (np.arange(HEAD_DIM) % B_QK_DIM) < (B_QK_DIM // 2)
    sin_lo = jnp.where(first_half[None, :], -sin_b2, 0.0)
    sin_hi = jnp.where(first_half[None, :], 0.0, sin_b2)
    return cos_a, sin_a, cos_b2, sin_lo, sin_hi


def _mask_a():
    qc = (np.arange(A_TQ) // CHUNK)[:, None]
    kc = (np.arange(A_WINDOW + A_TQ) // CHUNK)[None, :]
    ok = (kc >= qc) & (kc <= qc + A_WINDOW // CHUNK)
    m = np.where(ok, 0.0, NEG_INF).astype(np.float32)
    return jnp.asarray(np.tile(m, (A_GROUP, 1)))


def _bias_diagonals(rel_bias):
    nq, nk = C_TQ, C_KBLOCKS * C_TQ
    rb = rel_bias.astype(F32)
    n_clip = (nk - nq) - REL_CLIP
    far = jnp.broadcast_to(rb[..., -1:], rb.shape[:-1] + (n_clip,))
    near = rb[..., REL_CLIP - (nq - 1):][..., ::-1]
    d_nonneg = jnp.concatenate([far, near], axis=-1)
    assert d_nonneg.shape[-1] == nk
    d_neg = jnp.broadcast_to(rb[..., -1:], rb.shape[:-1] + (nq,))
    return jnp.concatenate([d_nonneg, d_neg], axis=-1)


def kernel(x, w_in, sinks, lambda_q1, lambda_k1, lambda_q2, lambda_k2, diff_norm_g, rel_bias,
           w_br_a, w_br_b, w_br_c, w_gate, b_gate, w_out, ln1_g, ln1_b,
           w_ffn_in, w_ffn_out, ln2_g, ln2_b):
    assert x.shape == (1, SEQ, D_MODEL)
    tabs = _rope_tables()
    mask_a = _mask_a()
    xf = x.reshape(SEQ, D_MODEL)
    xb = xf.astype(BF16)
    lam_vecs = jnp.stack([lambda_q1, lambda_k1, lambda_q2, lambda_k2], axis=1).astype(F32)
    sub_g = diff_norm_g.astype(F32).reshape(DEPTH, HEAD_DIM, 1)
    bias_diag = _bias_diagonals(rel_bias)
    b_gate3 = b_gate.reshape(DEPTH, 1, -1)
    vec3 = lambda v: v.reshape(DEPTH, 1, D_MODEL)
    ln1_g, ln1_b, ln2_g, ln2_b = vec3(ln1_g), vec3(ln1_b), vec3(ln2_g), vec3(ln2_b)
    for l in range(DEPTH):
        lam_init = 0.8 - 0.6 * math.exp(-0.3 * l)
        h = _in_proj(xb, w_in, l, tabs)
        ya, yc = _attn_ac(h, sinks, mask_a, bias_diag, l)
        yb = _attn_b(h, lam_vecs, sub_g, l, lam_init)
        mix = _mix(xb, ya, yb, yc, w_gate, b_gate3, w_br_a, w_br_b, w_br_c, l)
        xf, xb = _matmul_ln(mix, w_out, xf, ln1_g, ln1_b, l, OUT_TM, "out_ln")
        f, w_ffn_out_b = _ffn_in(xb, w_ffn_in, w_ffn_out, l)
        xf, xb = _matmul_ln(f, w_ffn_out_b, xf, ln2_g, ln2_b, l, FFN_OUT_TM, "ffn_out")
    return xf.reshape(1, SEQ, D_MODEL)
```

```python
import functools
import math

import jax
import jax.numpy as jnp
import numpy as np
from jax import lax
from jax.experimental import pallas as pl
from jax.experimental.pallas import tpu as pltpu

D_MODEL = 2048
SEQ = 8192
DEPTH = 4
CHUNK = 64
HEAD_DIM = 128
A_Q_HEADS = 8
A_KV_HEADS = 2
A_GROUP = A_Q_HEADS // A_KV_HEADS
B_HEADS = 4
B_QK_DIM = 64
C_HEADS = 4
C_PREV_CHUNKS = 8
REL_CLIP = 256
FFN_HIDDEN = 5632
IN_WIDTH = 4608
ROPE_THETA = 10000.0
LN_EPS = 1e-5
DEEPNORM_ALPHA = (2 * DEPTH) ** 0.25
NEG_INF = -1e30
LOG2_E = math.log2(math.e)

BF16 = jnp.bfloat16
F32 = jnp.float32

VMEM_LIMIT_BYTES = 58 * 1024 * 1024

AQ_OFF, AK_OFF, AV_OFF = 0, 1024, 1280
BQ_OFF, BK_OFF, BV_OFF = 1536, 2048, 2560
CQ_OFF, CK_OFF, CV_OFF = 3072, 3584, 4096

LN_ROW_CHUNK = 128
ROW_CHUNK = 256
PROJ_TM, PROJ_TN = 1024, 1536
MIX_TM, MIX_TN = 512, 512
OUT_TM = 512
FFN_TM, FFN_TN = 1024, 512
FFN_WO_PHASE = 4
FFN_OUT_TM = 512
A_TQ = 256
A_WINDOW = 128
B_TQ = 512
B_TK = 512
B_ONES_ROWS = 16
C_TQ = 256
C_KBLOCKS = (C_PREV_CHUNKS * CHUNK) // C_TQ + 1


def _params(*sem):
    return pltpu.CompilerParams(dimension_semantics=sem, vmem_limit_bytes=VMEM_LIMIT_BYTES)


def _rope_a(t, cos, sin_signed):
    return t * cos + pltpu.roll(t, HEAD_DIM // 2, 1) * sin_signed


def _rope_b(t, cos2, sin_lo, sin_hi):
    half = B_QK_DIM // 2
    return t * cos2 + pltpu.roll(t, HEAD_DIM - half, 1) * sin_lo + pltpu.roll(t, half, 1) * sin_hi


def _in_proj_kernel(x_ref, w_ref, cosa_ref, sina_ref, cosb_ref, sinb_lo_ref, sinb_hi_ref,
                    o_ref, wb_ref):
    n = pl.program_id(0)
    m = pl.program_id(1)

    @pl.when(m == 0)
    def _():
        wb_ref[...] = w_ref[...].astype(BF16)

    a_scale = HEAD_DIM ** -0.5 * LOG2_E
    b_scale = B_QK_DIM ** -0.5 * LOG2_E

    def run(epilogue):
        for r in range(PROJ_TM // ROW_CHUNK):
            rows = slice(r * ROW_CHUNK, (r + 1) * ROW_CHUNK)
            acc = jnp.dot(x_ref[rows, :], wb_ref[...], preferred_element_type=F32)
            epilogue(acc, rows)

    def head_cols(acc, rows, lo, hi, fn, scale):
        for j in range(lo, hi):
            sl = slice(j * HEAD_DIM, (j + 1) * HEAD_DIM)
            r = fn(acc[:, sl])
            if scale != 1.0:
                r = r * scale
            o_ref[rows, sl] = r.astype(o_ref.dtype)

    def plain_cols(acc, rows, lo, scale=1.0):
        t = acc[:, lo:]
        if scale != 1.0:
            t = t * scale
        o_ref[rows, lo:] = t.astype(o_ref.dtype)

    def mixer_a(acc, rows):
        cos, sin = cosa_ref[rows, :], sina_ref[rows, :]
        rope = lambda t: _rope_a(t, cos, sin)
        head_cols(acc, rows, 0, A_Q_HEADS, rope, a_scale)
        head_cols(acc, rows, A_Q_HEADS, A_Q_HEADS + A_KV_HEADS, rope, 1.0)
        plain_cols(acc, rows, (A_Q_HEADS + A_KV_HEADS) * HEAD_DIM)

    def mixer_b(acc, rows):
        cos, lo_, hi_ = cosb_ref[rows, :], sinb_lo_ref[rows, :], sinb_hi_ref[rows, :]
        rope = lambda t: _rope_b(t, cos, lo_, hi_)
        head_cols(acc, rows, 0, B_HEADS, rope, b_scale)
        head_cols(acc, rows, B_HEADS, 2 * B_HEADS, rope, 1.0)
        plain_cols(acc, rows, 2 * B_HEADS * HEAD_DIM)

    def mixer_c(acc, rows):
        o_ref[rows, :C_HEADS * HEAD_DIM] = (acc[:, :C_HEADS * HEAD_DIM] * a_scale).astype(o_ref.dtype)
        plain_cols(acc, rows, C_HEADS * HEAD_DIM)

    pl.when(n == 0)(lambda: run(mixer_a))
    pl.when(n == 1)(lambda: run(mixer_b))
    pl.when(n == 2)(lambda: run(mixer_c))


def _in_proj(xb, w_in, layer, tabs):
    tm, tn = PROJ_TM, PROJ_TN
    rope_spec = pl.BlockSpec((tm, HEAD_DIM), lambda n, m: (m, 0))
    return pl.pallas_call(
        _in_proj_kernel,
        grid=(IN_WIDTH // tn, SEQ // tm),
        in_specs=[
            pl.BlockSpec((tm, D_MODEL), lambda n, m: (m, 0)),
            pl.BlockSpec((None, D_MODEL, tn), lambda n, m: (layer, 0, n)),
            rope_spec, rope_spec, rope_spec, rope_spec, rope_spec,
        ],
        out_specs=pl.BlockSpec((tm, tn), lambda n, m: (m, n)),
        out_shape=jax.ShapeDtypeStruct((SEQ, IN_WIDTH), BF16),
        scratch_shapes=[pltpu.VMEM((D_MODEL, tn), BF16)],
        compiler_params=_params("arbitrary", "arbitrary"),
        name="in_proj",
    )(xb, w_in, *tabs)


def _attn_a_stages(sink_ref, q_ref, kp_ref, kc_ref, vp_ref, vc_ref, mask_ref, o_ref, sa_ref, layer):
    i = pl.program_id(0)
    has_prev = i > 0
    dn = (((1,), (1,)), ((), ()))
    hcols = lambda hd: slice(hd * HEAD_DIM, (hd + 1) * HEAD_DIM)
    for g in range(A_KV_HEADS):
        q = jnp.concatenate([q_ref[:, hcols(A_GROUP * g + j)] for j in range(A_GROUP)], axis=0)
        sa_ref[g, :, :A_WINDOW] = lax.dot_general(q, kp_ref[:, hcols(g)], dn, preferred_element_type=F32)
        sa_ref[g, :, A_WINDOW:] = lax.dot_general(q, kc_ref[:, hcols(g)], dn, preferred_element_type=F32)

    def stage(g):
        s_prev, s_cur = sa_ref[g, :, :A_WINDOW], sa_ref[g, :, A_WINDOW:]
        s_prev = jnp.where(has_prev, s_prev + mask_ref[:, :A_WINDOW], NEG_INF)
        s_cur = s_cur + mask_ref[:, A_WINDOW:]
        sink = jnp.concatenate(
            [jnp.full((A_TQ, HEAD_DIM), sink_ref[layer, A_GROUP * g + j] * LOG2_E, F32)
             for j in range(A_GROUP)], axis=0)
        folded = functools.reduce(
            jnp.maximum, [s_prev] + [s_cur[:, c:c + HEAD_DIM] for c in range(0, A_TQ, HEAD_DIM)])
        mx = jnp.maximum(jnp.broadcast_to(folded.max(-1, keepdims=True), sink.shape), sink)
        p_prev = jnp.exp2(s_prev - mx)
        p_cur = jnp.exp2(s_cur - jnp.concatenate([mx] * (A_TQ // HEAD_DIM), axis=1))
        ones = lambda rows: jnp.ones((rows, HEAD_DIM), BF16)
        v_prev = jnp.concatenate([vp_ref[:, hcols(g)], ones(A_WINDOW)], axis=1)
        v_cur = jnp.concatenate([vc_ref[:, hcols(g)], ones(A_TQ)], axis=1)
        o_ext = (jnp.dot(p_prev.astype(BF16), v_prev, preferred_element_type=F32)
                 + jnp.dot(p_cur.astype(BF16), v_cur, preferred_element_type=F32))
        denom = o_ext[:, HEAD_DIM:] + jnp.exp2(sink - mx)
        o = o_ext[:, :HEAD_DIM] * (1.0 / denom)
        for j in range(A_GROUP):
            o_ref[:, hcols(A_GROUP * g + j)] = o[j * A_TQ:(j + 1) * A_TQ].astype(o_ref.dtype)

    return [functools.partial(stage, g) for g in range(A_KV_HEADS)]


def _attn_a_specs():
    tq, win = A_TQ, A_WINDOW
    kv_width = A_KV_HEADS * HEAD_DIM
    kblk, vblk = AK_OFF // kv_width, AV_OFF // kv_width
    prev = lambda i: jnp.maximum(i * (tq // win) - 1, 0)
    in_specs = [
        pl.BlockSpec(memory_space=pltpu.SMEM),
        pl.BlockSpec((tq, A_Q_HEADS * HEAD_DIM), lambda i: (i, 0)),
        pl.BlockSpec((win, kv_width), lambda i: (prev(i), kblk)),
        pl.BlockSpec((tq, kv_width), lambda i: (i, kblk)),
        pl.BlockSpec((win, kv_width), lambda i: (prev(i), vblk)),
        pl.BlockSpec((tq, kv_width), lambda i: (i, vblk)),
        pl.BlockSpec((A_GROUP * tq, win + tq), lambda i: (0, 0)),
    ]
    out_spec = pl.BlockSpec((tq, A_Q_HEADS * HEAD_DIM), lambda i: (i, 0))
    return in_specs, out_spec, jax.ShapeDtypeStruct((SEQ, A_Q_HEADS * HEAD_DIM), BF16)


def _attn_b_kernel(lam_ref, g_ref, q_ref, qn_ref, k_ref, v_ref, o_ref,
                   vt_s, qst_s, acc_s, s_s, sn_s, m_s, *, lam_init):
    i = pl.program_id(0)
    tq, tk = B_TQ, B_TK
    assert tq == tk
    hcols = lambda hd: slice(hd * HEAD_DIM, (hd + 1) * HEAD_DIM)
    cur = i % 2

    def stage_queries(src_ref, half):
        feat = lax.broadcasted_iota(jnp.int32, (HEAD_DIM, tq), 0)
        for hd in range(B_HEADS):
            qt = src_ref[:, hcols(hd)].astype(F32).T
            qst_s[half, hd, :, :tq] = jnp.where(feat < B_QK_DIM, qt, 0.0).astype(BF16)
            qst_s[half, hd, :, tq:] = jnp.where(feat >= B_QK_DIM, qt, 0.0).astype(BF16)

    def scores(j, hd, half):
        kj = k_ref[pl.ds(pl.multiple_of(j * tk, tk), tk), hcols(hd)]
        return jnp.dot(kj, qst_s[half, hd], preferred_element_type=F32)

    for hd in range(B_HEADS):
        vt_s[hd, i, :HEAD_DIM, :] = v_ref[:, hcols(hd)].astype(F32).T.astype(BF16)
        vt_s[hd, i, HEAD_DIM:, :] = jnp.ones((B_ONES_ROWS, tk), BF16)

    @pl.when(i == 0)
    def _():
        stage_queries(q_ref, 0)
        for hd in range(B_HEADS):
            s_s[0, hd] = scores(0, hd, 0)

    stage_queries(qn_ref, 1 - cur)
    acc_s[...] = jnp.zeros_like(acc_s)
    m_s[...] = jnp.full(m_s.shape, NEG_INF, F32)

    def block(j, src, dst, diagonal=False):
        for hd in range(B_HEADS):
            if diagonal:
                dst[hd] = scores(0, hd, 1 - cur)
            else:
                dst[hd] = scores(j + 1, hd, cur)
            s = src[hd]
            if diagonal:
                kc = lax.broadcasted_iota(jnp.int32, s.shape, 0) // CHUNK
                qc = (lax.broadcasted_iota(jnp.int32, s.shape, 1) % tq) // CHUNK
                s = jnp.where(kc <= qc, s, NEG_INF)
            m_old = m_s[hd]
            m_new = jnp.maximum(m_old, s.max(axis=0, keepdims=True))
            m_s[hd] = m_new
            a = jnp.exp2(m_old - m_new)
            p = jnp.exp2(s - m_new).astype(BF16)
            acc_s[hd] = a * acc_s[hd] + jnp.dot(vt_s[hd, j], p, preferred_element_type=F32)

    half0, half1 = s_s.at[0], s_s.at[1]

    @pl.when(i == 0)
    def _():
        block(0, half0, sn_s, diagonal=True)

    @pl.when(i > 0)
    def _():
        block(0, sn_s, half0)

    def pair(t, carry):
        block(2 * t + 1, half0, half1)
        block(2 * t + 2, half1, half0)
        return carry

    n_mid = jnp.maximum(i - 1, 0)
    lax.fori_loop(0, n_mid // 2, pair, 0)

    @pl.when((i > 0) & (n_mid % 2 == 0))
    def _():
        block(i, half0, sn_s, diagonal=True)

    @pl.when((i > 0) & (n_mid % 2 == 1))
    def _():
        block(i - 1, half0, half1)
        block(i, half1, sn_s, diagonal=True)

    lam = (jnp.exp(jnp.sum(lam_ref[0:1, :] * lam_ref[1:2, :], axis=-1, keepdims=True))
           - jnp.exp(jnp.sum(lam_ref[2:3, :] * lam_ref[3:4, :], axis=-1, keepdims=True))
           + lam_init)
    for hd in range(B_HEADS):
        acc = acc_s[hd]
        o_all = acc[:HEAD_DIM] * (1.0 / acc[HEAD_DIM:HEAD_DIM + 1])
        o = o_all[:, :tq] - lam * o_all[:, tq:]
        o = o * lax.rsqrt(jnp.mean(jnp.square(o), axis=0, keepdims=True) + LN_EPS)
        o = o * g_ref[...] * (1.0 - lam_init)
        o_ref[:, hcols(hd)] = o.T.astype(o_ref.dtype)


def _attn_b(h, lam_vecs, sub_g, layer, lam_init):
    tq, tk = B_TQ, B_TK
    width = B_HEADS * HEAD_DIM
    n_steps = SEQ // tq
    return pl.pallas_call(
        functools.partial(_attn_b_kernel, lam_init=lam_init),
        grid=(n_steps,),
        in_specs=[
            pl.BlockSpec((None, 4, B_QK_DIM), lambda i: (layer, 0, 0)),
            pl.BlockSpec((None, HEAD_DIM, 1), lambda i: (layer, 0, 0)),
            pl.BlockSpec((tq, width), lambda i: (i, BQ_OFF // width)),
            pl.BlockSpec((tq, width), lambda i: (jnp.minimum(i + 1, n_steps - 1), BQ_OFF // width)),
            pl.BlockSpec((SEQ, width), lambda i: (0, BK_OFF // width), pipeline_mode=pl.Buffered(1)),
            pl.BlockSpec((tk, width), lambda i: (i, BV_OFF // width)),
        ],
        out_specs=pl.BlockSpec((tq, width), lambda i: (i, 0)),
        out_shape=jax.ShapeDtypeStruct((SEQ, width), BF16),
        scratch_shapes=[
            pltpu.VMEM((B_HEADS, SEQ // tk, HEAD_DIM + B_ONES_ROWS, tk), BF16),
            pltpu.VMEM((2, B_HEADS, HEAD_DIM, 2 * tq), BF16),
            pltpu.VMEM((B_HEADS, HEAD_DIM + B_ONES_ROWS, 2 * tq), F32),
            pltpu.VMEM((2, B_HEADS, tk, 2 * tq), F32),
            pltpu.VMEM((B_HEADS, tk, 2 * tq), F32),
            pltpu.VMEM((B_HEADS, 1, 2 * tq), F32),
        ],
        compiler_params=_params("arbitrary"),
        name="attn_b",
    )(lam_vecs, sub_g, h, h, h, h)


def _attn_c_build_bias(diag_ref, bias_ref):
    nk = C_KBLOCKS * C_TQ
    qc = lax.broadcasted_iota(jnp.int32, (C_TQ, nk), 0) // CHUNK
    kc = lax.broadcasted_iota(jnp.int32, (C_TQ, nk), 1) // CHUNK
    band = (kc >= qc) & (kc <= qc + C_PREV_CHUNKS)
    for hd in range(C_HEADS):
        rows = jnp.broadcast_to(diag_ref[hd:hd + 1, :] * LOG2_E, (C_TQ, C_TQ + nk))
        skew = pltpu.roll(rows, 0, 1, stride=1, stride_axis=0)
        bias_ref[hd] = jnp.where(band, skew[:, :nk], NEG_INF)


def _attn_c_stages(q_ref, k_refs, v_refs, o_ref, bias_ref, sc_ref):
    i = pl.program_id(0)
    dn = (((1,), (1,)), ((), ()))
    hcols = lambda hd: slice(hd * HEAD_DIM, (hd + 1) * HEAD_DIM)
    for hd in range(C_HEADS):
        for b in range(C_KBLOCKS):
            sc_ref[hd, :, b * C_TQ:(b + 1) * C_TQ] = lax.dot_general(
                q_ref[:, hcols(hd)], k_refs[b][:, hcols(hd)], dn, preferred_element_type=F32)

    def stage(hd):
        hsl = hcols(hd)
        s = []
        for b in range(C_KBLOCKS):
            blk = slice(b * C_TQ, (b + 1) * C_TQ)
            sb = sc_ref[hd, :, blk] + bias_ref[hd, :, blk]
            s.append(jnp.where(i - (C_KBLOCKS - 1) + b >= 0, sb, NEG_INF))
        mx = functools.reduce(jnp.maximum, [sb.max(-1, keepdims=True) for sb in s])
        p = [jnp.exp2(sb - mx) for sb in s]
        ones = jnp.ones((C_TQ, HEAD_DIM), BF16)
        o_ext = functools.reduce(
            lambda a, b_: a + b_,
            [jnp.dot(p[b].astype(BF16), jnp.concatenate([v_refs[b][:, hsl], ones], axis=1),
                     preferred_element_type=F32) for b in range(C_KBLOCKS)])
        o_ref[:, hsl] = (o_ext[:, :HEAD_DIM] * (1.0 / o_ext[:, HEAD_DIM:])).astype(o_ref.dtype)

    return [functools.partial(stage, hd) for hd in range(C_HEADS)]


def _attn_ac_kernel(*refs, layer):
    n_a = 7
    a_in, c_in = refs[:n_a], refs[n_a:n_a + 2 + 2 * C_KBLOCKS]
    oa_ref, oc_ref, bias_ref, sa_ref, sc_ref = refs[n_a + 2 + 2 * C_KBLOCKS:]
    pl.when(pl.program_id(0) == 0)(lambda: _attn_c_build_bias(c_in[-1], bias_ref))
    c_stages = _attn_c_stages(c_in[0], c_in[1:1 + C_KBLOCKS], c_in[1 + C_KBLOCKS:1 + 2 * C_KBLOCKS],
                              oc_ref, bias_ref, sc_ref)
    a_stages = _attn_a_stages(*a_in, oa_ref, sa_ref, layer)
    per_a = len(c_stages) // len(a_stages)
    for g, a_stage in enumerate(a_stages):
        for c_stage in c_stages[g * per_a:(g + 1) * per_a]:
            c_stage()
        a_stage()


def _attn_ac(h, sinks, mask_a, bias_diag, layer):
    assert A_TQ == C_TQ
    tq = C_TQ
    c_width = C_HEADS * HEAD_DIM
    qb, kb, vb = CQ_OFF // c_width, CK_OFF // c_width, CV_OFF // c_width

    def kv_spec(b, col):
        return pl.BlockSpec((tq, c_width), lambda i: (jnp.maximum(i - (C_KBLOCKS - 1) + b, 0), col))

    a_in_specs, a_out_spec, a_out_shape = _attn_a_specs()
    c_in_specs = ([pl.BlockSpec((tq, c_width), lambda i: (i, qb))]
                  + [kv_spec(b, kb) for b in range(C_KBLOCKS)]
                  + [kv_spec(b, vb) for b in range(C_KBLOCKS)]
                  + [pl.BlockSpec((None, C_HEADS, (C_KBLOCKS + 1) * tq), lambda i: (layer, 0, 0))])
    return pl.pallas_call(
        functools.partial(_attn_ac_kernel, layer=layer),
        grid=(SEQ // tq,),
        in_specs=a_in_specs + c_in_specs,
        out_specs=[a_out_spec, pl.BlockSpec((tq, c_width), lambda i: (i, 0))],
        out_shape=[a_out_shape, jax.ShapeDtypeStruct((SEQ, c_width), BF16)],
        scratch_shapes=[pltpu.VMEM((C_HEADS, tq, C_KBLOCKS * tq), F32),
                        pltpu.VMEM((A_KV_HEADS, A_GROUP * tq, A_WINDOW + tq), F32),
                        pltpu.VMEM((C_HEADS, tq, C_KBLOCKS * tq), F32)],
        compiler_params=_params("arbitrary"),
        name="attn_ac",
    )(sinks, h, h, h, h, h, mask_a, h, *([h] * (2 * C_KBLOCKS)), bias_diag)


def _mix_kernel(x_ref, ya_ref, yb_ref, yc_ref, wga_ref, wgb_ref, wgc_ref, ba_ref, bb_ref, bc_ref,
                wa_ref, wb_ref, wc_ref, o_ref, wg_s, wa_s, wb_s, wc_s):
    m = pl.program_id(1)

    @pl.when(m == 0)
    def _():
        wg_s[0] = wga_ref[...].astype(BF16)
        wg_s[1] = wgb_ref[...].astype(BF16)
        wg_s[2] = wgc_ref[...].astype(BF16)
        wa_s[...] = wa_ref[...].astype(BF16)
        wb_s[...] = wb_ref[...].astype(BF16)
        wc_s[...] = wc_ref[...].astype(BF16)

    for r in range(MIX_TM // ROW_CHUNK):
        rows = slice(r * ROW_CHUNK, (r + 1) * ROW_CHUNK)
        x = x_ref[rows, :]

        def branch(idx, b_ref, y_ref, w_s):
            gate = jax.nn.sigmoid(jnp.dot(x, wg_s[idx], preferred_element_type=F32) + b_ref[...])
            return gate * jnp.dot(y_ref[rows, :], w_s[...], preferred_element_type=F32)

        mix = branch(0, ba_ref, ya_ref, wa_s) + branch(1, bb_ref, yb_ref, wb_s) + branch(2, bc_ref, yc_ref, wc_s)
        o_ref[rows, :] = mix.astype(o_ref.dtype)


def _mix(xb, ya, yb, yc, w_gate, b_gate, w_br_a, w_br_b, w_br_c, layer):
    tm, tn = MIX_TM, MIX_TN
    nb = D_MODEL // tn
    row = lambda width: pl.BlockSpec((tm, width), lambda n, m: (m, 0))
    gate_w = lambda k: pl.BlockSpec((None, D_MODEL, tn), lambda n, m: (layer, 0, k * nb + n))
    gate_b = lambda k: pl.BlockSpec((None, 1, tn), lambda n, m: (layer, 0, k * nb + n))
    br_w = lambda width: pl.BlockSpec((None, width, tn), lambda n, m: (layer, 0, n))
    return pl.pallas_call(
        _mix_kernel,
        grid=(nb, SEQ // tm),
        in_specs=[row(D_MODEL), row(1024), row(512), row(512),
                  gate_w(0), gate_w(1), gate_w(2), gate_b(0), gate_b(1), gate_b(2),
                  br_w(1024), br_w(512), br_w(512)],
        out_specs=pl.BlockSpec((tm, tn), lambda n, m: (m, n)),
        out_shape=jax.ShapeDtypeStruct((SEQ, D_MODEL), BF16),
        scratch_shapes=[pltpu.VMEM((3, D_MODEL, tn), BF16), pltpu.VMEM((1024, tn), BF16),
                        pltpu.VMEM((512, tn), BF16), pltpu.VMEM((512, tn), BF16)],
        compiler_params=_params("arbitrary", "arbitrary"),
        name="mix",
    )(xb, ya, yb, yc, w_gate, w_gate, w_gate, b_gate, b_gate, b_gate, w_br_a, w_br_b, w_br_c)


def _matmul_ln_kernel(y_ref, w_ref, x_ref, g_ref, b_ref, of_ref, ob_ref, *wb_scratch, chunk):
    if wb_scratch:
        wb_ref, = wb_scratch

        @pl.when(pl.program_id(0) == 0)
        def _():
            wb_ref[...] = w_ref[...].astype(BF16)
    else:
        wb_ref = w_ref
    for r in range(y_ref.shape[0] // chunk):
        rows = slice(r * chunk, (r + 1) * chunk)
        z = DEEPNORM_ALPHA * x_ref[rows, :] + jnp.dot(y_ref[rows, :], wb_ref[...], preferred_element_type=F32)
        mu = jnp.mean(z, axis=-1, keepdims=True)
        zc = z - mu
        var = jnp.mean(jnp.square(zc), axis=-1, keepdims=True)
        out = zc * lax.rsqrt(var + LN_EPS) * g_ref[...] + b_ref[...]
        of_ref[rows, :] = out
        ob_ref[rows, :] = out.astype(ob_ref.dtype)


def _matmul_ln(y, w, x, g, b, layer, tm, name):
    k = y.shape[1]
    vec = pl.BlockSpec((None, 1, D_MODEL), lambda m: (layer, 0, 0))
    if w.ndim == 3:
        w_spec = pl.BlockSpec((None, k, D_MODEL), lambda m: (layer, 0, 0), pipeline_mode=pl.Buffered(1))
        scratch = [pltpu.VMEM((k, D_MODEL), BF16)]
    else:
        w_spec = pl.BlockSpec((k, D_MODEL), lambda m: (0, 0), pipeline_mode=pl.Buffered(1))
        scratch = []
    return pl.pallas_call(
        functools.partial(_matmul_ln_kernel, chunk=LN_ROW_CHUNK),
        grid=(SEQ // tm,),
        in_specs=[
            pl.BlockSpec((tm, k), lambda m: (m, 0)),
            w_spec,
            pl.BlockSpec((tm, D_MODEL), lambda m: (m, 0)),
            vec, vec,
        ],
        out_specs=[pl.BlockSpec((tm, D_MODEL), lambda m: (m, 0)),
                   pl.BlockSpec((tm, D_MODEL), lambda m: (m, 0))],
        out_shape=[jax.ShapeDtypeStruct((SEQ, D_MODEL), F32),
                   jax.ShapeDtypeStruct((SEQ, D_MODEL), BF16)],
        scratch_shapes=scratch,
        compiler_params=_params("arbitrary"),
        name=name,
    )(y, w, x, g, b)


def _ffn_in_kernel(x_ref, wg_ref, wu_ref, wo_ref, o_ref, wo_b_ref, wg_s, wu_s):
    m = pl.program_id(1)

    @pl.when(m == 0)
    def _():
        wg_s[...] = wg_ref[...].astype(BF16)
        wu_s[...] = wu_ref[...].astype(BF16)

    @pl.when((m == FFN_WO_PHASE) | ((pl.program_id(0) == 0) & (m == 0)))
    def _():
        wo_b_ref[...] = wo_ref[...].astype(BF16)

    for r in range(FFN_TM // ROW_CHUNK):
        rows = slice(r * ROW_CHUNK, (r + 1) * ROW_CHUNK)
        x = x_ref[rows, :]
        gate = jnp.dot(x, wg_s[...], preferred_element_type=F32)
        up = jnp.dot(x, wu_s[...], preferred_element_type=F32)
        o_ref[rows, :] = (jax.nn.silu(gate) * up).astype(o_ref.dtype)


def _ffn_in(xb, w_ffn_in, w_ffn_out, layer):
    tm, tn = FFN_TM, FFN_TN
    nb = FFN_HIDDEN // tn

    def _wo_block(n, m):
        return jnp.minimum(n + (m >= FFN_WO_PHASE).astype(jnp.int32), nb - 1)

    return pl.pallas_call(
        _ffn_in_kernel,
        grid=(nb, SEQ // tm),
        in_specs=[
            pl.BlockSpec((tm, D_MODEL), lambda n, m: (m, 0)),
            pl.BlockSpec((None, D_MODEL, tn), lambda n, m: (layer, 0, n)),
            pl.BlockSpec((None, D_MODEL, tn), lambda n, m: (layer, 0, nb + n)),
            pl.BlockSpec((None, tn, D_MODEL), lambda n, m: (layer, _wo_block(n, m), 0)),
        ],
        out_specs=[pl.BlockSpec((tm, tn), lambda n, m: (m, n)),
                   pl.BlockSpec((tn, D_MODEL), lambda n, m: (_wo_block(n, m), 0))],
        out_shape=[jax.ShapeDtypeStruct((SEQ, FFN_HIDDEN), BF16),
                   jax.ShapeDtypeStruct((FFN_HIDDEN, D_MODEL), BF16)],
        scratch_shapes=[pltpu.VMEM((D_MODEL, tn), BF16), pltpu.VMEM((D_MODEL, tn), BF16)],
        compiler_params=_params("arbitrary", "arbitrary"),
        name="ffn_in",
    )(xb, w_ffn_in, w_ffn_in, w_ffn_out)


def _rope_tables():
    pos = jnp.arange(SEQ, dtype=F32)

    def cs(dim):
        inv = 1.0 / (ROPE_THETA ** (jnp.arange(0, dim, 2, dtype=F32) / dim))
        ang = pos[:, None] * inv[None, :]
        cos, sin = jnp.cos(ang), jnp.sin(ang)
        return jnp.concatenate([cos, cos], axis=-1), jnp.concatenate([sin, sin], axis=-1)

    cos_a, sin_a = cs(HEAD_DIM)
    half = HEAD_DIM // 2
    sin_a = jnp.concatenate([-sin_a[:, :half], sin_a[:, half:]], axis=-1)
    cos_b, sin_b = cs(B_QK_DIM)
    cos_b2 = jnp.concatenate([cos_b, cos_b], axis=-1)
    sin_b2 = jnp.concatenate([sin_b, sin_b], axis=-1)
    first_half = (np.arange(HEAD_DIM) % B_QK_DIM) < (B_QK_DIM // 2)
    sin_lo = jnp.where(first_half[None, :], -sin_b2, 0.0)
    sin_hi = jnp.where(first_half[None, :], 0.0, sin_b2)
    return cos_a, sin_a, cos_b2, sin_lo, sin_hi


def _mask_a():
    qc = (np.arange(A_TQ) // CHUNK)[:, None]
    kc = (np.arange(A_WINDOW + A_TQ) // CHUNK)[None, :]
    ok = (kc >= qc) & (kc <= qc + A_WINDOW // CHUNK)
    m = np.where(ok, 0.0, NEG_INF).astype(np.float32)
    return jnp.asarray(np.tile(m, (A_GROUP, 1)))


def _bias_diagonals(rel_bias):
    nq, nk = C_TQ, C_KBLOCKS * C_TQ
    rb = rel_bias.astype(F32)
    n_clip = (nk - nq) - REL_CLIP
    far = jnp.broadcast_to(rb[..., -1:], rb.shape[:-1] + (n_clip,))
    near = rb[..., REL_CLIP - (nq - 1):][..., ::-1]
    d_nonneg = jnp.concatenate([far, near], axis=-1)
    assert d_nonneg.shape[-1] == nk
    d_neg = jnp.broadcast_to(rb[..., -1:], rb.shape[:-1] + (nq,))
    return jnp.concatenate([d_nonneg, d_neg], axis=-1)


def kernel(x, w_in, sinks, lambda_q1, lambda_k1, lambda_q2, lambda_k2, diff_norm_g, rel_bias,
           w_br_a, w_br_b, w_br_c, w_gate, b_gate, w_out, ln1_g, ln1_b,
           w_ffn_in, w_ffn_out, ln2_g, ln2_b):
    assert x.shape == (1, SEQ, D_MODEL)
    tabs = _rope_tables()
    mask_a = _mask_a()
    xf = x.reshape(SEQ, D_MODEL)
    xb = xf.astype(BF16)
    lam_vecs = jnp.stack([lambda_q1, lambda_k1, lambda_q2, lambda_k2], axis=1).astype(F32)
    sub_g = diff_norm_g.astype(F32).reshape(DEPTH, HEAD_DIM, 1)
    bias_diag = _bias_diagonals(rel_bias)
    b_gate3 = b_gate.reshape(DEPTH, 1, -1)
    vec3 = lambda v: v.reshape(DEPTH, 1, D_MODEL)
    ln1_g, ln1_b, ln2_g, ln2_b = vec3(ln1_g), vec3(ln1_b), vec3(ln2_g), vec3(ln2_b)
    for l in range(DEPTH):
        lam_init = 0.8 - 0.6 * math.exp(-0.3 * l)
        h = _in_proj(xb, w_in, l, tabs)
        ya, yc = _attn_ac(h, sinks, mask_a, bias_diag, l)
        yb = _attn_b(h, lam_vecs, sub_g, l, lam_init)
        mix = _mix(xb, ya, yb, yc, w_gate, b_gate3, w_br_a, w_br_b, w_br_c, l)
        xf, xb = _matmul_ln(mix, w_out, xf, ln1_g, ln1_b, l, OUT_TM, "out_ln")
        f, w_ffn_out_b = _ffn_in(xb, w_ffn_in, w_ffn_out, l)
        xf, xb = _matmul_ln(f, w_ffn_out_b, xf, ln2_g, ln2_b, l, FFN_OUT_TM, "ffn_out")
    return xf.reshape(1, SEQ, D_MODEL)
```
